```python
import jax, jax.numpy as jnp
from jax import lax
import numpy as np

D_MODEL = 1024
BATCH = 8
SEQ = 2048
DEPTH = 2
DEC_BATCH = 128
DEC_SEQ = 8
PAST_LEN = 16384
PAGE_SIZE = 128

N_META = 16
N_MIXERS = 2
N_CONV_LAYERS = (DEPTH + 1) // N_MIXERS
N_SSM_LAYERS = DEPTH // N_MIXERS
SC_WIDTH = 3
D_INNER = 2 * D_MODEL
SSM_HEAD_DIM = 64
SSM_HEADS = D_INNER // SSM_HEAD_DIM
SSM_GROUPS = 4
HEADS_PER_GROUP = SSM_HEADS // SSM_GROUPS
D_STATE = 128
SSM_CONV_WIDTH = 4
CONV_DIM = D_INNER + 2 * SSM_GROUPS * D_STATE
SSM_IN_DIM = D_INNER + CONV_DIM + SSM_HEADS
SSM_CHUNK = 128
D_FF = 2816
FFN_CONV_WIDTH = 3
EPS = 1e-5

kernel_name = "hybrid_shortconv_ssd_convffn_step"


def rmsnorm(x, w):
    xf = x.astype(jnp.float32)
    r = lax.rsqrt(jnp.mean(xf * xf, axis=-1, keepdims=True) + EPS)
    return (xf * r).astype(x.dtype) * w


def causal_dwconv(x, prefix, w):
    width = w.shape[0]
    L = x.shape[1]
    xp = jnp.concatenate([prefix.astype(x.dtype), x], axis=1)
    y = xp[:, 0:L] * w[0]
    for k in range(1, width):
        y = y + xp[:, k:k + L] * w[k]
    return y, xp[:, L:]


def short_conv_mixer(h, prefix, w_in, conv_w, w_out):
    b_gate, c_gate, xv = jnp.split(h @ w_in, 3, axis=-1)
    v, new_prefix = causal_dwconv(c_gate * xv, prefix, conv_w)
    return (b_gate * v) @ w_out, new_prefix


def ssd_scan(x, dt, a, bm, cm, s0, chunk):
    b_, L = x.shape[0], x.shape[1]
    nc = L // chunk
    rs = lambda t: t.reshape((b_, nc, chunk) + t.shape[2:])
    x, dt, bm, cm = rs(x), rs(dt), rs(bm), rs(cm)
    cum = jnp.cumsum(dt * a, axis=2)
    seg = cum[:, :, :, None] - cum[:, :, None, :]
    causal = jnp.tril(jnp.ones((chunk, chunk), dtype=bool))[None, None, :, :, None, None]
    decay = jnp.exp(jnp.where(causal, seg, -jnp.inf))
    cb = jnp.einsum('bclgn,bcsgn->bclsg', cm, bm)
    y_diag = jnp.einsum('bclsgr,bcsgrp->bclgrp', cb[..., None] * decay * dt[:, :, None], x)
    decay_end = jnp.exp(cum[:, :, -1:] - cum)
    chunk_states = jnp.einsum('bcsgn,bcsgr,bcsgrp->bcgrpn', bm, decay_end * dt, x)
    chunk_decay = jnp.exp(cum[:, :, -1])

    def step(s, inp):
        st, dc = inp
        return s * dc[..., None, None] + st, s

    s_final, s_start = lax.scan(step, s0, (jnp.moveaxis(chunk_states, 1, 0), jnp.moveaxis(chunk_decay, 1, 0)))
    s_start = jnp.moveaxis(s_start, 0, 1)
    y_off = jnp.einsum('bclgn,bcgrpn,bclgr->bclgrp', cm, s_start, jnp.exp(cum))
    y = (y_diag + y_off).reshape((b_, L) + y_diag.shape[3:])
    return y, s_final


def mamba2_mixer(h, conv_prefix, ssm_state, w_in, conv_w, conv_b, dt_bias, a_log, d_skip, norm_w, w_out, chunk, n_pad):
    b_, L, _ = h.shape
    f32 = jnp.float32
    zxbcdt = h @ w_in
    z = zxbcdt[..., :D_INNER]
    xbc = zxbcdt[..., D_INNER:D_INNER + CONV_DIM]
    dt_raw = zxbcdt[..., D_INNER + CONV_DIM:]
    xbc, new_conv = causal_dwconv(xbc, conv_prefix, conv_w)
    xbc = jax.nn.silu(xbc + conv_b)
    xs = xbc[..., :D_INNER].reshape(b_, L, SSM_GROUPS, HEADS_PER_GROUP, SSM_HEAD_DIM).astype(f32)
    bm = xbc[..., D_INNER:D_INNER + SSM_GROUPS * D_STATE].reshape(b_, L, SSM_GROUPS, D_STATE).astype(f32)
    cm = xbc[..., D_INNER + SSM_GROUPS * D_STATE:].reshape(b_, L, SSM_GROUPS, D_STATE).astype(f32)
    dt = jax.nn.softplus(dt_raw.astype(f32) + dt_bias.astype(f32)).reshape(b_, L, SSM_GROUPS, HEADS_PER_GROUP)
    a = (-jnp.exp(a_log.astype(f32))).reshape(SSM_GROUPS, HEADS_PER_GROUP)
    pad = lambda t: jnp.pad(t, [(0, 0), (n_pad, 0)] + [(0, 0)] * (t.ndim - 2))
    s0 = ssm_state.astype(f32).reshape(b_, SSM_GROUPS, HEADS_PER_GROUP, SSM_HEAD_DIM, D_STATE)
    y, s_new = ssd_scan(pad(xs), pad(dt), a, pad(bm), pad(cm), s0, chunk)
    y = y[:, n_pad:] + d_skip.astype(f32).reshape(SSM_GROUPS, HEADS_PER_GROUP)[..., None] * xs
    y = y.reshape(b_, L, D_INNER) * jax.nn.silu(z.astype(f32))
    yg = y.reshape(b_, L, SSM_GROUPS, D_INNER // SSM_GROUPS)
    yg = yg * lax.rsqrt(jnp.mean(yg * yg, axis=-1, keepdims=True) + EPS)
    y = (yg.reshape(b_, L, D_INNER) * norm_w.astype(f32)).astype(h.dtype)
    s_new = s_new.reshape(b_, SSM_HEADS, SSM_HEAD_DIM, D_STATE).astype(ssm_state.dtype)
    return y @ w_out, new_conv, s_new


def conv_ffn(h, prefix, w_up, w_gate, conv_w, w_down):
    u, new_prefix = causal_dwconv(h @ w_up, prefix, conv_w)
    return (jax.nn.silu(u) * (h @ w_gate)) @ w_down, new_prefix


def run_trunk(x, sc_prev, ssm_conv_prev, ssm_prev, ffn_prev, chunk, n_pad,
              norm_mix, norm_ffn, norm_final, sc_w_in, sc_conv_w, sc_w_out,
              ssm_w_in, ssm_conv_w, ssm_conv_b, ssm_dt_bias, ssm_a_log, ssm_d, ssm_norm_w, ssm_w_out,
              ffn_w_up, ffn_w_gate, ffn_conv_w, ffn_w_down):
    sc_new, ssm_conv_new, ssm_new, ffn_new = [], [], [], []
    for i in range(DEPTH):
        j = i // N_MIXERS
        h = rmsnorm(x, norm_mix[i])
        if i % N_MIXERS == 0:
            m, st = short_conv_mixer(h, sc_prev[j], sc_w_in[j], sc_conv_w[j], sc_w_out[j])
            sc_new.append(st)
        else:
            m, cst, sst = mamba2_mixer(h, ssm_conv_prev[j], ssm_prev[j], ssm_w_in[j], ssm_conv_w[j], ssm_conv_b[j],
                                       ssm_dt_bias[j], ssm_a_log[j], ssm_d[j], ssm_norm_w[j], ssm_w_out[j], chunk, n_pad)
            ssm_conv_new.append(cst)
            ssm_new.append(sst)
        x = x + m
        f, fst = conv_ffn(rmsnorm(x, norm_ffn[i]), ffn_prev[i], ffn_w_up[i], ffn_w_gate[i], ffn_conv_w[i], ffn_w_down[i])
        ffn_new.append(fst)
        x = x + f
    return rmsnorm(x, norm_final), jnp.stack(sc_new), jnp.stack(ssm_conv_new), jnp.stack(ssm_new), jnp.stack(ffn_new)


def setup_inputs(seed: int = 0) -> dict:
    key = jax.random.key(seed)
    ks = jax.random.split(key, 32)
    f32 = jnp.float32
    nrm = lambda k, shape, s: jax.random.normal(k, shape, f32) * s
    dt0 = jnp.exp(jax.random.uniform(ks[20], (N_SSM_LAYERS, SSM_HEADS), f32, np.log(1e-3), np.log(1e-1)))
    return {
        "x_prompt": nrm(ks[0], (BATCH, SEQ, D_MODEL), 1.0),
        "x_sample": nrm(ks[1], (DEC_BATCH, DEC_SEQ, D_MODEL), 1.0),
        "cache_sc": nrm(ks[2], (N_CONV_LAYERS, DEC_BATCH, SC_WIDTH - 1, D_MODEL), 1.0),
        "cache_ssm_conv": nrm(ks[3], (N_SSM_LAYERS, DEC_BATCH, SSM_CONV_WIDTH - 1, CONV_DIM), 1.0),
        "state_ssm": nrm(ks[4], (N_SSM_LAYERS, DEC_BATCH, SSM_HEADS, SSM_HEAD_DIM, D_STATE), 0.1),
        "cache_ffn_conv": nrm(ks[5], (DEPTH, DEC_BATCH, FFN_CONV_WIDTH - 1, D_FF), 1.0),
        "meta_tokens": nrm(ks[6], (N_META, D_MODEL), 1.0),
        "norm_mix": 1.0 + nrm(ks[7], (DEPTH, D_MODEL), 0.02),
        "norm_ffn": 1.0 + nrm(ks[8], (DEPTH, D_MODEL), 0.02),
        "norm_final": 1.0 + nrm(ks[9], (D_MODEL,), 0.02),
        "sc_w_in": nrm(ks[10], (N_CONV_LAYERS, D_MODEL, 3 * D_MODEL), D_MODEL ** -0.5),
        "sc_conv_w": nrm(ks[11], (N_CONV_LAYERS, SC_WIDTH, D_MODEL), SC_WIDTH ** -0.5),
        "sc_w_out": nrm(ks[12], (N_CONV_LAYERS, D_MODEL, D_MODEL), D_MODEL ** -0.5),
        "ssm_w_in": nrm(ks[13], (N_SSM_LAYERS, D_MODEL, SSM_IN_DIM), D_MODEL ** -0.5),
        "ssm_conv_w": nrm(ks[14], (N_SSM_LAYERS, SSM_CONV_WIDTH, CONV_DIM), SSM_CONV_WIDTH ** -0.5),
        "ssm_conv_b": nrm(ks[15], (N_SSM_LAYERS, CONV_DIM), 0.02),
        "ssm_dt_bias": dt0 + jnp.log(-jnp.expm1(-dt0)),
        "ssm_a_log": jnp.log(jax.random.uniform(ks[16], (N_SSM_LAYERS, SSM_HEADS), f32, 1.0, 16.0)),
        "ssm_d": 1.0 + nrm(ks[17], (N_SSM_LAYERS, SSM_HEADS), 0.02),
        "ssm_norm_w": 1.0 + nrm(ks[18], (N_SSM_LAYERS, D_INNER), 0.02),
        "ssm_w_out": nrm(ks[19], (N_SSM_LAYERS, D_INNER, D_MODEL), D_INNER ** -0.5),
        "ffn_w_up": nrm(ks[21], (DEPTH, D_MODEL, D_FF), D_MODEL ** -0.5),
        "ffn_w_gate": nrm(ks[22], (DEPTH, D_MODEL, D_FF), D_MODEL ** -0.5),
        "ffn_conv_w": nrm(ks[23], (DEPTH, FFN_CONV_WIDTH, D_FF), FFN_CONV_WIDTH ** -0.5),
        "ffn_w_down": nrm(ks[24], (DEPTH, D_FF, D_MODEL), D_FF ** -0.5),
    }


def reference(x_prompt, x_sample, cache_sc, cache_ssm_conv, state_ssm, cache_ffn_conv,
              meta_tokens, norm_mix, norm_ffn, norm_final, sc_w_in, sc_conv_w, sc_w_out,
              ssm_w_in, ssm_conv_w, ssm_conv_b, ssm_dt_bias, ssm_a_log, ssm_d, ssm_norm_w, ssm_w_out,
              ffn_w_up, ffn_w_gate, ffn_conv_w, ffn_w_down):
    weights = (norm_mix, norm_ffn, norm_final, sc_w_in, sc_conv_w, sc_w_out,
               ssm_w_in, ssm_conv_w, ssm_conv_b, ssm_dt_bias, ssm_a_log, ssm_d, ssm_norm_w, ssm_w_out,
               ffn_w_up, ffn_w_gate, ffn_conv_w, ffn_w_down)
    dt_ = x_prompt.dtype
    b = x_prompt.shape[0]
    xp = jnp.concatenate([jnp.broadcast_to(meta_tokens.astype(dt_)[None], (b, N_META, D_MODEL)), x_prompt], axis=1)
    n_pad = (-N_META) % SSM_CHUNK
    yp, p_sc, p_ssm_conv, p_ssm, p_ffn = run_trunk(
        xp,
        jnp.zeros((N_CONV_LAYERS, b, SC_WIDTH - 1, D_MODEL), dt_),
        jnp.zeros((N_SSM_LAYERS, b, SSM_CONV_WIDTH - 1, CONV_DIM), dt_),
        jnp.zeros((N_SSM_LAYERS, b, SSM_HEADS, SSM_HEAD_DIM, D_STATE), state_ssm.dtype),
        jnp.zeros((DEPTH, b, FFN_CONV_WIDTH - 1, D_FF), dt_),
        SSM_CHUNK, n_pad, *weights)
    y_prompt = yp[:, N_META:]
    y_sample, s_sc, s_ssm_conv, s_ssm, s_ffn = run_trunk(
        x_sample, cache_sc, cache_ssm_conv, state_ssm, cache_ffn_conv,
        x_sample.shape[1], 0, *weights)
    return (y_prompt, y_sample, p_sc, p_ssm_conv, p_ssm, p_ffn, s_sc, s_ssm_conv, s_ssm, s_ffn)
```

```python
import functools

import jax
import jax.numpy as jnp
from jax import lax
from jax.experimental import pallas as pl
from jax.experimental.pallas import tpu as pltpu

F32 = jnp.float32
BF16 = jnp.bfloat16

EPS = 1e-5
N_META = 16
HEAD_DIM = 64
N_HEADS = 32
N_GROUPS = 4
D_STATE = 128
D_INNER = N_HEADS * HEAD_DIM
GROUP_W = D_INNER // N_GROUPS
BC_W = N_GROUPS * D_STATE
CONV_DIM = D_INNER + 2 * BC_W
CHUNK = 128
SHORT_LEN = 8
SUBLANES = 8
LANES = 128
NEG_BIG = -1e30
VMEM_LIMIT = 56 * 1024 * 1024


def _rmsnorm(x, w):
    r = lax.rsqrt(jnp.mean(x * x, axis=-1, keepdims=True) + EPS)
    return x * r * w


def _silu(x):
    return x / (1.0 + jnp.exp(-x))


def _softplus(x):
    return jnp.maximum(x, 0.0) + jnp.log1p(jnp.exp(-jnp.abs(x)))


def _dot(a, b):
    return jnp.dot(a, b, preferred_element_type=F32)


def _dot_nt(a, b):
    return lax.dot_general(a, b, (((1,), (1,)), ((), ())), preferred_element_type=F32)


def _dot_exact_lhs(m_bf16, x):
    hi = x.astype(BF16)
    r1 = x - hi.astype(F32)
    mid = r1.astype(BF16)
    lo = (r1 - mid.astype(F32)).astype(BF16)
    return _dot(m_bf16, hi) + _dot(m_bf16, mid) + _dot(m_bf16, lo)


def _causal_conv(x, prev, w_ref, width, short):
    tm = x.shape[0]
    y = x * w_ref[width - 1:width, :]
    if short:
        row_in_seq = lax.broadcasted_iota(jnp.int32, x.shape, 0) & (SHORT_LEN - 1)
    else:
        row8 = lax.broadcasted_iota(jnp.int32, (SUBLANES, x.shape[1]), 0)
    for k in range(1, width):
        r = pltpu.roll(x, k, axis=0)
        if short:
            rp = pltpu.roll(prev, tm - (SHORT_LEN - k), axis=0)
            xk = jnp.where(row_in_seq < k, rp, r)
        else:
            top = jnp.where(row8 < k, pltpu.roll(prev, k, axis=0), r[0:SUBLANES])
            xk = top if tm == SUBLANES else jnp.concatenate([top, r[SUBLANES:]], axis=0)
        y = y + xk * w_ref[width - 1 - k:width - k, :]
    return y


def _conv_step(v, pre_ref, carry_ref, tail_ref, w_ref, width, short):
    if short:
        y = _causal_conv(v, pre_ref[...], w_ref, width, True)
        tail_ref[...] = v
        return y

    @pl.when(pl.program_id(1) == 0)
    def _():
        carry_ref[...] = pre_ref[...]

    y = _causal_conv(v, carry_ref[...], w_ref, width, False)
    last = v[v.shape[0] - SUBLANES:]
    carry_ref[...] = last
    tail_ref[...] = last
    return y


def _sc_mixer_kernel(x_ref, pre_ref, nw_ref, win_ref, cw_ref, wout_ref, o_ref, tail_ref, *scratch, short):
    x = x_ref[...]
    d = x.shape[1]
    h = _rmsnorm(x, nw_ref[...]).astype(BF16)
    bcx = _dot(h, win_ref[...])
    p = bcx[:, d:2 * d] * bcx[:, 2 * d:]
    v = _conv_step(p, pre_ref, scratch[0] if scratch else None, tail_ref, cw_ref, 3, short)
    y = _dot((bcx[:, :d] * v).astype(BF16), wout_ref[...])
    o_ref[...] = x + y


def _conv_ffn_kernel(x_ref, pre_ref, nw_ref, wup_ref, wgate_ref, cw_ref, wdown_ref, nf_ref,
                     o_ref, tail_ref, *scratch, short, final_norm):
    x = x_ref[...]
    h = _rmsnorm(x, nw_ref[...]).astype(BF16)
    u = _dot(h, wup_ref[...])
    uc = _conv_step(u, pre_ref, scratch[0] if scratch else None, tail_ref, cw_ref, 3, short)
    g = _dot(h, wgate_ref[...])
    a = (_silu(uc) * g).astype(BF16)
    y = x + _dot(a, wdown_ref[...])
    if final_norm:
        y = _rmsnorm(y, nf_ref[...])
    o_ref[...] = y


def _ssm_in_kernel(x_ref, pre_ref, nw_ref, wzx_ref, wdt_ref, cw_ref, cb_ref, dtb_ref,
                   z_ref, xbc_ref, dt_ref, tail_ref, *scratch, short):
    x = x_ref[...]
    h = _rmsnorm(x, nw_ref[...]).astype(BF16)
    zx = _dot(h, wzx_ref[...])
    z_ref[...] = zx[:, :D_INNER].astype(BF16)
    xbc = zx[:, D_INNER:]
    c = _conv_step(xbc, pre_ref, scratch[0] if scratch else None, tail_ref, cw_ref, 4, short)
    xbc_ref[...] = _silu(c + cb_ref[...]).astype(BF16)
    dt_ref[...] = _softplus(_dot(h, wdt_ref[...]) + dtb_ref[...])


def _head_cols(v, h0, lo):
    return jnp.where(lo, v[:, h0:h0 + 1], v[:, h0 + 1:h0 + 2])


def _ssd_kernel(x_ref, z_ref, xbc_ref, dt_ref, s0_ref, alog_ref, dskip_ref, nw_ref, wout_ref,
                o_ref, sout_ref, *scratch, short):
    q = x_ref.shape[0]
    xbc = xbc_ref[...]
    xs = xbc[:, :D_INNER].astype(F32)
    bm = xbc[:, D_INNER:D_INNER + BC_W]
    cm = xbc[:, D_INNER + BC_W:]
    dt = dt_ref[...]
    a = -jnp.exp(alog_ref[...])
    dta = dt * a

    li = lax.broadcasted_iota(jnp.int32, (q, q), 0)
    si = lax.broadcasted_iota(jnp.int32, (q, q), 1)
    if short:
        same_seq = (li // SHORT_LEN) == (si // SHORT_LEN)
        causal = same_seq & (si <= li)
    else:
        causal = si <= li
    cum = _dot_exact_lhs(causal.astype(BF16), dta)
    cum_t = cum.T
    if short:
        cum_end = _dot_exact_lhs(same_seq.astype(BF16), dta)
    else:
        cum_end = cum[q - 1:q, :]
    w_end = jnp.exp(cum_end - cum) * dt
    ecum = jnp.exp(cum)

    lane = lax.broadcasted_iota(jnp.int32, (q, LANES), 1)
    lo = lane < HEAD_DIM
    lane_k = lax.broadcasted_iota(jnp.int32, (LANES, LANES), 1)
    lo_k = lane_k < HEAD_DIM

    if short:
        n_seq = q // SHORT_LEN
        row_seq = lax.broadcasted_iota(jnp.int32, (q, LANES), 0) // SHORT_LEN
        y_off = []
        for g in range(N_GROUPS):
            cg = cm[:, g * D_STATE:(g + 1) * D_STATE].astype(F32)
            acc = jnp.zeros((q, GROUP_W), F32)
            for s in range(n_seq):
                cmask = jnp.where(row_seq == s, cg, 0.0).astype(BF16)
                sg = s0_ref[s, g * GROUP_W:(g + 1) * GROUP_W, :].astype(BF16)
                acc = acc + _dot_nt(cmask, sg)
            y_off.append(acc)
    else:
        st_ref = scratch[0]

        @pl.when(pl.program_id(1) == 0)
        def _():
            st_ref[...] = s0_ref[...].T

        st = st_ref[...]

    e_end = jnp.exp(cum_end)
    y_pieces = []
    xw_pieces = []
    da_pieces = []
    for g in range(N_GROUPS):
        bg = bm[:, g * D_STATE:(g + 1) * D_STATE]
        cg = cm[:, g * D_STATE:(g + 1) * D_STATE]
        cb = _dot_nt(cg, bg)
        cg32 = cg.astype(F32)
        for pr in range(N_HEADS // N_GROUPS // 2):
            h0 = g * (N_HEADS // N_GROUPS) + 2 * pr
            l0 = h0 * HEAD_DIM
            lhs = []
            for h in (h0, h0 + 1):
                seg = cum[:, h:h + 1] - cum_t[h:h + 1, :]
                decay = jnp.exp(jnp.where(causal, seg, NEG_BIG))
                lhs.append((cb * decay).astype(BF16))
                if not short:
                    lhs.append((cg32 * ecum[:, h:h + 1]).astype(BF16))
            x_pair = xs[:, l0:l0 + LANES]
            xdt = x_pair * _head_cols(dt, h0, lo)
            if short:
                rhs = [jnp.where(lo, xdt, 0.0), jnp.where(lo, 0.0, xdt)]
            else:
                st_pair = st[:, l0:l0 + LANES]
                rhs = [jnp.where(lo, xdt, 0.0), jnp.where(lo_k, st_pair, 0.0),
                       jnp.where(lo, 0.0, xdt), jnp.where(lo_k, 0.0, st_pair)]
            y_pair = _dot(jnp.concatenate(lhs, axis=1), jnp.concatenate(rhs, axis=0).astype(BF16))
            if short:
                y_pair = y_pair + y_off[g][:, 2 * pr * HEAD_DIM:2 * pr * HEAD_DIM + LANES] * _head_cols(ecum, h0, lo)
            y_pieces.append(y_pair)
            xw_pieces.append(x_pair * _head_cols(w_end, h0, lo))
            da_pieces.append(_head_cols(e_end, h0, lo))

    if short:
        pad = jnp.zeros((LANES - q, D_INNER), F32)
        xw_t = jnp.concatenate([jnp.concatenate(xw_pieces, axis=1), pad], axis=0).T.astype(BF16)
        da_t = jnp.concatenate([jnp.concatenate(da_pieces, axis=1), pad], axis=0).T
        row_seq_k = lax.broadcasted_iota(jnp.int32, (LANES, D_STATE), 0) // SHORT_LEN
        pad_b = jnp.zeros((LANES - q, BC_W), F32)
        bm_k = jnp.concatenate([bm.astype(F32), pad_b], axis=0)
        for s in range(n_seq):
            col = s * SHORT_LEN + SHORT_LEN - 1
            for g in range(N_GROUPS):
                rows = slice(g * GROUP_W, (g + 1) * GROUP_W)
                bg = bm_k[:, g * D_STATE:(g + 1) * D_STATE]
                bmask = jnp.where(row_seq_k == s, bg, 0.0).astype(BF16)
                upd = _dot(xw_t[rows, :], bmask)
                sout_ref[s, rows, :] = s0_ref[s, rows, :] * da_t[rows, col:col + 1] + upd
    else:
        for g in range(N_GROUPS):
            lanes_g = slice(g * GROUP_W, (g + 1) * GROUP_W)
            bg_t = bm[:, g * D_STATE:(g + 1) * D_STATE].astype(F32).T.astype(BF16)
            xw_g = jnp.concatenate(xw_pieces[4 * g:4 * g + 4], axis=1).astype(BF16)
            da_g = jnp.concatenate([p[0:1, :] for p in da_pieces[4 * g:4 * g + 4]], axis=1)
            st_ref[:, lanes_g] = st[:, lanes_g] * da_g + _dot(bg_t, xw_g)

        @pl.when(pl.program_id(1) == pl.num_programs(1) - 1)
        def _():
            sout_ref[...] = st_ref[...].T

    y = jnp.concatenate(y_pieces, axis=1) + dskip_ref[...] * xs
    zf = z_ref[...].astype(F32)
    y = y * _silu(zf)
    normed = []
    for g in range(N_GROUPS):
        yg = y[:, g * GROUP_W:(g + 1) * GROUP_W]
        normed.append(yg * lax.rsqrt(jnp.mean(yg * yg, axis=-1, keepdims=True) + EPS))
    yn = (jnp.concatenate(normed, axis=1) * nw_ref[...]).astype(BF16)
    o_ref[...] = x_ref[...] + _dot(yn, wout_ref[...])


def _const_spec(shape):
    return pl.BlockSpec(shape, lambda *_: (0,) * len(shape), pipeline_mode=pl.Buffered(1))


def _token_layout(n_tok, n_seq, tm, short):
    if short:
        grid = (n_tok // tm,)
        tok = lambda i: (i, 0)
        return grid, tok, tok, tok, tm, n_tok, ("arbitrary",)
    tiles = n_tok // n_seq // tm
    grid = (n_seq, tiles)
    tok = lambda b, j: (b * tiles + j, 0)
    per_seq = lambda b, j: (b, 0)
    return grid, tok, per_seq, per_seq, SUBLANES, n_seq * SUBLANES, ("arbitrary", "arbitrary")


def _params(sem):
    return pltpu.CompilerParams(dimension_semantics=sem, vmem_limit_bytes=VMEM_LIMIT)


def _sc_mixer(x, pre, nw, w_in, cw, w_out, *, n_seq, tm, short):
    n_tok, d = x.shape
    grid, tok, pre_map, tail_map, pre_rows, tail_rows, sem = _token_layout(n_tok, n_seq, tm, short)
    return pl.pallas_call(
        functools.partial(_sc_mixer_kernel, short=short),
        grid=grid,
        in_specs=[pl.BlockSpec((tm, d), tok), pl.BlockSpec((pre_rows, d), pre_map),
                  _const_spec(nw.shape), _const_spec(w_in.shape), _const_spec(cw.shape), _const_spec(w_out.shape)],
        out_specs=[pl.BlockSpec((tm, d), tok), pl.BlockSpec((pre_rows, d), tail_map)],
        out_shape=[jax.ShapeDtypeStruct((n_tok, d), F32), jax.ShapeDtypeStruct((tail_rows, d), F32)],
        scratch_shapes=[] if short else [pltpu.VMEM((SUBLANES, d), F32)],
        compiler_params=_params(sem),
        name="sc_mixer_short" if short else "sc_mixer_long",
    )(x, pre, nw, w_in, cw, w_out)


def _conv_ffn(x, pre, nw, w_up, w_gate, cw, w_down, nf, *, n_seq, tm, short, final_norm):
    n_tok, d = x.shape
    f = w_up.shape[1]
    grid, tok, pre_map, tail_map, pre_rows, tail_rows, sem = _token_layout(n_tok, n_seq, tm, short)
    return pl.pallas_call(
        functools.partial(_conv_ffn_kernel, short=short, final_norm=final_norm),
        grid=grid,
        in_specs=[pl.BlockSpec((tm, d), tok), pl.BlockSpec((pre_rows, f), pre_map),
                  _const_spec(nw.shape), _const_spec(w_up.shape), _const_spec(w_gate.shape),
                  _const_spec(cw.shape), _const_spec(w_down.shape), _const_spec(nf.shape)],
        out_specs=[pl.BlockSpec((tm, d), tok), pl.BlockSpec((pre_rows, f), tail_map)],
        out_shape=[jax.ShapeDtypeStruct((n_tok, d), F32), jax.ShapeDtypeStruct((tail_rows, f), F32)],
        scratch_shapes=[] if short else [pltpu.VMEM((SUBLANES, f), F32)],
        compiler_params=_params(sem),
        name="conv_ffn_short" if short else "conv_ffn_long",
    )(x, pre, nw, w_up, w_gate, cw, w_down, nf)


def _ssm_in(x, pre, nw, w_zx, w_dt, cw, cb, dtb, *, n_seq, tm, short):
    n_tok, d = x.shape
    grid, tok, pre_map, tail_map, pre_rows, tail_rows, sem = _token_layout(n_tok, n_seq, tm, short)
    return pl.pallas_call(
        functools.partial(_ssm_in_kernel, short=short),
        grid=grid,
        in_specs=[pl.BlockSpec((tm, d), tok), pl.BlockSpec((pre_rows, CONV_DIM), pre_map),
                  _const_spec(nw.shape), _const_spec(w_zx.shape), _const_spec(w_dt.shape),
                  _const_spec(cw.shape), _const_spec(cb.shape), _const_spec(dtb.shape)],
        out_specs=[pl.BlockSpec((tm, D_INNER), tok), pl.BlockSpec((tm, CONV_DIM), tok),
                   pl.BlockSpec((tm, LANES), tok), pl.BlockSpec((pre_rows, CONV_DIM), tail_map)],
        out_shape=[jax.ShapeDtypeStruct((n_tok, D_INNER), BF16), jax.ShapeDtypeStruct((n_tok, CONV_DIM), BF16),
                   jax.ShapeDtypeStruct((n_tok, LANES), F32), jax.ShapeDtypeStruct((tail_rows, CONV_DIM), F32)],
        scratch_shapes=[] if short else [pltpu.VMEM((SUBLANES, CONV_DIM), F32)],
        compiler_params=_params(sem),
        name="ssm_in_short" if short else "ssm_in_long",
    )(x, pre, nw, w_zx, w_dt, cw, cb, dtb)


def _ssd(x, z, xbc, dt, s0, alog, dskip, nw, w_out, *, n_seq, tm, short):
    n_tok, d = x.shape
    if short:
        s_tile = tm // SHORT_LEN
        grid = (n_tok // tm,)
        tok = lambda i: (i, 0)
        s_spec = pl.BlockSpec((s_tile, D_INNER, D_STATE), lambda i: (i, 0, 0))
        s_shape = jax.ShapeDtypeStruct((n_seq, D_INNER, D_STATE), F32)
        scratch = []
        sem = ("arbitrary",)
    else:
        chunks = n_tok // n_seq // tm
        grid = (n_seq, chunks)
        tok = lambda b, c: (b * chunks + c, 0)
        s_spec = pl.BlockSpec((D_INNER, D_STATE), lambda b, c: (b, 0))
        s_shape = jax.ShapeDtypeStruct((n_seq * D_INNER, D_STATE), F32)
        scratch = [pltpu.VMEM((D_STATE, D_INNER), F32)]
        sem = ("arbitrary", "arbitrary")
    return pl.pallas_call(
        functools.partial(_ssd_kernel, short=short),
        grid=grid,
        in_specs=[pl.BlockSpec((tm, d), tok), pl.BlockSpec((tm, D_INNER), tok), pl.BlockSpec((tm, CONV_DIM), tok),
                  pl.BlockSpec((tm, LANES), tok), s_spec,
                  _const_spec(alog.shape), _const_spec(dskip.shape), _const_spec(nw.shape), _const_spec(w_out.shape)],
        out_specs=[pl.BlockSpec((tm, d), tok), s_spec],
        out_shape=[jax.ShapeDtypeStruct((n_tok, d), F32), s_shape],
        scratch_shapes=scratch,
        compiler_params=_params(sem),
        name="ssd_short" if short else "ssd_long",
    )(x, z, xbc, dt, s0, alog, dskip, nw, w_out)


def _trunk(x, pre_sc, pre_xbc, s0, pre_ffn, w, *, n_seq, short, tms, pad_front=0):
    kw = dict(n_seq=n_seq, short=short)
    x1, t_sc = _sc_mixer(x, pre_sc, w["nm0"], w["sc_in"], w["sc_cw"], w["sc_out"], tm=tms[0], **kw)
    x2, t_f0 = _conv_ffn(x1, pre_ffn[0], w["nf0"], w["up0"], w["gate0"], w["fcw0"], w["down0"], w["nfin"],
                         tm=tms[1], final_norm=False, **kw)
    z, xbc, dt, t_xbc = _ssm_in(x2, pre_xbc, w["nm1"], w["w_zx"], w["w_dt"], w["ssm_cw"], w["ssm_cb"], w["dtb"],
                                tm=tms[2], **kw)
    if pad_front:
        padf = lambda t: jnp.pad(t, ((pad_front, 0), (0, 0)))
        x3, s_new = _ssd(padf(x2), padf(z), padf(xbc), padf(dt), s0, w["alog"], w["dskip"], w["ssm_nw"], w["ssm_out"],
                         tm=CHUNK, **kw)
        x3 = x3[pad_front:]
    else:
        x3, s_new = _ssd(x2, z, xbc, dt, s0, w["alog"], w["dskip"], w["ssm_nw"], w["ssm_out"], tm=tms[3], **kw)
    y, t_f1 = _conv_ffn(x3, pre_ffn[1], w["nf1"], w["up1"], w["gate1"], w["fcw1"], w["down1"], w["nfin"],
                        tm=tms[4], final_norm=True, **kw)
    return y, t_sc, t_xbc, s_new, (t_f0, t_f1)


def _right_align(cache):
    s, wm1, c = cache.shape
    return jnp.pad(cache, ((0, 0), (SUBLANES - wm1, 0), (0, 0))).reshape(s * SUBLANES, c)


def kernel(x_prompt, x_sample, cache_sc, cache_ssm_conv, state_ssm, cache_ffn_conv, meta_tokens, norm_mix, norm_ffn, norm_final, sc_w_in, sc_conv_w, sc_w_out, ssm_w_in, ssm_conv_w, ssm_conv_b, ssm_dt_bias, ssm_a_log, ssm_d, ssm_norm_w, ssm_w_out, ffn_w_up, ffn_w_gate, ffn_conv_w, ffn_w_down):
    b, seq, d = x_prompt.shape
    n_dec, dec_len, _ = x_sample.shape
    d_ff = ffn_w_up.shape[2]
    assert dec_len == SHORT_LEN and seq % CHUNK == 0 and N_META % SUBLANES == 0

    row = lambda v: v.reshape(1, -1).astype(F32)
    pad_heads = lambda v: jnp.pad(v.reshape(1, -1).astype(F32), ((0, 0), (0, LANES - N_HEADS)))
    w_in1 = ssm_w_in[0]
    w = dict(
        nm0=row(norm_mix[0]), nm1=row(norm_mix[1]), nf0=row(norm_ffn[0]), nf1=row(norm_ffn[1]), nfin=row(norm_final),
        sc_in=sc_w_in[0].astype(BF16), sc_cw=sc_conv_w[0], sc_out=sc_w_out[0].astype(BF16),
        w_zx=w_in1[:, :D_INNER + CONV_DIM].astype(BF16),
        w_dt=jnp.pad(w_in1[:, D_INNER + CONV_DIM:], ((0, 0), (0, LANES - N_HEADS))).astype(BF16),
        ssm_cw=ssm_conv_w[0], ssm_cb=row(ssm_conv_b[0]), dtb=pad_heads(ssm_dt_bias[0]), alog=pad_heads(ssm_a_log[0]),
        dskip=row(jnp.repeat(ssm_d[0], HEAD_DIM)), ssm_nw=row(ssm_norm_w[0]), ssm_out=ssm_w_out[0].astype(BF16),
        up0=ffn_w_up[0].astype(BF16), gate0=ffn_w_gate[0].astype(BF16), fcw0=ffn_conv_w[0], down0=ffn_w_down[0].astype(BF16),
        up1=ffn_w_up[1].astype(BF16), gate1=ffn_w_gate[1].astype(BF16), fcw1=ffn_conv_w[1], down1=ffn_w_down[1].astype(BF16),
    )

    zeros8 = lambda c: jnp.zeros((SUBLANES, c), F32)
    _, m_sc, m_xbc, m_state, m_ffn = _trunk(
        meta_tokens.astype(F32), zeros8(d), zeros8(CONV_DIM), jnp.zeros((D_INNER, D_STATE), F32),
        (zeros8(d_ff), zeros8(d_ff)), w, n_seq=1, short=False, tms=(N_META,) * 5, pad_front=CHUNK - N_META)

    rep = lambda t: jnp.tile(t, (b, 1))
    yp, p_sc, p_xbc, p_state, p_ffn = _trunk(
        x_prompt.reshape(b * seq, d), rep(m_sc), rep(m_xbc), rep(m_state), (rep(m_ffn[0]), rep(m_ffn[1])), w,
        n_seq=b, short=False, tms=(512, 512, 256, CHUNK, 512))
    tail = lambda t, k: t.reshape(b, SUBLANES, -1)[:, SUBLANES - k:]
    out_prompt = (
        yp.reshape(b, seq, d),
        tail(p_sc, 2)[None], tail(p_xbc, 3)[None],
        p_state.reshape(1, b, N_HEADS, HEAD_DIM, D_STATE),
        jnp.stack([tail(p_ffn[0], 2), tail(p_ffn[1], 2)]),
    )

    ys, s_sc, s_xbc, s_state, s_ffn = _trunk(
        x_sample.reshape(n_dec * dec_len, d), _right_align(cache_sc[0]), _right_align(cache_ssm_conv[0]),
        state_ssm[0].reshape(n_dec, D_INNER, D_STATE),
        (_right_align(cache_ffn_conv[0]), _right_align(cache_ffn_conv[1])), w,
        n_seq=n_dec, short=True, tms=(256, 256, 256, 64, 256))
    tail_s = lambda t, k: t.reshape(n_dec, SHORT_LEN, -1)[:, SHORT_LEN - k:]
    out_sample = (
        ys.reshape(n_dec, dec_len, d),
        tail_s(s_sc, 2)[None], tail_s(s_xbc, 3)[None],
        s_state.reshape(1, n_dec, N_HEADS, HEAD_DIM, D_STATE),
        jnp.stack([tail_s(s_ffn[0], 2), tail_s(s_ffn[1], 2)]),
    )
    return (out_prompt[0], out_sample[0]) + out_prompt[1:] + out_sample[1:]
```

```python
import functools

import jax
import jax.numpy as jnp
from jax import lax
from jax.experimental import pallas as pl
from jax.experimental.pallas import tpu as pltpu

F32 = jnp.float32
BF16 = jnp.bfloat16

EPS = 1e-5
N_META = 16
HEAD_DIM = 64
N_HEADS = 32
N_GROUPS = 4
HEADS_PER_GROUP = N_HEADS // N_GROUPS
D_STATE = 128
D_INNER = N_HEADS * HEAD_DIM
GROUP_W = D_INNER // N_GROUPS
BC_W = N_GROUPS * D_STATE
CONV_DIM = D_INNER + 2 * BC_W
CHUNK = 128
SHORT_LEN = 8
SUBLANES = 8
LANES = 128
STRIP = 64
COL_BLOCK = 512
NEG_BIG = -1e30
VMEM_LIMIT = 56 * 1024 * 1024


def _rmsnorm(x, w):
    r = lax.rsqrt(jnp.mean(x * x, axis=-1, keepdims=True) + EPS)
    return x * r * w


def _silu(x):
    return x / (1.0 + jnp.exp(-x))


def _softplus(x):
    return jnp.maximum(x, 0.0) + jnp.log1p(jnp.exp(-jnp.abs(x)))


def _dot(a, b):
    return jnp.dot(a, b, preferred_element_type=F32)


def _dot_nt(a, b):
    return lax.dot_general(a, b, (((1,), (1,)), ((), ())), preferred_element_type=F32)


def _dot_exact_lhs(m_bf16, x):
    hi = x.astype(BF16)
    r1 = x - hi.astype(F32)
    mid = r1.astype(BF16)
    lo = (r1 - mid.astype(F32)).astype(BF16)
    return _dot(m_bf16, hi) + _dot(m_bf16, mid) + _dot(m_bf16, lo)


def _causal_conv(x, prev, w_ref, width, short):
    tm = x.shape[0]
    y = x * w_ref[width - 1:width, :]
    if short:
        row_in_seq = lax.broadcasted_iota(jnp.int32, x.shape, 0) & (SHORT_LEN - 1)
    else:
        row8 = lax.broadcasted_iota(jnp.int32, (SUBLANES, x.shape[1]), 0)
    for k in range(1, width):
        r = pltpu.roll(x, k, axis=0)
        if short:
            rp = pltpu.roll(prev, tm - (SHORT_LEN - k), axis=0)
            xk = jnp.where(row_in_seq < k, rp, r)
        else:
            top = jnp.where(row8 < k, pltpu.roll(prev, k, axis=0), r[0:SUBLANES])
            xk = top if tm == SUBLANES else jnp.concatenate([top, r[SUBLANES:]], axis=0)
        y = y + xk * w_ref[width - 1 - k:width - k, :]
    return y


def _conv_step(v, pre_ref, carry_ref, tail_ref, w_ref, width, short):
    if short:
        y = _causal_conv(v, pre_ref[...], w_ref, width, True)
        tail_ref[...] = v
        return y

    @pl.when(pl.program_id(1) == 0)
    def _():
        carry_ref[...] = pre_ref[...]

    y = _causal_conv(v, carry_ref[...], w_ref, width, False)
    last = v[v.shape[0] - SUBLANES:]
    carry_ref[...] = last
    tail_ref[...] = last
    return y


def _conv_strips(buf_ref, pre_ref, w_ref, width, tm, n_cols, short, emit):
    rows = min(STRIP, tm)
    row_in_seq = lax.broadcasted_iota(jnp.int32, (rows, LANES), 0) & (SHORT_LEN - 1)
    for c0 in range(0, n_cols, LANES):
        cols = slice(c0, c0 + LANES)
        taps = [w_ref[k:k + 1, cols] for k in range(width)]
        for r0 in range(0, tm, rows):
            if short:
                xv = buf_ref[SUBLANES + r0:SUBLANES + r0 + rows, cols]
                pv = pre_ref[r0:r0 + rows, cols]
                y = xv * taps[width - 1]
                for k in range(1, width):
                    xk = jnp.where(row_in_seq < k, pltpu.roll(pv, rows - (SHORT_LEN - k), axis=0),
                                   pltpu.roll(xv, k, axis=0))
                    y = y + xk * taps[width - 1 - k]
            else:
                ext = buf_ref[r0:r0 + rows + SUBLANES, cols]
                y = ext[SUBLANES:] * taps[width - 1]
                for k in range(1, width):
                    y = y + pltpu.roll(ext, k, axis=0)[SUBLANES:] * taps[width - 1 - k]
            emit(r0, cols, y)


def _sc_mixer_kernel(x_ref, pre_ref, nw_ref, win_ref, cw_ref, wout_ref, o_ref, tail_ref, *scratch, short):
    x = x_ref[...]
    d = x.shape[1]
    h = _rmsnorm(x, nw_ref[...]).astype(BF16)
    bcx = _dot(h, win_ref[...])
    p = bcx[:, d:2 * d] * bcx[:, 2 * d:]
    v = _conv_step(p, pre_ref, scratch[0] if scratch else None, tail_ref, cw_ref, 3, short)
    y = _dot((bcx[:, :d] * v).astype(BF16), wout_ref[...])
    o_ref[...] = x + y


def _conv_ffn_kernel(x_ref, pre_ref, nw_ref, wup_ref, wgate_ref, cw_ref, wdown_ref, nf_ref,
                     o_ref, tail_ref, *scratch, short, final_norm):
    x = x_ref[...]
    h = _rmsnorm(x, nw_ref[...]).astype(BF16)
    u = _dot(h, wup_ref[...])
    uc = _conv_step(u, pre_ref, scratch[0] if scratch else None, tail_ref, cw_ref, 3, short)
    g = _dot(h, wgate_ref[...])
    a = (_silu(uc) * g).astype(BF16)
    y = x + _dot(a, wdown_ref[...])
    if final_norm:
        y = _rmsnorm(y, nf_ref[...])
    o_ref[...] = y


def _head_cols(v, h0, lo):
    return jnp.where(lo, v[:, h0:h0 + 1], v[:, h0 + 1:h0 + 2])


def _ssm_in_kernel(x_ref, pre_ref, nw_ref, wz_ref, wxbc_ref, wdt_ref, cw_ref, cb_ref, dtb_ref, alog_ref,
                   z_ref, xbc_ref, *rest, short):
    if short:
        dt_ref, tail_ref, buf_ref = rest
    else:
        cum_ref, cspt_ref, we_ref, da_ref, tail_ref, buf_ref = rest
    tm = x_ref.shape[0]
    h = _rmsnorm(x_ref[...], nw_ref[...]).astype(BF16)

    if not short:
        @pl.when(pl.program_id(1) == 0)
        def _():
            buf_ref[0:SUBLANES, :] = pre_ref[...]

    for c0 in range(0, CONV_DIM, COL_BLOCK):
        buf_ref[SUBLANES:, c0:c0 + COL_BLOCK] = _dot(h, wxbc_ref[:, c0:c0 + COL_BLOCK])
    for c0 in range(0, D_INNER, COL_BLOCK):
        z_ref[:, c0:c0 + COL_BLOCK] = _dot(h, wz_ref[:, c0:c0 + COL_BLOCK]).astype(BF16)
    dt = _softplus(_dot(h, wdt_ref[...]) + dtb_ref[...])

    if short:
        tail_ref[...] = buf_ref[SUBLANES:, :]
    else:
        tail_ref[...] = buf_ref[tm:, :]

    def emit(r0, cols, y):
        xbc_ref[r0:r0 + y.shape[0], cols] = _silu(y + cb_ref[:, cols]).astype(BF16)

    _conv_strips(buf_ref, pre_ref, cw_ref, 4, tm, CONV_DIM, short, emit)

    if short:
        dt_ref[...] = dt
        return

    buf_ref[0:SUBLANES, :] = buf_ref[tm:, :]

    a = -jnp.exp(alog_ref[...])
    li = lax.broadcasted_iota(jnp.int32, (CHUNK, CHUNK), 0)
    si = lax.broadcasted_iota(jnp.int32, (CHUNK, CHUNK), 1)
    tri = (si <= li).astype(BF16)
    lo8 = lax.broadcasted_iota(jnp.int32, (SUBLANES, LANES), 1) < HEAD_DIM
    pad = CHUNK - tm if tm < CHUNK else 0
    for c in range(max(tm // CHUNK, 1)):
        if pad:
            dt_c = jnp.concatenate([jnp.zeros((pad, LANES), F32), dt], axis=0)
        else:
            dt_c = dt[c * CHUNK:(c + 1) * CHUNK]
        rows = slice(c * CHUNK, (c + 1) * CHUNK)
        cum = _dot_exact_lhs(tri, dt_c * a)
        end = cum[CHUNK - 1:CHUNK, :]
        cum_ref[rows, :] = cum
        cspt_ref[:, rows] = (cum - jnp.log(dt_c)).T
        we_ref[rows, 0:LANES] = (jnp.exp(end - cum) * dt_c).astype(BF16)
        we_ref[rows, LANES:2 * LANES] = jnp.exp(cum).astype(BF16)
        e_end = jnp.broadcast_to(jnp.exp(end), (SUBLANES, LANES))
        for pr in range(N_HEADS // 2):
            da_ref[c * SUBLANES:(c + 1) * SUBLANES, pr * LANES:(pr + 1) * LANES] = _head_cols(e_end, 2 * pr, lo8)


def _ssd_long_kernel(x_ref, z_ref, xbc_ref, cum_ref, cspt_ref, we_ref, da_ref, s0_ref, e_ref,
                     dskip_ref, nw_ref, wout_ref, o_ref, sout_ref, st_ref):
    q = CHUNK

    @pl.when(pl.program_id(1) == 0)
    def _():
        st_ref[...] = s0_ref[...].T

    cum = cum_ref[...]
    cspt = cspt_ref[...]
    li = lax.broadcasted_iota(jnp.int32, (q, q), 0)
    si = lax.broadcasted_iota(jnp.int32, (q, q), 1)
    causal = si <= li
    lane = lax.broadcasted_iota(jnp.int32, (q, LANES), 1)
    m_lo = (lane < HEAD_DIM).astype(BF16)
    m_hi = (lane >= HEAD_DIM).astype(BF16)
    w_end = we_ref[:, 0:LANES]
    e_cum = we_ref[:, LANES:2 * LANES]

    normed = []
    for g in range(N_GROUPS):
        gl = slice(g * GROUP_W, (g + 1) * GROUP_W)
        bg = xbc_ref[:, D_INNER + g * D_STATE:D_INNER + (g + 1) * D_STATE]
        cg = xbc_ref[:, D_INNER + BC_W + g * D_STATE:D_INNER + BC_W + (g + 1) * D_STATE]
        xs_b = xbc_ref[:, gl]
        xs = xs_b.astype(F32)
        st_g = st_ref[:, gl]
        cb = _dot_nt(cg, bg)
        y_off = _dot(cg, st_g.astype(BF16))
        w_exp = _dot(w_end, e_ref[:, gl])
        e_exp = _dot(e_cum, e_ref[:, gl])
        xw = (xs * w_exp).astype(BF16)
        bg_t = bg.astype(F32).T.astype(BF16)
        st_ref[:, gl] = st_g * da_ref[0:1, gl] + _dot(bg_t, xw)

        ys = []
        for pr in range(HEADS_PER_GROUP // 2):
            h0 = g * HEADS_PER_GROUP + 2 * pr
            ms = []
            for h in (h0, h0 + 1):
                seg = cum[:, h:h + 1] - cspt[h:h + 1, :]
                ms.append((cb * jnp.exp(jnp.where(causal, seg, NEG_BIG))).astype(BF16))
            xp = xs_b[:, pr * LANES:(pr + 1) * LANES]
            rhs = jnp.concatenate([xp * m_lo, xp * m_hi], axis=0)
            ys.append(_dot(jnp.concatenate(ms, axis=1), rhs))
        y = jnp.concatenate(ys, axis=1) + y_off * e_exp + dskip_ref[:, gl] * xs
        y = y * _silu(z_ref[:, gl].astype(F32))
        y = y * lax.rsqrt(jnp.mean(y * y, axis=-1, keepdims=True) + EPS)
        normed.append((y * nw_ref[:, gl]).astype(BF16))

    o_ref[...] = x_ref[...] + _dot(jnp.concatenate(normed, axis=1), wout_ref[...])

    @pl.when(pl.program_id(1) == pl.num_programs(1) - 1)
    def _():
        sout_ref[...] = st_ref[...].T


def _ssd_short_kernel(x_ref, z_ref, xbc_ref, dt_ref, s0_ref, alog_ref, dskip_ref, nw_ref, wout_ref,
                      o_ref, sout_ref):
    q = x_ref.shape[0]
    n_seq = q // SHORT_LEN
    xbc = xbc_ref[...]
    xs = xbc[:, :D_INNER].astype(F32)
    bm = xbc[:, D_INNER:D_INNER + BC_W]
    cm = xbc[:, D_INNER + BC_W:]
    dt = dt_ref[...]
    a = -jnp.exp(alog_ref[...])
    dta = dt * a

    li = lax.broadcasted_iota(jnp.int32, (q, q), 0)
    si = lax.broadcasted_iota(jnp.int32, (q, q), 1)
    same_seq = (li // SHORT_LEN) == (si // SHORT_LEN)
    causal = same_seq & (si <= li)
    cum = _dot_exact_lhs(causal.astype(BF16), dta)
    cum_t = cum.T
    cum_end = _dot_exact_lhs(same_seq.astype(BF16), dta)
    w_end = jnp.exp(cum_end - cum) * dt
    ecum = jnp.exp(cum)
    e_end = jnp.exp(cum_end)

    lane = lax.broadcasted_iota(jnp.int32, (q, LANES), 1)
    lo = lane < HEAD_DIM
    row_seq = lax.broadcasted_iota(jnp.int32, (q, LANES), 0) // SHORT_LEN

    y_off = []
    for g in range(N_GROUPS):
        cg = cm[:, g * D_STATE:(g + 1) * D_STATE].astype(F32)
        acc = jnp.zeros((q, GROUP_W), F32)
        for s in range(n_seq):
            cmask = jnp.where(row_seq == s, cg, 0.0).astype(BF16)
            sg = s0_ref[s, g * GROUP_W:(g + 1) * GROUP_W, :].astype(BF16)
            acc = acc + _dot_nt(cmask, sg)
        y_off.append(acc)

    y_pieces = []
    xw_pieces = []
    da_pieces = []
    for g in range(N_GROUPS):
        bg = bm[:, g * D_STATE:(g + 1) * D_STATE]
        cg = cm[:, g * D_STATE:(g + 1) * D_STATE]
        cb = _dot_nt(cg, bg)
        for pr in range(HEADS_PER_GROUP // 2):
            h0 = g * HEADS_PER_GROUP + 2 * pr
            l0 = h0 * HEAD_DIM
            lhs = []
            for h in (h0, h0 + 1):
                seg = cum[:, h:h + 1] - cum_t[h:h + 1, :]
                lhs.append((cb * jnp.exp(jnp.where(causal, seg, NEG_BIG))).astype(BF16))
            x_pair = xs[:, l0:l0 + LANES]
            xdt = x_pair * _head_cols(dt, h0, lo)
            rhs = jnp.concatenate([jnp.where(lo, xdt, 0.0), jnp.where(lo, 0.0, xdt)], axis=0).astype(BF16)
            y_pair = _dot(jnp.concatenate(lhs, axis=1), rhs)
            y_pair = y_pair + y_off[g][:, 2 * pr * HEAD_DIM:2 * pr * HEAD_DIM + LANES] * _head_cols(ecum, h0, lo)
            y_pieces.append(y_pair)
            xw_pieces.append(x_pair * _head_cols(w_end, h0, lo))
            da_pieces.append(_head_cols(e_end, h0, lo))

    pad = jnp.zeros((LANES - q, D_INNER), F32)
    xw_t = jnp.concatenate([jnp.concatenate(xw_pieces, axis=1), pad], axis=0).T.astype(BF16)
    da_t = jnp.concatenate([jnp.concatenate(da_pieces, axis=1), pad], axis=0).T
    row_seq_k = lax.broadcasted_iota(jnp.int32, (LANES, D_STATE), 0) // SHORT_LEN
    bm_k = jnp.concatenate([bm.astype(F32), jnp.zeros((LANES - q, BC_W), F32)], axis=0)
    for s in range(n_seq):
        col = s * SHORT_LEN + SHORT_LEN - 1
        for g in range(N_GROUPS):
            rows = slice(g * GROUP_W, (g + 1) * GROUP_W)
            bmask = jnp.where(row_seq_k == s, bm_k[:, g * D_STATE:(g + 1) * D_STATE], 0.0).astype(BF16)
            upd = _dot(xw_t[rows, :], bmask)
            sout_ref[s, rows, :] = s0_ref[s, rows, :] * da_t[rows, col:col + 1] + upd

    y = jnp.concatenate(y_pieces, axis=1) + dskip_ref[...] * xs
    y = y * _silu(z_ref[...].astype(F32))
    normed = []
    for g in range(N_GROUPS):
        yg = y[:, g * GROUP_W:(g + 1) * GROUP_W]
        normed.append(yg * lax.rsqrt(jnp.mean(yg * yg, axis=-1, keepdims=True) + EPS))
    yn = (jnp.concatenate(normed, axis=1) * nw_ref[...]).astype(BF16)
    o_ref[...] = x_ref[...] + _dot(yn, wout_ref[...])


def _const_spec(shape):
    return pl.BlockSpec(shape, lambda *_: (0,) * len(shape), pipeline_mode=pl.Buffered(1))


def _token_layout(n_tok, n_seq, tm, short):
    if short:
        grid = (n_tok // tm,)
        tok = lambda i: (i, 0)
        return grid, tok, tok, tok, tm, n_tok, ("arbitrary",)
    tiles = n_tok // n_seq // tm
    grid = (n_seq, tiles)
    tok = lambda b, j: (b * tiles + j, 0)
    per_seq = lambda b, j: (b, 0)
    return grid, tok, per_seq, per_seq, SUBLANES, n_seq * SUBLANES, ("arbitrary", "arbitrary")


def _params(sem):
    return pltpu.CompilerParams(dimension_semantics=sem, vmem_limit_bytes=VMEM_LIMIT)


def _sc_mixer(x, pre, nw, w_in, cw, w_out, *, n_seq, tm, short):
    n_tok, d = x.shape
    grid, tok, pre_map, tail_map, pre_rows, tail_rows, sem = _token_layout(n_tok, n_seq, tm, short)
    return pl.pallas_call(
        functools.partial(_sc_mixer_kernel, short=short),
        grid=grid,
        in_specs=[pl.BlockSpec((tm, d), tok), pl.BlockSpec((pre_rows, d), pre_map),
                  _const_spec(nw.shape), _const_spec(w_in.shape), _const_spec(cw.shape), _const_spec(w_out.shape)],
        out_specs=[pl.BlockSpec((tm, d), tok), pl.BlockSpec((pre_rows, d), tail_map)],
        out_shape=[jax.ShapeDtypeStruct((n_tok, d), F32), jax.ShapeDtypeStruct((tail_rows, d), F32)],
        scratch_shapes=[] if short else [pltpu.VMEM((SUBLANES, d), F32)],
        compiler_params=_params(sem),
        name="sc_mixer_short" if short else "sc_mixer_long",
    )(x, pre, nw, w_in, cw, w_out)


def _conv_ffn(x, pre, nw, w_up, w_gate, cw, w_down, nf, *, n_seq, tm, short, final_norm):
    n_tok, d = x.shape
    f = w_up.shape[1]
    grid, tok, pre_map, tail_map, pre_rows, tail_rows, sem = _token_layout(n_tok, n_seq, tm, short)
    return pl.pallas_call(
        functools.partial(_conv_ffn_kernel, short=short, final_norm=final_norm),
        grid=grid,
        in_specs=[pl.BlockSpec((tm, d), tok), pl.BlockSpec((pre_rows, f), pre_map),
                  _const_spec(nw.shape), _const_spec(w_up.shape), _const_spec(w_gate.shape),
                  _const_spec(cw.shape), _const_spec(w_down.shape), _const_spec(nf.shape)],
        out_specs=[pl.BlockSpec((tm, d), tok), pl.BlockSpec((pre_rows, f), tail_map)],
        out_shape=[jax.ShapeDtypeStruct((n_tok, d), F32), jax.ShapeDtypeStruct((tail_rows, f), F32)],
        scratch_shapes=[] if short else [pltpu.VMEM((SUBLANES, f), F32)],
        compiler_params=_params(sem),
        name="conv_ffn_short" if short else "conv_ffn_long",
    )(x, pre, nw, w_up, w_gate, cw, w_down, nf)


def _ssm_in(x, pre, nw, w_z, w_xbc, w_dt, cw, cb, dtb, alog, *, n_seq, tm, short):
    n_tok, d = x.shape
    grid, tok, pre_map, tail_map, pre_rows, tail_rows, sem = _token_layout(n_tok, n_seq, tm, short)
    out_specs = [pl.BlockSpec((tm, D_INNER), tok), pl.BlockSpec((tm, CONV_DIM), tok)]
    out_shape = [jax.ShapeDtypeStruct((n_tok, D_INNER), BF16), jax.ShapeDtypeStruct((n_tok, CONV_DIM), BF16)]
    if short:
        out_specs.append(pl.BlockSpec((tm, LANES), tok))
        out_shape.append(jax.ShapeDtypeStruct((n_tok, LANES), F32))
    else:
        ctm = max(tm, CHUNK)
        n_rows = n_tok // tm * ctm
        tiles = n_tok // n_seq // tm
        tok_t = lambda b, j: (0, b * tiles + j)
        da_rows = ctm // CHUNK * SUBLANES
        out_specs += [pl.BlockSpec((ctm, LANES), tok), pl.BlockSpec((LANES, ctm), tok_t),
                      pl.BlockSpec((ctm, 2 * LANES), tok), pl.BlockSpec((da_rows, D_INNER), tok)]
        out_shape += [jax.ShapeDtypeStruct((n_rows, LANES), F32), jax.ShapeDtypeStruct((LANES, n_rows), F32),
                      jax.ShapeDtypeStruct((n_rows, 2 * LANES), BF16),
                      jax.ShapeDtypeStruct((n_rows // CHUNK * SUBLANES, D_INNER), F32)]
    out_specs.append(pl.BlockSpec((pre_rows, CONV_DIM), tail_map))
    out_shape.append(jax.ShapeDtypeStruct((tail_rows, CONV_DIM), F32))
    return pl.pallas_call(
        functools.partial(_ssm_in_kernel, short=short),
        grid=grid,
        in_specs=[pl.BlockSpec((tm, d), tok), pl.BlockSpec((pre_rows, CONV_DIM), pre_map),
                  _const_spec(nw.shape), _const_spec(w_z.shape), _const_spec(w_xbc.shape), _const_spec(w_dt.shape),
                  _const_spec(cw.shape), _const_spec(cb.shape), _const_spec(dtb.shape), _const_spec(alog.shape)],
        out_specs=out_specs,
        out_shape=out_shape,
        scratch_shapes=[pltpu.VMEM((tm + SUBLANES, CONV_DIM), F32)],
        compiler_params=_params(sem),
        name="ssm_in_short" if short else "ssm_in_long",
    )(x, pre, nw, w_z, w_xbc, w_dt, cw, cb, dtb, alog)


def _ssd_long(x, z, xbc, cum, cspt, we, da, s0, expand, dskip, nw, w_out, *, n_seq):
    n_tok, d = x.shape
    chunks = n_tok // n_seq // CHUNK
    tok = lambda b, c: (b * chunks + c, 0)
    tok_t = lambda b, c: (0, b * chunks + c)
    s_spec = pl.BlockSpec((D_INNER, D_STATE), lambda b, c: (b, 0))
    return pl.pallas_call(
        _ssd_long_kernel,
        grid=(n_seq, chunks),
        in_specs=[pl.BlockSpec((CHUNK, d), tok), pl.BlockSpec((CHUNK, D_INNER), tok),
                  pl.BlockSpec((CHUNK, CONV_DIM), tok), pl.BlockSpec((CHUNK, LANES), tok),
                  pl.BlockSpec((LANES, CHUNK), tok_t), pl.BlockSpec((CHUNK, 2 * LANES), tok),
                  pl.BlockSpec((SUBLANES, D_INNER), tok), s_spec,
                  _const_spec(expand.shape), _const_spec(dskip.shape), _const_spec(nw.shape), _const_spec(w_out.shape)],
        out_specs=[pl.BlockSpec((CHUNK, d), tok), s_spec],
        out_shape=[jax.ShapeDtypeStruct((n_tok, d), F32), jax.ShapeDtypeStruct((n_seq * D_INNER, D_STATE), F32)],
        scratch_shapes=[pltpu.VMEM((D_STATE, D_INNER), F32)],
        compiler_params=_params(("arbitrary", "arbitrary")),
        name="ssd_long",
    )(x, z, xbc, cum, cspt, we, da, s0, expand, dskip, nw, w_out)


def _ssd_short(x, z, xbc, dt, s0, alog, dskip, nw, w_out, *, tm):
    n_tok, d = x.shape
    n_seq = s0.shape[0]
    tok = lambda i: (i, 0)
    s_spec = pl.BlockSpec((tm // SHORT_LEN, D_INNER, D_STATE), lambda i: (i, 0, 0))
    return pl.pallas_call(
        _ssd_short_kernel,
        grid=(n_tok // tm,),
        in_specs=[pl.BlockSpec((tm, d), tok), pl.BlockSpec((tm, D_INNER), tok), pl.BlockSpec((tm, CONV_DIM), tok),
                  pl.BlockSpec((tm, LANES), tok), s_spec,
                  _const_spec(alog.shape), _const_spec(dskip.shape), _const_spec(nw.shape), _const_spec(w_out.shape)],
        out_specs=[pl.BlockSpec((tm, d), tok), s_spec],
        out_shape=[jax.ShapeDtypeStruct((n_tok, d), F32), jax.ShapeDtypeStruct((n_seq, D_INNER, D_STATE), F32)],
        compiler_params=_params(("arbitrary",)),
        name="ssd_short",
    )(x, z, xbc, dt, s0, alog, dskip, nw, w_out)


def _trunk(x, pre_sc, pre_xbc, s0, pre_ffn, w, *, n_seq, short, tms):
    kw = dict(n_seq=n_seq, short=short)
    x1, t_sc = _sc_mixer(x, pre_sc, w["nm0"], w["sc_in"], w["sc_cw"], w["sc_out"], tm=tms[0], **kw)
    x2, t_f0 = _conv_ffn(x1, pre_ffn[0], w["nf0"], w["up0"], w["gate0"], w["fcw0"], w["down0"], w["nfin"],
                         tm=tms[1], final_norm=False, **kw)
    ssm = _ssm_in(x2, pre_xbc, w["nm1"], w["w_z"], w["w_xbc"], w["w_dt"], w["ssm_cw"], w["ssm_cb"], w["dtb"],
                  w["alog"], tm=tms[2], **kw)
    if short:
        z, xbc, dt, t_xbc = ssm
        x3, s_new = _ssd_short(x2, z, xbc, dt, s0, w["alog"], w["dskip"], w["ssm_nw"], w["ssm_out"], tm=tms[3])
    else:
        z, xbc, cum, cspt, we, da, t_xbc = ssm
        pad_front = cum.shape[0] - x2.shape[0]
        padf = lambda t: jnp.pad(t, ((pad_front, 0), (0, 0))) if pad_front else t
        x3, s_new = _ssd_long(padf(x2), padf(z), padf(xbc), cum, cspt, we, da, s0, w["expand"], w["dskip"],
                              w["ssm_nw"], w["ssm_out"], n_seq=n_seq)
        x3 = x3[pad_front:]
    y, t_f1 = _conv_ffn(x3, pre_ffn[1], w["nf1"], w["up1"], w["gate1"], w["fcw1"], w["down1"], w["nfin"],
                        tm=tms[4], final_norm=True, **kw)
    return y, t_sc, t_xbc, s_new, (t_f0, t_f1)


def _right_align(cache):
    s, wm1, c = cache.shape
    return jnp.pad(cache, ((0, 0), (SUBLANES - wm1, 0), (0, 0))).reshape(s * SUBLANES, c)


def kernel(x_prompt, x_sample, cache_sc, cache_ssm_conv, state_ssm, cache_ffn_conv, meta_tokens, norm_mix, norm_ffn, norm_final, sc_w_in, sc_conv_w, sc_w_out, ssm_w_in, ssm_conv_w, ssm_conv_b, ssm_dt_bias, ssm_a_log, ssm_d, ssm_norm_w, ssm_w_out, ffn_w_up, ffn_w_gate, ffn_conv_w, ffn_w_down):
    b, seq, d = x_prompt.shape
    n_dec, dec_len, _ = x_sample.shape
    d_ff = ffn_w_up.shape[2]
    assert dec_len == SHORT_LEN and seq % CHUNK == 0 and N_META % SUBLANES == 0 and N_META <= CHUNK

    row = lambda v: v.reshape(1, -1).astype(F32)
    pad_heads = lambda v: jnp.pad(v.reshape(1, -1).astype(F32), ((0, 0), (0, LANES - N_HEADS)))
    w_in1 = ssm_w_in[0]
    head_of_lane = jnp.arange(D_INNER, dtype=jnp.int32)[None, :] // HEAD_DIM
    w = dict(
        nm0=row(norm_mix[0]), nm1=row(norm_mix[1]), nf0=row(norm_ffn[0]), nf1=row(norm_ffn[1]), nfin=row(norm_final),
        sc_in=sc_w_in[0].astype(BF16), sc_cw=sc_conv_w[0], sc_out=sc_w_out[0].astype(BF16),
        w_z=w_in1[:, :D_INNER].astype(BF16), w_xbc=w_in1[:, D_INNER:D_INNER + CONV_DIM].astype(BF16),
        w_dt=jnp.pad(w_in1[:, D_INNER + CONV_DIM:], ((0, 0), (0, LANES - N_HEADS))).astype(BF16),
        ssm_cw=ssm_conv_w[0], ssm_cb=row(ssm_conv_b[0]), dtb=pad_heads(ssm_dt_bias[0]), alog=pad_heads(ssm_a_log[0]),
        dskip=row(jnp.repeat(ssm_d[0], HEAD_DIM)), ssm_nw=row(ssm_norm_w[0]), ssm_out=ssm_w_out[0].astype(BF16),
        expand=(jnp.arange(LANES, dtype=jnp.int32)[:, None] == head_of_lane).astype(BF16),
        up0=ffn_w_up[0].astype(BF16), gate0=ffn_w_gate[0].astype(BF16), fcw0=ffn_conv_w[0], down0=ffn_w_down[0].astype(BF16),
        up1=ffn_w_up[1].astype(BF16), gate1=ffn_w_gate[1].astype(BF16), fcw1=ffn_conv_w[1], down1=ffn_w_down[1].astype(BF16),
    )

    zeros8 = lambda c: jnp.zeros((SUBLANES, c), F32)
    _, m_sc, m_xbc, m_state, m_ffn = _trunk(
        meta_tokens.astype(F32), zeros8(d), zeros8(CONV_DIM), jnp.zeros((D_INNER, D_STATE), F32),
        (zeros8(d_ff), zeros8(d_ff)), w, n_seq=1, short=False, tms=(N_META,) * 5)

    rep = lambda t: jnp.tile(t, (b, 1))
    yp, p_sc, p_xbc, p_state, p_ffn = _trunk(
        x_prompt.reshape(b * seq, d), rep(m_sc), rep(m_xbc), rep(m_state), (rep(m_ffn[0]), rep(m_ffn[1])), w,
        n_seq=b, short=False, tms=(512, 512, 512, CHUNK, 512))
    tail = lambda t, k: t.reshape(b, SUBLANES, -1)[:, SUBLANES - k:]
    out_prompt = (
        yp.reshape(b, seq, d),
        tail(p_sc, 2)[None], tail(p_xbc, 3)[None],
        p_state.reshape(1, b, N_HEADS, HEAD_DIM, D_STATE),
        jnp.stack([tail(p_ffn[0], 2), tail(p_ffn[1], 2)]),
    )

    ys, s_sc, s_xbc, s_state, s_ffn = _trunk(
        x_sample.reshape(n_dec * dec_len, d), _right_align(cache_sc[0]), _right_align(cache_ssm_conv[0]),
        state_ssm[0].reshape(n_dec, D_INNER, D_STATE),
        (_right_align(cache_ffn_conv[0]), _right_align(cache_ffn_conv[1])), w,
        n_seq=n_dec, short=True, tms=(256, 256, 256, 64, 256))
    tail_s = lambda t, k: t.reshape(n_dec, SHORT_LEN, -1)[:, SHORT_LEN - k:]
    out_sample = (
        ys.reshape(n_dec, dec_len, d),
        tail_s(s_sc, 2)[None], tail_s(s_xbc, 3)[None],
        s_state.reshape(1, n_dec, N_HEADS, HEAD_DIM, D_STATE),
        jnp.stack([tail_s(s_ffn[0], 2), tail_s(s_ffn[1], 2)]),
    )
    return (out_prompt[0], out_sample[0]) + out_prompt[1:] + out_sample[1:]
```

```python
import functools

import jax
import jax.numpy as jnp
from jax import lax
from jax.experimental import pallas as pl
from jax.experimental.pallas import tpu as pltpu

F32 = jnp.float32
BF16 = jnp.bfloat16

EPS = 1e-5
N_META = 16
HEAD_DIM = 64
N_HEADS = 32
N_GROUPS = 4
HEADS_PER_GROUP = N_HEADS // N_GROUPS
D_STATE = 128
D_INNER = N_HEADS * HEAD_DIM
GROUP_W = D_INNER // N_GROUPS
BC_W = N_GROUPS * D_STATE
CONV_DIM = D_INNER + 2 * BC_W
CHUNK = 128
SHORT_LEN = 8
SUBLANES = 8
LANES = 128
STRIP = 64
COL_BLOCK = 512
NEG_BIG = -1e30
NEG_LOG2E = -1.4426950408889634
VMEM_LIMIT = 56 * 1024 * 1024


def _rmsnorm(x, w):
    r = lax.rsqrt(jnp.mean(x * x, axis=-1, keepdims=True) + EPS)
    return x * r * w


def _silu(x):
    return x / (1.0 + jnp.exp2(x * NEG_LOG2E))


def _softplus(x):
    return jnp.maximum(x, 0.0) + jnp.log1p(jnp.exp(-jnp.abs(x)))


def _dot(a, b):
    return jnp.dot(a, b, preferred_element_type=F32)


def _dot_nt(a, b):
    return lax.dot_general(a, b, (((1,), (1,)), ((), ())), preferred_element_type=F32)


def _dot_exact_lhs(m_bf16, x):
    hi = x.astype(BF16)
    r1 = x - hi.astype(F32)
    mid = r1.astype(BF16)
    lo = (r1 - mid.astype(F32)).astype(BF16)
    return _dot(m_bf16, hi) + _dot(m_bf16, mid) + _dot(m_bf16, lo)


def _causal_conv(x, prev, w_ref, width, short):
    tm = x.shape[0]
    y = x * w_ref[width - 1:width, :]
    if short:
        row_in_seq = lax.broadcasted_iota(jnp.int32, x.shape, 0) & (SHORT_LEN - 1)
    else:
        row8 = lax.broadcasted_iota(jnp.int32, (SUBLANES, x.shape[1]), 0)
    for k in range(1, width):
        r = pltpu.roll(x, k, axis=0)
        if short:
            rp = pltpu.roll(prev, tm - (SHORT_LEN - k), axis=0)
            xk = jnp.where(row_in_seq < k, rp, r)
        else:
            top = jnp.where(row8 < k, pltpu.roll(prev, k, axis=0), r[0:SUBLANES])
            xk = top if tm == SUBLANES else jnp.concatenate([top, r[SUBLANES:]], axis=0)
        y = y + xk * w_ref[width - 1 - k:width - k, :]
    return y


def _conv_step(v, pre_ref, carry_ref, tail_ref, w_ref, width, short):
    if short:
        y = _causal_conv(v, pre_ref[...], w_ref, width, True)
        for s in range(v.shape[0] // SHORT_LEN):
            tail_ref[s] = v[(s + 1) * SHORT_LEN - (width - 1):(s + 1) * SHORT_LEN]
        return y

    @pl.when(pl.program_id(1) == 0)
    def _():
        carry_ref[...] = pre_ref[...]

    y = _causal_conv(v, carry_ref[...], w_ref, width, False)
    last = v[v.shape[0] - SUBLANES:]
    carry_ref[...] = last
    tail_ref[...] = last
    return y


def _conv_strips(buf_ref, pre_ref, w_ref, width, tm, col_lo, col_hi, short, emit):
    rows = min(STRIP, tm)
    row_in_seq = lax.broadcasted_iota(jnp.int32, (rows, LANES), 0) & (SHORT_LEN - 1)
    for c0 in range(col_lo, col_hi, LANES):
        cols = slice(c0, c0 + LANES)
        taps = [w_ref[k:k + 1, cols] for k in range(width)]
        for r0 in range(0, tm, rows):
            if short:
                xv = buf_ref[SUBLANES + r0:SUBLANES + r0 + rows, cols]
                pv = pre_ref[r0:r0 + rows, cols]
                y = xv * taps[width - 1]
                for k in range(1, width):
                    xk = jnp.where(row_in_seq < k, pltpu.roll(pv, rows - (SHORT_LEN - k), axis=0),
                                   pltpu.roll(xv, k, axis=0))
                    y = y + xk * taps[width - 1 - k]
            else:
                ext = buf_ref[r0:r0 + rows + SUBLANES, cols]
                y = ext[SUBLANES:] * taps[width - 1]
                for k in range(1, width):
                    y = y + pltpu.roll(ext, k, axis=0)[SUBLANES:] * taps[width - 1 - k]
            emit(r0, cols, y)


def _sc_mixer_kernel(x_ref, pre_ref, nw_ref, win_ref, cw_ref, wout_ref, o_ref, tail_ref, *scratch, short):
    x = x_ref[...]
    d = x.shape[1]
    h = _rmsnorm(x, nw_ref[...]).astype(BF16)
    bcx = _dot(h, win_ref[...])
    p = bcx[:, d:2 * d] * bcx[:, 2 * d:]
    v = _conv_step(p, pre_ref, scratch[0] if scratch else None, tail_ref, cw_ref, 3, short)
    y = _dot((bcx[:, :d] * v).astype(BF16), wout_ref[...])
    o_ref[...] = x + y


def _conv_ffn_kernel(x_ref, pre_ref, nw_ref, wup_ref, wgate_ref, cw_ref, wdown_ref, nf_ref,
                     o_ref, tail_ref, *scratch, short, final_norm):
    x = x_ref[...]
    h = _rmsnorm(x, nw_ref[...]).astype(BF16)
    u = _dot(h, wup_ref[...])
    uc = _conv_step(u, pre_ref, scratch[0] if scratch else None, tail_ref, cw_ref, 3, short)
    g = _dot(h, wgate_ref[...])
    a = (_silu(uc) * g).astype(BF16)
    y = x + _dot(a, wdown_ref[...])
    if final_norm:
        y = _rmsnorm(y, nf_ref[...])
    o_ref[...] = y


def _head_cols(v, h0, lo):
    return jnp.where(lo, v[:, h0:h0 + 1], v[:, h0 + 1:h0 + 2])


def _ssm_in_kernel(x_ref, pre_ref, nw_ref, w_ref, wdt_ref, cw_ref, cb_ref, dtb_ref, alog_ref,
                   z_ref, xbc_ref, *rest, short):
    if short:
        dt_ref, tail_ref, buf_ref = rest
    else:
        cum_ref, cspt_ref, we_ref, da_ref, tail_ref, buf_ref = rest
    tm = x_ref.shape[0]
    h = _rmsnorm(x_ref[...], nw_ref[...]).astype(BF16)

    if not short:
        @pl.when(pl.program_id(1) == 0)
        def _():
            buf_ref[0:SUBLANES, :] = pre_ref[...]

    def dot_xbc(i):
        cols = slice(i * COL_BLOCK, (i + 1) * COL_BLOCK)
        buf_ref[SUBLANES:, cols] = _dot(h, w_ref[:, D_INNER + i * COL_BLOCK:D_INNER + (i + 1) * COL_BLOCK])

    def dot_z(i):
        cols = slice(i * COL_BLOCK, (i + 1) * COL_BLOCK)
        z_ref[:, cols] = _dot(h, w_ref[:, cols]).astype(BF16)

    def emit(r0, cols, y):
        xbc_ref[r0:r0 + y.shape[0], cols] = _silu(y + cb_ref[:, cols]).astype(BF16)

    def conv_block(i):
        _conv_strips(buf_ref, pre_ref, cw_ref, 4, tm, i * COL_BLOCK, (i + 1) * COL_BLOCK, short, emit)

    n_x, n_z = CONV_DIM // COL_BLOCK, D_INNER // COL_BLOCK
    dot_xbc(0)
    for i in range(n_x):
        if i + 1 < n_x:
            dot_xbc(i + 1)
        if i < n_z:
            dot_z(i)
        conv_block(i)
    for i in range(n_x, n_z):
        dot_z(i)
    dt = _softplus(_dot(h, wdt_ref[...]) + dtb_ref[...])

    if short:
        for s in range(tm // SHORT_LEN):
            tail_ref[s] = buf_ref[SUBLANES + (s + 1) * SHORT_LEN - 3:SUBLANES + (s + 1) * SHORT_LEN, :]
        dt_ref[...] = dt
        return

    tail_ref[...] = buf_ref[tm:, :]
    buf_ref[0:SUBLANES, :] = buf_ref[tm:, :]

    a = -jnp.exp(alog_ref[...])
    li = lax.broadcasted_iota(jnp.int32, (CHUNK, CHUNK), 0)
    si = lax.broadcasted_iota(jnp.int32, (CHUNK, CHUNK), 1)
    tri = (si <= li).astype(BF16)
    lo8 = lax.broadcasted_iota(jnp.int32, (SUBLANES, LANES), 1) < HEAD_DIM
    pad = CHUNK - tm if tm < CHUNK else 0
    for c in range(max(tm // CHUNK, 1)):
        if pad:
            dt_c = jnp.concatenate([jnp.zeros((pad, LANES), F32), dt], axis=0)
        else:
            dt_c = dt[c * CHUNK:(c + 1) * CHUNK]
        rows = slice(c * CHUNK, (c + 1) * CHUNK)
        cum = _dot_exact_lhs(tri, dt_c * a)
        end = cum[CHUNK - 1:CHUNK, :]
        cum_ref[rows, :] = cum
        cspt_ref[:, rows] = (cum - jnp.log(dt_c)).T
        we_ref[rows, 0:LANES] = (jnp.exp(end - cum) * dt_c).astype(BF16)
        we_ref[rows, LANES:2 * LANES] = jnp.exp(cum).astype(BF16)
        e_end = jnp.broadcast_to(jnp.exp(end), (SUBLANES, LANES))
        for pr in range(N_HEADS // 2):
            da_ref[c * SUBLANES:(c + 1) * SUBLANES, pr * LANES:(pr + 1) * LANES] = _head_cols(e_end, 2 * pr, lo8)


def _ssd_long_kernel(x_ref, z_ref, xbc_ref, cum_ref, cspt_ref, we_ref, da_ref, s0_ref, e_ref,
                     dskip_ref, nw_ref, wout_ref, o_ref, sout_ref, st_ref):
    q = CHUNK

    @pl.when(pl.program_id(1) == 0)
    def _():
        st_ref[...] = s0_ref[...].T

    cum = cum_ref[...]
    cspt = cspt_ref[...]
    li = lax.broadcasted_iota(jnp.int32, (q, q), 0)
    si = lax.broadcasted_iota(jnp.int32, (q, q), 1)
    causal = si <= li
    lane = lax.broadcasted_iota(jnp.int32, (q, LANES), 1)
    m_lo = (lane < HEAD_DIM).astype(BF16)
    m_hi = (lane >= HEAD_DIM).astype(BF16)
    w_end = we_ref[:, 0:LANES]
    e_cum = we_ref[:, LANES:2 * LANES]

    normed = []
    for g in range(N_GROUPS):
        gl = slice(g * GROUP_W, (g + 1) * GROUP_W)
        bg = xbc_ref[:, D_INNER + g * D_STATE:D_INNER + (g + 1) * D_STATE]
        cg = xbc_ref[:, D_INNER + BC_W + g * D_STATE:D_INNER + BC_W + (g + 1) * D_STATE]
        xs_b = xbc_ref[:, gl]
        xs = xs_b.astype(F32)
        st_g = st_ref[:, gl]
        cb = _dot_nt(cg, bg)
        y_off = _dot(cg, st_g.astype(BF16))
        w_exp = _dot(w_end, e_ref[:, gl])
        e_exp = _dot(e_cum, e_ref[:, gl])
        xw = (xs * w_exp).astype(BF16)
        bg_t = bg.astype(F32).T.astype(BF16)
        st_ref[:, gl] = st_g * da_ref[0:1, gl] + _dot(bg_t, xw)

        ys = []
        for pr in range(HEADS_PER_GROUP // 2):
            h0 = g * HEADS_PER_GROUP + 2 * pr
            ms = []
            for h in (h0, h0 + 1):
                seg = cum[:, h:h + 1] - cspt[h:h + 1, :]
                ms.append((cb * jnp.exp(jnp.where(causal, seg, NEG_BIG))).astype(BF16))
            xp = xs_b[:, pr * LANES:(pr + 1) * LANES]
            rhs = jnp.concatenate([xp * m_lo, xp * m_hi], axis=0)
            ys.append(_dot(jnp.concatenate(ms, axis=1), rhs))
        y = jnp.concatenate(ys, axis=1) + y_off * e_exp + dskip_ref[:, gl] * xs
        y = y * _silu(z_ref[:, gl].astype(F32))
        y = y * lax.rsqrt(jnp.mean(y * y, axis=-1, keepdims=True) + EPS)
        normed.append((y * nw_ref[:, gl]).astype(BF16))

    o_ref[...] = x_ref[...] + _dot(jnp.concatenate(normed, axis=1), wout_ref[...])

    @pl.when(pl.program_id(1) == pl.num_programs(1) - 1)
    def _():
        sout_ref[...] = st_ref[...].T


def _ssd_short_kernel(x_ref, z_ref, xbc_ref, dt_ref, s0_ref, alog_ref, dskip_ref, nw_ref, wout_ref,
                      o_ref, sout_ref):
    q = x_ref.shape[0]
    n_seq = q // SHORT_LEN
    xbc = xbc_ref[...]
    xs = xbc[:, :D_INNER].astype(F32)
    bm = xbc[:, D_INNER:D_INNER + BC_W]
    cm = xbc[:, D_INNER + BC_W:]
    dt = dt_ref[...]
    a = -jnp.exp(alog_ref[...])
    dta = dt * a

    li = lax.broadcasted_iota(jnp.int32, (q, q), 0)
    si = lax.broadcasted_iota(jnp.int32, (q, q), 1)
    same_seq = (li // SHORT_LEN) == (si // SHORT_LEN)
    causal = same_seq & (si <= li)
    cum = _dot_exact_lhs(causal.astype(BF16), dta)
    cum_t = cum.T
    cum_end = _dot_exact_lhs(same_seq.astype(BF16), dta)
    w_end = jnp.exp(cum_end - cum) * dt
    ecum = jnp.exp(cum)
    e_end = jnp.exp(cum_end)

    lane = lax.broadcasted_iota(jnp.int32, (q, LANES), 1)
    lo = lane < HEAD_DIM
    row_seq = lax.broadcasted_iota(jnp.int32, (q, LANES), 0) // SHORT_LEN

    y_off = []
    for g in range(N_GROUPS):
        cg = cm[:, g * D_STATE:(g + 1) * D_STATE].astype(F32)
        acc = jnp.zeros((q, GROUP_W), F32)
        for s in range(n_seq):
            cmask = jnp.where(row_seq == s, cg, 0.0).astype(BF16)
            sg = s0_ref[s, g * GROUP_W:(g + 1) * GROUP_W, :].astype(BF16)
            acc = acc + _dot_nt(cmask, sg)
        y_off.append(acc)

    y_pieces = []
    xw_pieces = []
    da_pieces = []
    for g in range(N_GROUPS):
        bg = bm[:, g * D_STATE:(g + 1) * D_STATE]
        cg = cm[:, g * D_STATE:(g + 1) * D_STATE]
        cb = _dot_nt(cg, bg)
        for pr in range(HEADS_PER_GROUP // 2):
            h0 = g * HEADS_PER_GROUP + 2 * pr
            l0 = h0 * HEAD_DIM
            lhs = []
            for h in (h0, h0 + 1):
                seg = cum[:, h:h + 1] - cum_t[h:h + 1, :]
                lhs.append((cb * jnp.exp(jnp.where(causal, seg, NEG_BIG))).astype(BF16))
            x_pair = xs[:, l0:l0 + LANES]
            xdt = x_pair * _head_cols(dt, h0, lo)
            rhs = jnp.concatenate([jnp.where(lo, xdt, 0.0), jnp.where(lo, 0.0, xdt)], axis=0).astype(BF16)
            y_pair = _dot(jnp.concatenate(lhs, axis=1), rhs)
            y_pair = y_pair + y_off[g][:, 2 * pr * HEAD_DIM:2 * pr * HEAD_DIM + LANES] * _head_cols(ecum, h0, lo)
            y_pieces.append(y_pair)
            xw_pieces.append(x_pair * _head_cols(w_end, h0, lo))
            da_pieces.append(_head_cols(e_end, h0, lo))

    pad = jnp.zeros((LANES - q, D_INNER), F32)
    xw_t = jnp.concatenate([jnp.concatenate(xw_pieces, axis=1), pad], axis=0).T.astype(BF16)
    da_t = jnp.concatenate([jnp.concatenate(da_pieces, axis=1), pad], axis=0).T
    row_seq_k = lax.broadcasted_iota(jnp.int32, (LANES, D_STATE), 0) // SHORT_LEN
    bm_k = jnp.concatenate([bm.astype(F32), jnp.zeros((LANES - q, BC_W), F32)], axis=0)
    for s in range(n_seq):
        col = s * SHORT_LEN + SHORT_LEN - 1
        for g in range(N_GROUPS):
            rows = slice(g * GROUP_W, (g + 1) * GROUP_W)
            bmask = jnp.where(row_seq_k == s, bm_k[:, g * D_STATE:(g + 1) * D_STATE], 0.0).astype(BF16)
            upd = _dot(xw_t[rows, :], bmask)
            sout_ref[s, rows, :] = s0_ref[s, rows, :] * da_t[rows, col:col + 1] + upd

    y = jnp.concatenate(y_pieces, axis=1) + dskip_ref[...] * xs
    y = y * _silu(z_ref[...].astype(F32))
    normed = []
    for g in range(N_GROUPS):
        yg = y[:, g * GROUP_W:(g + 1) * GROUP_W]
        normed.append(yg * lax.rsqrt(jnp.mean(yg * yg, axis=-1, keepdims=True) + EPS))
    yn = (jnp.concatenate(normed, axis=1) * nw_ref[...]).astype(BF16)
    o_ref[...] = x_ref[...] + _dot(yn, wout_ref[...])


def _const_spec(shape):
    return pl.BlockSpec(shape, lambda *_: (0,) * len(shape), pipeline_mode=pl.Buffered(1))


def _layer_spec(arr, layer):
    return pl.BlockSpec((None,) + arr.shape[1:], lambda *_: (layer, 0, 0), pipeline_mode=pl.Buffered(1))


def _token_layout(n_tok, n_seq, tm, short, shared_pre):
    if short:
        grid = (n_tok // tm,)
        tok = lambda i: (i, 0)

        def carried(c, width):
            tail_spec = pl.BlockSpec((tm // SHORT_LEN, width - 1, c), lambda i: (i, 0, 0))
            tail_shape = jax.ShapeDtypeStruct((n_tok // SHORT_LEN, width - 1, c), F32)
            return pl.BlockSpec((tm, c), tok), tail_spec, tail_shape

        return grid, tok, ("arbitrary",), carried
    tiles = n_tok // n_seq // tm
    grid = (n_seq, tiles)
    tok = lambda b, j: (b * tiles + j, 0)
    per_seq = lambda b, j: (b, 0)
    pre_map = (lambda b, j: (0, 0)) if shared_pre else per_seq

    def carried(c, width):
        return (pl.BlockSpec((SUBLANES, c), pre_map), pl.BlockSpec((SUBLANES, c), per_seq),
                jax.ShapeDtypeStruct((n_seq * SUBLANES, c), F32))

    return grid, tok, ("arbitrary", "arbitrary"), carried


def _params(sem):
    return pltpu.CompilerParams(dimension_semantics=sem, vmem_limit_bytes=VMEM_LIMIT)


def _sc_mixer(x, pre, w, layer, *, n_seq, tm, short, shared_pre):
    n_tok, d = x.shape
    grid, tok, sem, carried = _token_layout(n_tok, n_seq, tm, short, shared_pre)
    pre_spec, tail_spec, tail_shape = carried(d, 3)
    return pl.pallas_call(
        functools.partial(_sc_mixer_kernel, short=short),
        grid=grid,
        in_specs=[pl.BlockSpec((tm, d), tok), pre_spec, _layer_spec(w["norm_mix"], 2 * layer),
                  _layer_spec(w["sc_in"], layer), _layer_spec(w["sc_cw"], layer), _layer_spec(w["sc_out"], layer)],
        out_specs=[pl.BlockSpec((tm, d), tok), tail_spec],
        out_shape=[jax.ShapeDtypeStruct((n_tok, d), F32), tail_shape],
        scratch_shapes=[] if short else [pltpu.VMEM((SUBLANES, d), F32)],
        compiler_params=_params(sem),
        name="sc_mixer_short" if short else "sc_mixer_long",
    )(x, pre, w["norm_mix"], w["sc_in"], w["sc_cw"], w["sc_out"])


def _conv_ffn(x, pre, w, layer, *, n_seq, tm, short, shared_pre, final_norm):
    n_tok, d = x.shape
    f = w["up"].shape[2]
    grid, tok, sem, carried = _token_layout(n_tok, n_seq, tm, short, shared_pre)
    pre_spec, tail_spec, tail_shape = carried(f, 3)
    return pl.pallas_call(
        functools.partial(_conv_ffn_kernel, short=short, final_norm=final_norm),
        grid=grid,
        in_specs=[pl.BlockSpec((tm, d), tok), pre_spec, _layer_spec(w["norm_ffn"], layer),
                  _layer_spec(w["up"], layer), _layer_spec(w["gate"], layer), _layer_spec(w["ffn_cw"], layer),
                  _layer_spec(w["down"], layer), _const_spec(w["nfin"].shape)],
        out_specs=[pl.BlockSpec((tm, d), tok), tail_spec],
        out_shape=[jax.ShapeDtypeStruct((n_tok, d), F32), tail_shape],
        scratch_shapes=[] if short else [pltpu.VMEM((SUBLANES, f), F32)],
        compiler_params=_params(sem),
        name="conv_ffn_short" if short else "conv_ffn_long",
    )(x, pre, w["norm_ffn"], w["up"], w["gate"], w["ffn_cw"], w["down"], w["nfin"])


def _ssm_in(x, pre, w, layer, *, n_seq, tm, short, shared_pre):
    n_tok, d = x.shape
    grid, tok, sem, carried = _token_layout(n_tok, n_seq, tm, short, shared_pre)
    pre_spec, tail_spec, tail_shape = carried(CONV_DIM, 4)
    out_specs = [pl.BlockSpec((tm, D_INNER), tok), pl.BlockSpec((tm, CONV_DIM), tok)]
    out_shape = [jax.ShapeDtypeStruct((n_tok, D_INNER), BF16), jax.ShapeDtypeStruct((n_tok, CONV_DIM), BF16)]
    if short:
        out_specs.append(pl.BlockSpec((tm, LANES), tok))
        out_shape.append(jax.ShapeDtypeStruct((n_tok, LANES), F32))
    else:
        ctm = max(tm, CHUNK)
        n_rows = n_tok // tm * ctm
        tiles = n_tok // n_seq // tm
        tok_t = lambda b, j: (0, b * tiles + j)
        da_rows = ctm // CHUNK * SUBLANES
        out_specs += [pl.BlockSpec((ctm, LANES), tok), pl.BlockSpec((LANES, ctm), tok_t),
                      pl.BlockSpec((ctm, 2 * LANES), tok), pl.BlockSpec((da_rows, D_INNER), tok)]
        out_shape += [jax.ShapeDtypeStruct((n_rows, LANES), F32), jax.ShapeDtypeStruct((LANES, n_rows), F32),
                      jax.ShapeDtypeStruct((n_rows, 2 * LANES), BF16),
                      jax.ShapeDtypeStruct((n_rows // CHUNK * SUBLANES, D_INNER), F32)]
    out_specs.append(tail_spec)
    out_shape.append(tail_shape)
    return pl.pallas_call(
        functools.partial(_ssm_in_kernel, short=short),
        grid=grid,
        in_specs=[pl.BlockSpec((tm, d), tok), pre_spec, _layer_spec(w["norm_mix"], 2 * layer + 1),
                  _layer_spec(w["ssm_in"], layer), _const_spec(w["w_dt"].shape), _layer_spec(w["ssm_cw"], layer),
                  _const_spec(w["ssm_cb"].shape), _const_spec(w["dtb"].shape), _const_spec(w["alog"].shape)],
        out_specs=out_specs,
        out_shape=out_shape,
        scratch_shapes=[pltpu.VMEM((tm + SUBLANES, CONV_DIM), F32)],
        compiler_params=_params(sem),
        name="ssm_in_short" if short else "ssm_in_long",
    )(x, pre, w["norm_mix"], w["ssm_in"], w["w_dt"], w["ssm_cw"], w["ssm_cb"], w["dtb"], w["alog"])


def _ssd_long(x, z, xbc, cum, cspt, we, da, s0, w, layer, *, n_seq, shared_s0):
    expand, dskip, nw, w_out = w["expand"], w["dskip"], w["ssm_nw"], w["ssm_out"]
    n_tok, d = x.shape
    chunks = n_tok // n_seq // CHUNK
    tok = lambda b, c: (b * chunks + c, 0)
    tok_t = lambda b, c: (0, b * chunks + c)
    s_spec = pl.BlockSpec((D_INNER, D_STATE), lambda b, c: (b, 0))
    s0_spec = pl.BlockSpec((D_INNER, D_STATE), lambda b, c: (0, 0)) if shared_s0 else s_spec
    return pl.pallas_call(
        _ssd_long_kernel,
        grid=(n_seq, chunks),
        in_specs=[pl.BlockSpec((CHUNK, d), tok), pl.BlockSpec((CHUNK, D_INNER), tok),
                  pl.BlockSpec((CHUNK, CONV_DIM), tok), pl.BlockSpec((CHUNK, LANES), tok),
                  pl.BlockSpec((LANES, CHUNK), tok_t), pl.BlockSpec((CHUNK, 2 * LANES), tok),
                  pl.BlockSpec((SUBLANES, D_INNER), tok), s0_spec,
                  _const_spec(expand.shape), _const_spec(dskip.shape), _const_spec(nw.shape), _layer_spec(w_out, layer)],
        out_specs=[pl.BlockSpec((CHUNK, d), tok), s_spec],
        out_shape=[jax.ShapeDtypeStruct((n_tok, d), F32), jax.ShapeDtypeStruct((n_seq * D_INNER, D_STATE), F32)],
        scratch_shapes=[pltpu.VMEM((D_STATE, D_INNER), F32)],
        compiler_params=_params(("arbitrary", "arbitrary")),
        name="ssd_long",
    )(x, z, xbc, cum, cspt, we, da, s0, expand, dskip, nw, w_out)


def _ssd_short(x, z, xbc, dt, s0, w, layer, *, tm):
    alog, dskip, nw, w_out = w["alog"], w["dskip"], w["ssm_nw"], w["ssm_out"]
    n_tok, d = x.shape
    n_seq = s0.shape[0]
    tok = lambda i: (i, 0)
    s_spec = pl.BlockSpec((tm // SHORT_LEN, D_INNER, D_STATE), lambda i: (i, 0, 0))
    return pl.pallas_call(
        _ssd_short_kernel,
        grid=(n_tok // tm,),
        in_specs=[pl.BlockSpec((tm, d), tok), pl.BlockSpec((tm, D_INNER), tok), pl.BlockSpec((tm, CONV_DIM), tok),
                  pl.BlockSpec((tm, LANES), tok), s_spec,
                  _const_spec(alog.shape), _const_spec(dskip.shape), _const_spec(nw.shape), _layer_spec(w_out, layer)],
        out_specs=[pl.BlockSpec((tm, d), tok), s_spec],
        out_shape=[jax.ShapeDtypeStruct((n_tok, d), F32), jax.ShapeDtypeStruct((n_seq, D_INNER, D_STATE), F32)],
        compiler_params=_params(("arbitrary",)),
        name="ssd_short",
    )(x, z, xbc, dt, s0, alog, dskip, nw, w_out)


def _trunk(x, pre_sc, pre_xbc, s0, pre_ffn, w, *, n_seq, short, tms, shared=False):
    kw = dict(n_seq=n_seq, short=short, shared_pre=shared)
    x1, t_sc = _sc_mixer(x, pre_sc, w, 0, tm=tms[0], **kw)
    x2, t_f0 = _conv_ffn(x1, pre_ffn[0], w, 0, tm=tms[1], final_norm=False, **kw)
    ssm = _ssm_in(x2, pre_xbc, w, 0, tm=tms[2], **kw)
    if short:
        z, xbc, dt, t_xbc = ssm
        x3, s_new = _ssd_short(x2, z, xbc, dt, s0, w, 0, tm=tms[3])
    else:
        z, xbc, cum, cspt, we, da, t_xbc = ssm
        pad_front = cum.shape[0] - x2.shape[0]
        padf = lambda t: jnp.pad(t, ((pad_front, 0), (0, 0))) if pad_front else t
        x3, s_new = _ssd_long(padf(x2), padf(z), padf(xbc), cum, cspt, we, da, s0, w, 0, n_seq=n_seq, shared_s0=shared)
        x3 = x3[pad_front:]
    y, t_f1 = _conv_ffn(x3, pre_ffn[1], w, 1, tm=tms[4], final_norm=True, **kw)
    return y, t_sc, t_xbc, s_new, (t_f0, t_f1)


def _right_align(cache):
    s, wm1, c = cache.shape
    return jnp.pad(cache, ((0, 0), (SUBLANES - wm1, 0), (0, 0))).reshape(s * SUBLANES, c)


def kernel(x_prompt, x_sample, cache_sc, cache_ssm_conv, state_ssm, cache_ffn_conv, meta_tokens, norm_mix, norm_ffn, norm_final, sc_w_in, sc_conv_w, sc_w_out, ssm_w_in, ssm_conv_w, ssm_conv_b, ssm_dt_bias, ssm_a_log, ssm_d, ssm_norm_w, ssm_w_out, ffn_w_up, ffn_w_gate, ffn_conv_w, ffn_w_down):
    b, seq, d = x_prompt.shape
    n_dec, dec_len, _ = x_sample.shape
    d_ff = ffn_w_up.shape[2]
    assert dec_len == SHORT_LEN and seq % CHUNK == 0 and N_META % SUBLANES == 0 and N_META <= CHUNK

    pad_heads = lambda v: jnp.pad(v.reshape(1, -1).astype(F32), ((0, 0), (0, LANES - N_HEADS)))
    head_of_lane = jnp.arange(D_INNER, dtype=jnp.int32)[None, :] // HEAD_DIM
    w = dict(
        norm_mix=norm_mix.reshape(-1, 1, d), norm_ffn=norm_ffn.reshape(-1, 1, d), nfin=norm_final.reshape(1, d),
        sc_in=sc_w_in.astype(BF16), sc_cw=sc_conv_w, sc_out=sc_w_out.astype(BF16),
        ssm_in=ssm_w_in.astype(BF16),
        w_dt=jnp.pad(ssm_w_in[0][:, D_INNER + CONV_DIM:], ((0, 0), (0, LANES - N_HEADS))).astype(BF16),
        ssm_cw=ssm_conv_w, ssm_cb=ssm_conv_b[0:1], dtb=pad_heads(ssm_dt_bias[0]), alog=pad_heads(ssm_a_log[0]),
        dskip=jnp.repeat(ssm_d[0], HEAD_DIM).reshape(1, -1), ssm_nw=ssm_norm_w[0:1], ssm_out=ssm_w_out.astype(BF16),
        expand=(jnp.arange(LANES, dtype=jnp.int32)[:, None] == head_of_lane).astype(BF16),
        up=ffn_w_up.astype(BF16), gate=ffn_w_gate.astype(BF16), ffn_cw=ffn_conv_w, down=ffn_w_down.astype(BF16),
    )

    zeros8 = lambda c: jnp.zeros((SUBLANES, c), F32)
    _, m_sc, m_xbc, m_state, m_ffn = _trunk(
        meta_tokens.astype(F32), zeros8(d), zeros8(CONV_DIM), jnp.zeros((D_INNER, D_STATE), F32),
        (zeros8(d_ff), zeros8(d_ff)), w, n_seq=1, short=False, tms=(N_META,) * 5)

    yp, p_sc, p_xbc, p_state, p_ffn = _trunk(
        x_prompt.reshape(b * seq, d), m_sc, m_xbc, m_state, m_ffn, w,
        n_seq=b, short=False, shared=True, tms=(512, 512, 512, CHUNK, 512))
    tail = lambda t, k: t.reshape(b, SUBLANES, -1)[:, SUBLANES - k:]
    out_prompt = (
        yp.reshape(b, seq, d),
        tail(p_sc, 2)[None], tail(p_xbc, 3)[None],
        p_state.reshape(1, b, N_HEADS, HEAD_DIM, D_STATE),
        jnp.stack([tail(p_ffn[0], 2), tail(p_ffn[1], 2)]),
    )

    ys, s_sc, s_xbc, s_state, s_ffn = _trunk(
        x_sample.reshape(n_dec * dec_len, d), _right_align(cache_sc[0]), _right_align(cache_ssm_conv[0]),
        state_ssm[0].reshape(n_dec, D_INNER, D_STATE),
        (_right_align(cache_ffn_conv[0]), _right_align(cache_ffn_conv[1])), w,
        n_seq=n_dec, short=True, tms=(256, 256, 256, 64, 256))
    out_sample = (
        ys.reshape(n_dec, dec_len, d),
        s_sc[None], s_xbc[None],
        s_state.reshape(1, n_dec, N_HEADS, HEAD_DIM, D_STATE),
        jnp.stack([s_ffn[0], s_ffn[1]]),
    )
    return (out_prompt[0], out_sample[0]) + out_prompt[1:] + out_sample[1:]
```

```python
import functools

import jax
import jax.numpy as jnp
from jax import lax
from jax.experimental import pallas as pl
from jax.experimental.pallas import tpu as pltpu

F32 = jnp.float32
BF16 = jnp.bfloat16

EPS = 1e-5
N_META = 16
HEAD_DIM = 64
N_HEADS = 32
N_GROUPS = 4
HEADS_PER_GROUP = N_HEADS // N_GROUPS
D_STATE = 128
D_INNER = N_HEADS * HEAD_DIM
GROUP_W = D_INNER // N_GROUPS
BC_W = N_GROUPS * D_STATE
CONV_DIM = D_INNER + 2 * BC_W
CHUNK = 128
SHORT_LEN = 8
SUBLANES = 8
LANES = 128
STRIP = 64
COL_BLOCK = 512
NEG_BIG = -1e30
NEG_LOG2E = -1.4426950408889634
VMEM_LIMIT = 56 * 1024 * 1024


def _rmsnorm(x, w):
    r = lax.rsqrt(jnp.mean(x * x, axis=-1, keepdims=True) + EPS)
    return x * r * w


def _silu(x):
    return x / (1.0 + jnp.exp2(x * NEG_LOG2E))


def _softplus(x):
    return jnp.maximum(x, 0.0) + jnp.log1p(jnp.exp(-jnp.abs(x)))


def _dot(a, b):
    return jnp.dot(a, b, preferred_element_type=F32)


def _dot_nt(a, b):
    return lax.dot_general(a, b, (((1,), (1,)), ((), ())), preferred_element_type=F32)


def _dot_exact_lhs(m_bf16, x):
    hi = x.astype(BF16)
    r1 = x - hi.astype(F32)
    mid = r1.astype(BF16)
    lo = (r1 - mid.astype(F32)).astype(BF16)
    return _dot(m_bf16, hi) + _dot(m_bf16, mid) + _dot(m_bf16, lo)


def _causal_conv(x, prev, w_ref, width, short):
    tm = x.shape[0]
    y = x * w_ref[width - 1:width, :]
    if short:
        row_in_seq = lax.broadcasted_iota(jnp.int32, x.shape, 0) & (SHORT_LEN - 1)
    else:
        row8 = lax.broadcasted_iota(jnp.int32, (SUBLANES, x.shape[1]), 0)
    for k in range(1, width):
        r = pltpu.roll(x, k, axis=0)
        if short:
            rp = pltpu.roll(prev, tm - (SHORT_LEN - k), axis=0)
            xk = jnp.where(row_in_seq < k, rp, r)
        else:
            top = jnp.where(row8 < k, pltpu.roll(prev, k, axis=0), r[0:SUBLANES])
            xk = top if tm == SUBLANES else jnp.concatenate([top, r[SUBLANES:]], axis=0)
        y = y + xk * w_ref[width - 1 - k:width - k, :]
    return y


def _conv_step(v, pre_ref, carry_ref, tail_ref, w_ref, width, short):
    if short:
        y = _causal_conv(v, pre_ref[...], w_ref, width, True)
        for s in range(v.shape[0] // SHORT_LEN):
            tail_ref[s] = v[(s + 1) * SHORT_LEN - (width - 1):(s + 1) * SHORT_LEN]
        return y

    @pl.when(pl.program_id(1) == 0)
    def _():
        carry_ref[...] = pre_ref[...]

    y = _causal_conv(v, carry_ref[...], w_ref, width, False)
    last = v[v.shape[0] - SUBLANES:]
    carry_ref[...] = last
    tail_ref[...] = last
    return y


def _conv_strips(buf_ref, pre_ref, w_ref, width, tm, col_lo, col_hi, short, emit):
    rows = min(STRIP, tm)
    row_in_seq = lax.broadcasted_iota(jnp.int32, (rows, LANES), 0) & (SHORT_LEN - 1)
    for c0 in range(col_lo, col_hi, LANES):
        cols = slice(c0, c0 + LANES)
        taps = [w_ref[k:k + 1, cols] for k in range(width)]
        for r0 in range(0, tm, rows):
            if short:
                xv = buf_ref[SUBLANES + r0:SUBLANES + r0 + rows, cols]
                pv = pre_ref[r0:r0 + rows, cols]
                y = xv * taps[width - 1]
                for k in range(1, width):
                    xk = jnp.where(row_in_seq < k, pltpu.roll(pv, rows - (SHORT_LEN - k), axis=0),
                                   pltpu.roll(xv, k, axis=0))
                    y = y + xk * taps[width - 1 - k]
            else:
                ext = buf_ref[r0:r0 + rows + SUBLANES, cols]
                y = ext[SUBLANES:] * taps[width - 1]
                for k in range(1, width):
                    y = y + pltpu.roll(ext, k, axis=0)[SUBLANES:] * taps[width - 1 - k]
            emit(r0, cols, y)


def _sc_mixer_kernel(x_ref, pre_ref, nw_ref, win_ref, cw_ref, wout_ref, o_ref, tail_ref, *scratch, short):
    x = x_ref[...]
    d = x.shape[1]
    h = _rmsnorm(x, nw_ref[...]).astype(BF16)
    bcx = _dot(h, win_ref[...])
    p = bcx[:, d:2 * d] * bcx[:, 2 * d:]
    v = _conv_step(p, pre_ref, scratch[0] if scratch else None, tail_ref, cw_ref, 3, short)
    y = _dot((bcx[:, :d] * v).astype(BF16), wout_ref[...])
    o_ref[...] = x + y


def _conv_ffn_kernel(x_ref, pre_ref, nw_ref, wup_ref, wgate_ref, cw_ref, wdown_ref, nf_ref,
                     o_ref, tail_ref, *scratch, short, final_norm):
    x = x_ref[...]
    h = _rmsnorm(x, nw_ref[...]).astype(BF16)
    u = _dot(h, wup_ref[...])
    uc = _conv_step(u, pre_ref, scratch[0] if scratch else None, tail_ref, cw_ref, 3, short)
    g = _dot(h, wgate_ref[...])
    a = (_silu(uc) * g).astype(BF16)
    y = x + _dot(a, wdown_ref[...])
    if final_norm:
        y = _rmsnorm(y, nf_ref[...])
    o_ref[...] = y


def _head_cols(v, h0, lo):
    return jnp.where(lo, v[:, h0:h0 + 1], v[:, h0 + 1:h0 + 2])


def _ssm_in_kernel(x_ref, pre_ref, nw_ref, w_ref, wdt_ref, cw_ref, cb_ref, dtb_ref, alog_ref,
                   z_ref, xbc_ref, *rest, short):
    tm = x_ref.shape[0]
    if short:
        dt_ref, tail_ref, buf_ref = rest
    else:
        cum_ref, cspt_ref, we_ref, da_ref, tail_ref, buf_ref = rest

        @pl.when(pl.program_id(1) == 0)
        def _():
            buf_ref[0:SUBLANES, :] = pre_ref[...]

    h = _rmsnorm(x_ref[...], nw_ref[...]).astype(BF16)

    def dot_xbc(i):
        cols = slice(i * COL_BLOCK, (i + 1) * COL_BLOCK)
        buf_ref[SUBLANES:, cols] = _dot(h, w_ref[:, D_INNER + i * COL_BLOCK:D_INNER + (i + 1) * COL_BLOCK])

    def dot_z(i):
        cols = slice(i * COL_BLOCK, (i + 1) * COL_BLOCK)
        z_ref[:, cols] = _dot(h, w_ref[:, cols]).astype(BF16)

    def emit(r0, cols, y):
        xbc_ref[r0:r0 + y.shape[0], cols] = _silu(y + cb_ref[:, cols]).astype(BF16)

    def conv_block(i):
        _conv_strips(buf_ref, pre_ref, cw_ref, 4, tm, i * COL_BLOCK, (i + 1) * COL_BLOCK, short, emit)

    n_x, n_z = CONV_DIM // COL_BLOCK, D_INNER // COL_BLOCK
    dot_xbc(0)
    for i in range(n_x):
        if i + 1 < n_x:
            dot_xbc(i + 1)
        if i < n_z:
            dot_z(i)
        conv_block(i)
    for i in range(n_x, n_z):
        dot_z(i)
    dt = _softplus(_dot(h, wdt_ref[...]) + dtb_ref[...])

    if short:
        for s in range(tm // SHORT_LEN):
            tail_ref[s] = buf_ref[SUBLANES + (s + 1) * SHORT_LEN - 3:SUBLANES + (s + 1) * SHORT_LEN, :]
        dt_ref[...] = dt
    else:
        tail_ref[...] = buf_ref[tm:, :]
        buf_ref[0:SUBLANES, :] = buf_ref[tm:, :]
        _decay_terms(dt, alog_ref, cum_ref, cspt_ref, we_ref, da_ref)


def _decay_terms(dt, alog_ref, cum_ref, cspt_ref, we_ref, da_ref):
    tm = dt.shape[0]
    a = -jnp.exp(alog_ref[...])
    li = lax.broadcasted_iota(jnp.int32, (CHUNK, CHUNK), 0)
    si = lax.broadcasted_iota(jnp.int32, (CHUNK, CHUNK), 1)
    tri = (si <= li).astype(BF16)
    lo8 = lax.broadcasted_iota(jnp.int32, (SUBLANES, LANES), 1) < HEAD_DIM
    pad = CHUNK - tm if tm < CHUNK else 0
    for c in range(max(tm // CHUNK, 1)):
        if pad:
            dt_c = jnp.concatenate([jnp.zeros((pad, LANES), F32), dt], axis=0)
        else:
            dt_c = dt[c * CHUNK:(c + 1) * CHUNK]
        rows = slice(c * CHUNK, (c + 1) * CHUNK)
        cum = _dot_exact_lhs(tri, dt_c * a)
        end = cum[CHUNK - 1:CHUNK, :]
        cum_ref[rows, :] = cum
        cspt_ref[:, rows] = (cum - jnp.log(dt_c)).T
        we_ref[rows, 0:LANES] = (jnp.exp(end - cum) * dt_c).astype(BF16)
        we_ref[rows, LANES:2 * LANES] = jnp.exp(cum).astype(BF16)
        e_end = jnp.broadcast_to(jnp.exp(end), (SUBLANES, LANES))
        for pr in range(N_HEADS // 2):
            da_ref[c * SUBLANES:(c + 1) * SUBLANES, pr * LANES:(pr + 1) * LANES] = _head_cols(e_end, 2 * pr, lo8)


def _ssd_long_kernel(x_ref, z_ref, xbc_ref, cum_ref, cspt_ref, we_ref, da_ref, s0_ref, e_ref,
                     dskip_ref, nw_ref, wout_ref, o_ref, sout_ref, st_ref):
    q = CHUNK

    @pl.when(pl.program_id(1) == 0)
    def _():
        st_ref[...] = s0_ref[...].T

    li = lax.broadcasted_iota(jnp.int32, (q, q), 0)
    si = lax.broadcasted_iota(jnp.int32, (q, q), 1)
    causal = si <= li
    lane = lax.broadcasted_iota(jnp.int32, (q, LANES), 1)
    m_lo = (lane < HEAD_DIM).astype(BF16)
    m_hi = (lane >= HEAD_DIM).astype(BF16)

    normed_chunks = []
    for c in range(x_ref.shape[0] // q):
        rows = slice(c * q, (c + 1) * q)
        cum = cum_ref[rows, :]
        cspt = cspt_ref[:, rows]
        w_end = we_ref[rows, 0:LANES]
        e_cum = we_ref[rows, LANES:2 * LANES]
        normed = []
        for g in range(N_GROUPS):
            gl = slice(g * GROUP_W, (g + 1) * GROUP_W)
            bg = xbc_ref[rows, D_INNER + g * D_STATE:D_INNER + (g + 1) * D_STATE]
            cg = xbc_ref[rows, D_INNER + BC_W + g * D_STATE:D_INNER + BC_W + (g + 1) * D_STATE]
            xs_b = xbc_ref[rows, gl]
            xs = xs_b.astype(F32)
            st_g = st_ref[:, gl]
            cb = _dot_nt(cg, bg)
            y_off = _dot(cg, st_g.astype(BF16))
            w_exp = _dot(w_end, e_ref[:, gl])
            e_exp = _dot(e_cum, e_ref[:, gl])
            xw = xs_b * w_exp.astype(BF16)
            bg_t = bg.astype(F32).T.astype(BF16)
            st_ref[:, gl] = st_g * da_ref[c * SUBLANES:c * SUBLANES + 1, gl] + _dot(bg_t, xw)

            ys = []
            for pr in range(HEADS_PER_GROUP // 2):
                h0 = g * HEADS_PER_GROUP + 2 * pr
                ms = []
                for h in (h0, h0 + 1):
                    seg = cum[:, h:h + 1] - cspt[h:h + 1, :]
                    ms.append((cb * jnp.exp(jnp.where(causal, seg, NEG_BIG))).astype(BF16))
                xp = xs_b[:, pr * LANES:(pr + 1) * LANES]
                rhs = jnp.concatenate([xp * m_lo, xp * m_hi], axis=0)
                ys.append(_dot(jnp.concatenate(ms, axis=1), rhs))
            y = jnp.concatenate(ys, axis=1) + y_off * e_exp + dskip_ref[:, gl] * xs
            y = y * _silu(z_ref[rows, gl].astype(F32))
            y = y * lax.rsqrt(jnp.mean(y * y, axis=-1, keepdims=True) + EPS)
            normed.append((y * nw_ref[:, gl]).astype(BF16))
        normed_chunks.append(jnp.concatenate(normed, axis=1))

    o_ref[...] = x_ref[...] + _dot(jnp.concatenate(normed_chunks, axis=0), wout_ref[...])

    @pl.when(pl.program_id(1) == pl.num_programs(1) - 1)
    def _():
        sout_ref[...] = st_ref[...].T


def _ssd_short_kernel(x_ref, z_ref, xbc_ref, dt_ref, s0_ref, alog_ref, dskip_ref, nw_ref, wout_ref,
                      o_ref, sout_ref):
    q = x_ref.shape[0]
    n_seq = q // SHORT_LEN
    xbc = xbc_ref[...]
    xs = xbc[:, :D_INNER].astype(F32)
    bm = xbc[:, D_INNER:D_INNER + BC_W]
    cm = xbc[:, D_INNER + BC_W:]
    dt = dt_ref[...]
    a = -jnp.exp(alog_ref[...])
    dta = dt * a

    li = lax.broadcasted_iota(jnp.int32, (q, q), 0)
    si = lax.broadcasted_iota(jnp.int32, (q, q), 1)
    same_seq = (li // SHORT_LEN) == (si // SHORT_LEN)
    causal = same_seq & (si <= li)
    cum = _dot_exact_lhs(causal.astype(BF16), dta)
    cum_t = cum.T
    cum_end = _dot_exact_lhs(same_seq.astype(BF16), dta)
    w_end = jnp.exp(cum_end - cum) * dt
    ecum = jnp.exp(cum)
    e_end = jnp.exp(cum_end)

    lane = lax.broadcasted_iota(jnp.int32, (q, LANES), 1)
    lo = lane < HEAD_DIM
    row_seq = lax.broadcasted_iota(jnp.int32, (q, LANES), 0) // SHORT_LEN

    y_off = []
    for g in range(N_GROUPS):
        cg = cm[:, g * D_STATE:(g + 1) * D_STATE].astype(F32)
        acc = jnp.zeros((q, GROUP_W), F32)
        for s in range(n_seq):
            cmask = jnp.where(row_seq == s, cg, 0.0).astype(BF16)
            sg = s0_ref[s, g * GROUP_W:(g + 1) * GROUP_W, :].astype(BF16)
            acc = acc + _dot_nt(cmask, sg)
        y_off.append(acc)

    y_pieces = []
    xw_pieces = []
    da_pieces = []
    for g in range(N_GROUPS):
        bg = bm[:, g * D_STATE:(g + 1) * D_STATE]
        cg = cm[:, g * D_STATE:(g + 1) * D_STATE]
        cb = _dot_nt(cg, bg)
        for pr in range(HEADS_PER_GROUP // 2):
            h0 = g * HEADS_PER_GROUP + 2 * pr
            l0 = h0 * HEAD_DIM
            lhs = []
            for h in (h0, h0 + 1):
                seg = cum[:, h:h + 1] - cum_t[h:h + 1, :]
                lhs.append((cb * jnp.exp(jnp.where(causal, seg, NEG_BIG))).astype(BF16))
            x_pair = xs[:, l0:l0 + LANES]
            xdt = x_pair * _head_cols(dt, h0, lo)
            rhs = jnp.concatenate([jnp.where(lo, xdt, 0.0), jnp.where(lo, 0.0, xdt)], axis=0).astype(BF16)
            y_pair = _dot(jnp.concatenate(lhs, axis=1), rhs)
            y_pair = y_pair + y_off[g][:, 2 * pr * HEAD_DIM:2 * pr * HEAD_DIM + LANES] * _head_cols(ecum, h0, lo)
            y_pieces.append(y_pair)
            xw_pieces.append(x_pair * _head_cols(w_end, h0, lo))
            da_pieces.append(_head_cols(e_end, h0, lo))

    pad = jnp.zeros((LANES - q, D_INNER), F32)
    xw_t = jnp.concatenate([jnp.concatenate(xw_pieces, axis=1), pad], axis=0).T.astype(BF16)
    da_t = jnp.concatenate([jnp.concatenate(da_pieces, axis=1), pad], axis=0).T
    row_seq_k = lax.broadcasted_iota(jnp.int32, (LANES, D_STATE), 0) // SHORT_LEN
    bm_k = jnp.concatenate([bm.astype(F32), jnp.zeros((LANES - q, BC_W), F32)], axis=0)
    for s in range(n_seq):
        col = s * SHORT_LEN + SHORT_LEN - 1
        for g in range(N_GROUPS):
            rows = slice(g * GROUP_W, (g + 1) * GROUP_W)
            bmask = jnp.where(row_seq_k == s, bm_k[:, g * D_STATE:(g + 1) * D_STATE], 0.0).astype(BF16)
            upd = _dot(xw_t[rows, :], bmask)
            sout_ref[s, rows, :] = s0_ref[s, rows, :] * da_t[rows, col:col + 1] + upd

    y = jnp.concatenate(y_pieces, axis=1) + dskip_ref[...] * xs
    y = y * _silu(z_ref[...].astype(F32))
    normed = []
    for g in range(N_GROUPS):
        yg = y[:, g * GROUP_W:(g + 1) * GROUP_W]
        normed.append(yg * lax.rsqrt(jnp.mean(yg * yg, axis=-1, keepdims=True) + EPS))
    yn = (jnp.concatenate(normed, axis=1) * nw_ref[...]).astype(BF16)
    o_ref[...] = x_ref[...] + _dot(yn, wout_ref[...])


def _const_spec(shape):
    return pl.BlockSpec(shape, lambda *_: (0,) * len(shape), pipeline_mode=pl.Buffered(1))


def _layer_spec(arr, layer):
    return pl.BlockSpec((None,) + arr.shape[1:], lambda *_: (layer, 0, 0), pipeline_mode=pl.Buffered(1))


def _token_layout(n_tok, n_seq, tm, short, shared_pre):
    if short:
        grid = (n_tok // tm,)
        tok = lambda i: (i, 0)

        def carried(c, width):
            tail_spec = pl.BlockSpec((tm // SHORT_LEN, width - 1, c), lambda i: (i, 0, 0))
            tail_shape = jax.ShapeDtypeStruct((n_tok // SHORT_LEN, width - 1, c), F32)
            return pl.BlockSpec((tm, c), tok), tail_spec, tail_shape

        return grid, tok, ("arbitrary",), carried
    tiles = n_tok // n_seq // tm
    grid = (n_seq, tiles)
    tok = lambda b, j: (b * tiles + j, 0)
    per_seq = lambda b, j: (b, 0)
    pre_map = (lambda b, j: (0, 0)) if shared_pre else per_seq

    def carried(c, width):
        return (pl.BlockSpec((SUBLANES, c), pre_map), pl.BlockSpec((SUBLANES, c), per_seq),
                jax.ShapeDtypeStruct((n_seq * SUBLANES, c), F32))

    return grid, tok, ("arbitrary", "arbitrary"), carried


def _params(sem):
    return pltpu.CompilerParams(dimension_semantics=sem, vmem_limit_bytes=VMEM_LIMIT)


def _sc_mixer(x, pre, w, layer, *, n_seq, tm, short, shared_pre):
    n_tok, d = x.shape
    grid, tok, sem, carried = _token_layout(n_tok, n_seq, tm, short, shared_pre)
    pre_spec, tail_spec, tail_shape = carried(d, 3)
    return pl.pallas_call(
        functools.partial(_sc_mixer_kernel, short=short),
        grid=grid,
        in_specs=[pl.BlockSpec((tm, d), tok), pre_spec, _layer_spec(w["norm_mix"], 2 * layer),
                  _layer_spec(w["sc_in"], layer), _layer_spec(w["sc_cw"], layer), _layer_spec(w["sc_out"], layer)],
        out_specs=[pl.BlockSpec((tm, d), tok), tail_spec],
        out_shape=[jax.ShapeDtypeStruct((n_tok, d), F32), tail_shape],
        scratch_shapes=[] if short else [pltpu.VMEM((SUBLANES, d), F32)],
        compiler_params=_params(sem),
        name="sc_mixer_short" if short else "sc_mixer_long",
    )(x, pre, w["norm_mix"], w["sc_in"], w["sc_cw"], w["sc_out"])


def _conv_ffn(x, pre, w, layer, *, n_seq, tm, short, shared_pre, final_norm):
    n_tok, d = x.shape
    f = w["up"].shape[2]
    grid, tok, sem, carried = _token_layout(n_tok, n_seq, tm, short, shared_pre)
    pre_spec, tail_spec, tail_shape = carried(f, 3)
    return pl.pallas_call(
        functools.partial(_conv_ffn_kernel, short=short, final_norm=final_norm),
        grid=grid,
        in_specs=[pl.BlockSpec((tm, d), tok), pre_spec, _layer_spec(w["norm_ffn"], layer),
                  _layer_spec(w["up"], layer), _layer_spec(w["gate"], layer), _layer_spec(w["ffn_cw"], layer),
                  _layer_spec(w["down"], layer), _const_spec(w["nfin"].shape)],
        out_specs=[pl.BlockSpec((tm, d), tok), tail_spec],
        out_shape=[jax.ShapeDtypeStruct((n_tok, d), F32), tail_shape],
        scratch_shapes=[] if short else [pltpu.VMEM((SUBLANES, f), F32)],
        compiler_params=_params(sem),
        name="conv_ffn_short" if short else "conv_ffn_long",
    )(x, pre, w["norm_ffn"], w["up"], w["gate"], w["ffn_cw"], w["down"], w["nfin"])


def _ssm_in(x, pre, w, layer, *, n_seq, tm, short, shared_pre):
    n_tok, d = x.shape
    params = [w["norm_mix"], w["ssm_in"], w["w_dt"], w["ssm_cw"], w["ssm_cb"], w["dtb"], w["alog"]]
    param_specs = [_layer_spec(w["norm_mix"], 2 * layer + 1), _layer_spec(w["ssm_in"], layer),
                   _const_spec(w["w_dt"].shape), _layer_spec(w["ssm_cw"], layer),
                   _const_spec(w["ssm_cb"].shape), _const_spec(w["dtb"].shape), _const_spec(w["alog"].shape)]
    out_shape = [jax.ShapeDtypeStruct((n_tok, D_INNER), BF16), jax.ShapeDtypeStruct((n_tok, CONV_DIM), BF16)]
    if short:
        grid, tok, sem, carried = _token_layout(n_tok, n_seq, tm, True, shared_pre)
        pre_spec, tail_spec, tail_shape = carried(CONV_DIM, 4)
        in_specs = [pl.BlockSpec((tm, d), tok), pre_spec] + param_specs
        out_specs = [pl.BlockSpec((tm, D_INNER), tok), pl.BlockSpec((tm, CONV_DIM), tok),
                     pl.BlockSpec((tm, LANES), tok), tail_spec]
        out_shape += [jax.ShapeDtypeStruct((n_tok, LANES), F32), tail_shape]
    else:
        grid, tok, sem, carried = _token_layout(n_tok, n_seq, tm, False, shared_pre)
        pre_spec, tail_spec, tail_shape = carried(CONV_DIM, 4)
        tiles = n_tok // n_seq // tm
        tok_t = lambda b, j: (0, b * tiles + j)
        ctm = max(tm, CHUNK)
        da_rows = ctm // CHUNK * SUBLANES
        n_rows = n_tok // tm * ctm
        in_specs = [pl.BlockSpec((tm, d), tok), pre_spec] + param_specs
        out_specs = [pl.BlockSpec((tm, D_INNER), tok), pl.BlockSpec((tm, CONV_DIM), tok),
                     pl.BlockSpec((ctm, LANES), tok), pl.BlockSpec((LANES, ctm), tok_t),
                     pl.BlockSpec((ctm, 2 * LANES), tok), pl.BlockSpec((da_rows, D_INNER), tok), tail_spec]
        out_shape += [jax.ShapeDtypeStruct((n_rows, LANES), F32), jax.ShapeDtypeStruct((LANES, n_rows), F32),
                      jax.ShapeDtypeStruct((n_rows, 2 * LANES), BF16),
                      jax.ShapeDtypeStruct((n_rows // CHUNK * SUBLANES, D_INNER), F32), tail_shape]
    return pl.pallas_call(
        functools.partial(_ssm_in_kernel, short=short),
        grid=grid,
        in_specs=in_specs,
        out_specs=out_specs,
        out_shape=out_shape,
        scratch_shapes=[pltpu.VMEM((tm + SUBLANES, CONV_DIM), F32)],
        compiler_params=_params(sem),
        name="ssm_in_short" if short else "ssm_in_long",
    )(x, pre, *params)


def _ssd_long(x, z, xbc, cum, cspt, we, da, s0, w, layer, *, n_seq, tm, shared_s0):
    expand, dskip, nw, w_out = w["expand"], w["dskip"], w["ssm_nw"], w["ssm_out"]
    n_tok, d = x.shape
    tiles = n_tok // n_seq // tm
    tok = lambda b, c: (b * tiles + c, 0)
    tok_t = lambda b, c: (0, b * tiles + c)
    s_spec = pl.BlockSpec((D_INNER, D_STATE), lambda b, c: (b, 0))
    s0_spec = pl.BlockSpec((D_INNER, D_STATE), lambda b, c: (0, 0)) if shared_s0 else s_spec
    return pl.pallas_call(
        _ssd_long_kernel,
        grid=(n_seq, tiles),
        in_specs=[pl.BlockSpec((tm, d), tok), pl.BlockSpec((tm, D_INNER), tok),
                  pl.BlockSpec((tm, CONV_DIM), tok), pl.BlockSpec((tm, LANES), tok),
                  pl.BlockSpec((LANES, tm), tok_t), pl.BlockSpec((tm, 2 * LANES), tok),
                  pl.BlockSpec((tm // CHUNK * SUBLANES, D_INNER), tok), s0_spec,
                  _const_spec(expand.shape), _const_spec(dskip.shape), _const_spec(nw.shape), _layer_spec(w_out, layer)],
        out_specs=[pl.BlockSpec((tm, d), tok), s_spec],
        out_shape=[jax.ShapeDtypeStruct((n_tok, d), F32), jax.ShapeDtypeStruct((n_seq * D_INNER, D_STATE), F32)],
        scratch_shapes=[pltpu.VMEM((D_STATE, D_INNER), F32)],
        compiler_params=_params(("arbitrary", "arbitrary")),
        name="ssd_long",
    )(x, z, xbc, cum, cspt, we, da, s0, expand, dskip, nw, w_out)


def _ssd_short(x, z, xbc, dt, s0, w, layer, *, tm):
    alog, dskip, nw, w_out = w["alog"], w["dskip"], w["ssm_nw"], w["ssm_out"]
    n_tok, d = x.shape
    n_seq = s0.shape[0]
    tok = lambda i: (i, 0)
    s_spec = pl.BlockSpec((tm // SHORT_LEN, D_INNER, D_STATE), lambda i: (i, 0, 0))
    return pl.pallas_call(
        _ssd_short_kernel,
        grid=(n_tok // tm,),
        in_specs=[pl.BlockSpec((tm, d), tok), pl.BlockSpec((tm, D_INNER), tok), pl.BlockSpec((tm, CONV_DIM), tok),
                  pl.BlockSpec((tm, LANES), tok), s_spec,
                  _const_spec(alog.shape), _const_spec(dskip.shape), _const_spec(nw.shape), _layer_spec(w_out, layer)],
        out_specs=[pl.BlockSpec((tm, d), tok), s_spec],
        out_shape=[jax.ShapeDtypeStruct((n_tok, d), F32), jax.ShapeDtypeStruct((n_seq, D_INNER, D_STATE), F32)],
        compiler_params=_params(("arbitrary",)),
        name="ssd_short",
    )(x, z, xbc, dt, s0, alog, dskip, nw, w_out)


def _trunk(x, pre_sc, pre_xbc, s0, pre_ffn, w, *, n_seq, short, tms, shared=False):
    kw = dict(n_seq=n_seq, short=short, shared_pre=shared)
    x1, t_sc = _sc_mixer(x, pre_sc, w, 0, tm=tms[0], **kw)
    x2, t_f0 = _conv_ffn(x1, pre_ffn[0], w, 0, tm=tms[1], final_norm=False, **kw)
    ssm = _ssm_in(x2, pre_xbc, w, 0, tm=tms[2], **kw)
    if short:
        z, xbc, dt, t_xbc = ssm
        x3, s_new = _ssd_short(x2, z, xbc, dt, s0, w, 0, tm=tms[3])
    else:
        z, xbc, cum, cspt, we, da, t_xbc = ssm
        pad_front = cum.shape[0] - x2.shape[0]
        padf = lambda t: jnp.pad(t, ((pad_front, 0), (0, 0))) if pad_front else t
        x3, s_new = _ssd_long(padf(x2), padf(z), padf(xbc), cum, cspt, we, da, s0, w, 0, n_seq=n_seq,
                              tm=max(tms[3], CHUNK), shared_s0=shared)
        x3 = x3[pad_front:]
    y, t_f1 = _conv_ffn(x3, pre_ffn[1], w, 1, tm=tms[4], final_norm=True, **kw)
    return y, t_sc, t_xbc, s_new, (t_f0, t_f1)


def _right_align(cache):
    s, wm1, c = cache.shape
    return jnp.pad(cache, ((0, 0), (SUBLANES - wm1, 0), (0, 0))).reshape(s * SUBLANES, c)


def kernel(x_prompt, x_sample, cache_sc, cache_ssm_conv, state_ssm, cache_ffn_conv, meta_tokens, norm_mix, norm_ffn, norm_final, sc_w_in, sc_conv_w, sc_w_out, ssm_w_in, ssm_conv_w, ssm_conv_b, ssm_dt_bias, ssm_a_log, ssm_d, ssm_norm_w, ssm_w_out, ffn_w_up, ffn_w_gate, ffn_conv_w, ffn_w_down):
    b, seq, d = x_prompt.shape
    n_dec, dec_len, _ = x_sample.shape
    d_ff = ffn_w_up.shape[2]
    assert dec_len == SHORT_LEN and seq % CHUNK == 0 and N_META % SUBLANES == 0 and N_META <= CHUNK

    pad_heads = lambda v: jnp.pad(v.reshape(1, -1).astype(F32), ((0, 0), (0, LANES - N_HEADS)))
    head_of_lane = jnp.arange(D_INNER, dtype=jnp.int32)[None, :] // HEAD_DIM
    w = dict(
        norm_mix=norm_mix.reshape(-1, 1, d), norm_ffn=norm_ffn.reshape(-1, 1, d), nfin=norm_final.reshape(1, d),
        sc_in=sc_w_in.astype(BF16), sc_cw=sc_conv_w, sc_out=sc_w_out.astype(BF16),
        ssm_in=ssm_w_in[:, :, :D_INNER + CONV_DIM].astype(BF16),
        w_dt=jnp.pad(ssm_w_in[0][:, D_INNER + CONV_DIM:], ((0, 0), (0, LANES - N_HEADS))).astype(BF16),
        ssm_cw=ssm_conv_w, ssm_cb=ssm_conv_b[0:1], dtb=pad_heads(ssm_dt_bias[0]), alog=pad_heads(ssm_a_log[0]),
        dskip=jnp.repeat(ssm_d[0], HEAD_DIM).reshape(1, -1), ssm_nw=ssm_norm_w[0:1], ssm_out=ssm_w_out.astype(BF16),
        expand=(jnp.arange(LANES, dtype=jnp.int32)[:, None] == head_of_lane).astype(BF16),
        up=ffn_w_up.astype(BF16), gate=ffn_w_gate.astype(BF16), ffn_cw=ffn_conv_w, down=ffn_w_down.astype(BF16),
    )

    zeros8 = lambda c: jnp.zeros((SUBLANES, c), F32)
    _, m_sc, m_xbc, m_state, m_ffn = _trunk(
        meta_tokens.astype(F32), zeros8(d), zeros8(CONV_DIM), jnp.zeros((D_INNER, D_STATE), F32),
        (zeros8(d_ff), zeros8(d_ff)), w, n_seq=1, short=False, tms=(N_META,) * 5)

    yp, p_sc, p_xbc, p_state, p_ffn = _trunk(
        x_prompt.reshape(b * seq, d), m_sc, m_xbc, m_state, m_ffn, w,
        n_seq=b, short=False, shared=True, tms=(512, 512, 512, 512, 512))
    tail = lambda t, k: t.reshape(b, SUBLANES, -1)[:, SUBLANES - k:]
    out_prompt = (
        yp.reshape(b, seq, d),
        tail(p_sc, 2)[None], tail(p_xbc, 3)[None],
        p_state.reshape(1, b, N_HEADS, HEAD_DIM, D_STATE),
        jnp.stack([tail(p_ffn[0], 2), tail(p_ffn[1], 2)]),
    )

    ys, s_sc, s_xbc, s_state, s_ffn = _trunk(
        x_sample.reshape(n_dec * dec_len, d), _right_align(cache_sc[0]), _right_align(cache_ssm_conv[0]),
        state_ssm[0].reshape(n_dec, D_INNER, D_STATE),
        (_right_align(cache_ffn_conv[0]), _right_align(cache_ffn_conv[1])), w,
        n_seq=n_dec, short=True, tms=(256, 256, 256, 64, 256))
    out_sample = (
        ys.reshape(n_dec, dec_len, d),
        s_sc[None], s_xbc[None],
        s_state.reshape(1, n_dec, N_HEADS, HEAD_DIM, D_STATE),
        jnp.stack([s_ffn[0], s_ffn[1]]),
    )
    return (out_prompt[0], out_sample[0]) + out_prompt[1:] + out_sample[1:]
```

```python
import functools

import jax
import jax.numpy as jnp
from jax import lax
from jax.experimental import pallas as pl
from jax.experimental.pallas import tpu as pltpu

F32 = jnp.float32
BF16 = jnp.bfloat16

EPS = 1e-5
N_META = 16
HEAD_DIM = 64
N_HEADS = 32
N_GROUPS = 4
HEADS_PER_GROUP = N_HEADS // N_GROUPS
D_STATE = 128
D_INNER = N_HEADS * HEAD_DIM
GROUP_W = D_INNER // N_GROUPS
BC_W = N_GROUPS * D_STATE
CONV_DIM = D_INNER + 2 * BC_W
CHUNK = 128
SHORT_LEN = 8
SUBLANES = 8
LANES = 128
STRIP = 64
COL_BLOCK = 512
FFN_COL_BLOCK = 768
NEG_BIG = -1e30
NEG_LOG2E = -1.4426950408889634
VMEM_LIMIT = 56 * 1024 * 1024


def _rmsnorm(x, w):
    r = lax.rsqrt(jnp.mean(x * x, axis=-1, keepdims=True) + EPS)
    return x * r * w


def _silu(x):
    return x / (1.0 + jnp.exp2(x * NEG_LOG2E))


def _softplus(x):
    return jnp.maximum(x, 0.0) + jnp.log1p(jnp.exp(-jnp.abs(x)))


def _dot(a, b):
    return jnp.dot(a, b, preferred_element_type=F32)


def _dot_nt(a, b):
    return lax.dot_general(a, b, (((1,), (1,)), ((), ())), preferred_element_type=F32)


def _dot_exact_lhs(m_bf16, x):
    hi = x.astype(BF16)
    r1 = x - hi.astype(F32)
    mid = r1.astype(BF16)
    lo = (r1 - mid.astype(F32)).astype(BF16)
    return _dot(m_bf16, hi) + _dot(m_bf16, mid) + _dot(m_bf16, lo)


def _conv_strips(buf_ref, pre_ref, w_ref, width, tm, col_lo, col_hi, short, emit):
    rows = min(STRIP, tm)
    row_in_seq = lax.broadcasted_iota(jnp.int32, (rows, LANES), 0) & (SHORT_LEN - 1)
    for c0 in range(col_lo, col_hi, LANES):
        cols = slice(c0, c0 + LANES)
        taps = [w_ref[k:k + 1, cols] for k in range(width)]
        for r0 in range(0, tm, rows):
            if short:
                xv = buf_ref[SUBLANES + r0:SUBLANES + r0 + rows, cols]
                pv = pre_ref[r0:r0 + rows, cols]
                y = xv * taps[width - 1]
                for k in range(1, width):
                    xk = jnp.where(row_in_seq < k, pltpu.roll(pv, rows - (SHORT_LEN - k), axis=0),
                                   pltpu.roll(xv, k, axis=0))
                    y = y + xk * taps[width - 1 - k]
            else:
                ext = buf_ref[r0:r0 + rows + SUBLANES, cols]
                y = ext[SUBLANES:] * taps[width - 1]
                for k in range(1, width):
                    y = y + pltpu.roll(ext, k, axis=0)[SUBLANES:] * taps[width - 1 - k]
            emit(r0, cols, y)


def _col_blocks(n_cols, block):
    return [(lo, min(lo + block, n_cols)) for lo in range(0, n_cols, block)]


def _pipelined_conv(buf_ref, pre_ref, tail_ref, cw_ref, width, tm, blocks, short, dots, emit):
    if not short:
        @pl.when(pl.program_id(1) == 0)
        def _():
            buf_ref[0:SUBLANES, :] = pre_ref[...]

    dots(*blocks[0])
    for i, (lo, hi) in enumerate(blocks):
        if i + 1 < len(blocks):
            dots(*blocks[i + 1])
        _conv_strips(buf_ref, pre_ref, cw_ref, width, tm, lo, hi, short, emit)

    if short:
        for s in range(tm // SHORT_LEN):
            last = SUBLANES + (s + 1) * SHORT_LEN
            tail_ref[s] = buf_ref[last - (width - 1):last, :]
    else:
        tail_ref[...] = buf_ref[tm:, :]
        buf_ref[0:SUBLANES, :] = buf_ref[tm:, :]


def _sc_mixer_kernel(x_ref, pre_ref, nw_ref, win_ref, cw_ref, wout_ref, o_ref, tail_ref,
                     buf_ref, b_ref, g_ref, *, short):
    tm, d = x_ref.shape
    h = _rmsnorm(x_ref[...], nw_ref[...]).astype(BF16)

    def dots(lo, hi):
        buf_ref[SUBLANES:, lo:hi] = _dot(h, win_ref[:, d + lo:d + hi]) * _dot(h, win_ref[:, 2 * d + lo:2 * d + hi])
        b_ref[:, lo:hi] = _dot(h, win_ref[:, lo:hi])

    def emit(r0, cols, y):
        rows = slice(r0, r0 + y.shape[0])
        g_ref[rows, cols] = (b_ref[rows, cols] * y).astype(BF16)

    _pipelined_conv(buf_ref, pre_ref, tail_ref, cw_ref, 3, tm, _col_blocks(d, COL_BLOCK), short, dots, emit)
    o_ref[...] = x_ref[...] + _dot(g_ref[...], wout_ref[...])


def _conv_ffn_kernel(x_ref, pre_ref, nw_ref, wup_ref, wgate_ref, cw_ref, wdown_ref, nf_ref,
                     o_ref, tail_ref, buf_ref, g_ref, a_ref, *, short, final_norm):
    tm = x_ref.shape[0]
    f = wup_ref.shape[1]
    h = _rmsnorm(x_ref[...], nw_ref[...]).astype(BF16)

    def dots(lo, hi):
        buf_ref[SUBLANES:, lo:hi] = _dot(h, wup_ref[:, lo:hi])
        g_ref[:, lo:hi] = _dot(h, wgate_ref[:, lo:hi])

    def emit(r0, cols, y):
        rows = slice(r0, r0 + y.shape[0])
        a_ref[rows, cols] = (_silu(y) * g_ref[rows, cols]).astype(BF16)

    _pipelined_conv(buf_ref, pre_ref, tail_ref, cw_ref, 3, tm, _col_blocks(f, FFN_COL_BLOCK), short, dots, emit)
    y = x_ref[...] + _dot(a_ref[...], wdown_ref[...])
    if final_norm:
        y = _rmsnorm(y, nf_ref[...])
    o_ref[...] = y


def _head_cols(v, h0, lo):
    return jnp.where(lo, v[:, h0:h0 + 1], v[:, h0 + 1:h0 + 2])


def _ssm_in_kernel(x_ref, pre_ref, nw_ref, w_ref, wdt_ref, cw_ref, cb_ref, dtb_ref, alog_ref,
                   z_ref, xbc_ref, *rest, short):
    tm = x_ref.shape[0]
    h = _rmsnorm(x_ref[...], nw_ref[...]).astype(BF16)
    if short:
        dt_ref, tail_ref, buf_ref = rest
    else:
        cum_ref, cspt_ref, we_ref, da_ref, tail_ref, buf_ref = rest

        @pl.when(pl.program_id(1) == 0)
        def _():
            buf_ref[0:SUBLANES, :] = pre_ref[...]

    def dot_xbc(i):
        cols = slice(i * COL_BLOCK, (i + 1) * COL_BLOCK)
        buf_ref[SUBLANES:, cols] = _dot(h, w_ref[:, D_INNER + i * COL_BLOCK:D_INNER + (i + 1) * COL_BLOCK])

    def dot_z(i):
        cols = slice(i * COL_BLOCK, (i + 1) * COL_BLOCK)
        z_ref[:, cols] = _dot(h, w_ref[:, cols]).astype(BF16)

    def emit(r0, cols, y):
        xbc_ref[r0:r0 + y.shape[0], cols] = _silu(y + cb_ref[:, cols]).astype(BF16)

    def conv_block(i):
        _conv_strips(buf_ref, pre_ref, cw_ref, 4, tm, i * COL_BLOCK, (i + 1) * COL_BLOCK, short, emit)

    n_x, n_z = CONV_DIM // COL_BLOCK, D_INNER // COL_BLOCK
    dot_xbc(0)
    for i in range(n_x):
        if i + 1 < n_x:
            dot_xbc(i + 1)
        if i < n_z:
            dot_z(i)
        conv_block(i)
    for i in range(n_x, n_z):
        dot_z(i)
    dt = _softplus(_dot(h, wdt_ref[...]) + dtb_ref[...])

    if short:
        for s in range(tm // SHORT_LEN):
            tail_ref[s] = buf_ref[SUBLANES + (s + 1) * SHORT_LEN - 3:SUBLANES + (s + 1) * SHORT_LEN, :]
        dt_ref[...] = dt
    else:
        tail_ref[...] = buf_ref[tm:, :]
        buf_ref[0:SUBLANES, :] = buf_ref[tm:, :]
        _decay_terms(dt, alog_ref, cum_ref, cspt_ref, we_ref, da_ref)


def _decay_terms(dt, alog_ref, cum_ref, cspt_ref, we_ref, da_ref):
    tm = dt.shape[0]
    a = -jnp.exp(alog_ref[...])
    li = lax.broadcasted_iota(jnp.int32, (CHUNK, CHUNK), 0)
    si = lax.broadcasted_iota(jnp.int32, (CHUNK, CHUNK), 1)
    tri = (si <= li).astype(BF16)
    lo8 = lax.broadcasted_iota(jnp.int32, (SUBLANES, LANES), 1) < HEAD_DIM
    pad = CHUNK - tm if tm < CHUNK else 0
    for c in range(max(tm // CHUNK, 1)):
        if pad:
            dt_c = jnp.concatenate([jnp.zeros((pad, LANES), F32), dt], axis=0)
        else:
            dt_c = dt[c * CHUNK:(c + 1) * CHUNK]
        rows = slice(c * CHUNK, (c + 1) * CHUNK)
        cum = _dot_exact_lhs(tri, dt_c * a)
        end = cum[CHUNK - 1:CHUNK, :]
        cum_ref[rows, :] = cum
        cspt_ref[:, rows] = (cum - jnp.log(dt_c)).T
        we_ref[rows, 0:LANES] = (jnp.exp(end - cum) * dt_c).astype(BF16)
        we_ref[rows, LANES:2 * LANES] = jnp.exp(cum).astype(BF16)
        e_end = jnp.broadcast_to(jnp.exp(end), (SUBLANES, LANES))
        for pr in range(N_HEADS // 2):
            da_ref[c * SUBLANES:(c + 1) * SUBLANES, pr * LANES:(pr + 1) * LANES] = _head_cols(e_end, 2 * pr, lo8)


def _ssd_long_kernel(x_ref, z_ref, xbc_ref, cum_ref, cspt_ref, we_ref, da_ref, s0_ref, e_ref,
                     dskip_ref, nw_ref, wout_ref, o_ref, sout_ref, st_ref):
    q = CHUNK

    @pl.when(pl.program_id(1) == 0)
    def _():
        st_ref[...] = s0_ref[...].T

    li = lax.broadcasted_iota(jnp.int32, (q, q), 0)
    si = lax.broadcasted_iota(jnp.int32, (q, q), 1)
    causal = si <= li
    lane = lax.broadcasted_iota(jnp.int32, (q, LANES), 1)
    m_lo = (lane < HEAD_DIM).astype(BF16)
    m_hi = (lane >= HEAD_DIM).astype(BF16)

    normed_chunks = []
    for c in range(x_ref.shape[0] // q):
        rows = slice(c * q, (c + 1) * q)
        cum = cum_ref[rows, :]
        cspt = cspt_ref[:, rows]
        w_end = we_ref[rows, 0:LANES]
        e_cum = we_ref[rows, LANES:2 * LANES]
        normed = []
        for g in range(N_GROUPS):
            gl = slice(g * GROUP_W, (g + 1) * GROUP_W)
            bg = xbc_ref[rows, D_INNER + g * D_STATE:D_INNER + (g + 1) * D_STATE]
            cg = xbc_ref[rows, D_INNER + BC_W + g * D_STATE:D_INNER + BC_W + (g + 1) * D_STATE]
            xs_b = xbc_ref[rows, gl]
            xs = xs_b.astype(F32)
            st_g = st_ref[:, gl]
            cb = _dot_nt(cg, bg)
            y_off = _dot(cg, st_g.astype(BF16))
            w_exp = _dot(w_end, e_ref[:, gl])
            e_exp = _dot(e_cum, e_ref[:, gl])
            xw = xs_b * w_exp.astype(BF16)
            bg_t = bg.astype(F32).T.astype(BF16)
            st_ref[:, gl] = st_g * da_ref[c * SUBLANES:c * SUBLANES + 1, gl] + _dot(bg_t, xw)

            ys = []
            for pr in range(HEADS_PER_GROUP // 2):
                h0 = g * HEADS_PER_GROUP + 2 * pr
                ms = []
                for h in (h0, h0 + 1):
                    seg = cum[:, h:h + 1] - cspt[h:h + 1, :]
                    ms.append((cb * jnp.exp(jnp.where(causal, seg, NEG_BIG))).astype(BF16))
                xp = xs_b[:, pr * LANES:(pr + 1) * LANES]
                rhs = jnp.concatenate([xp * m_lo, xp * m_hi], axis=0)
                ys.append(_dot(jnp.concatenate(ms, axis=1), rhs))
            y = jnp.concatenate(ys, axis=1) + y_off * e_exp + dskip_ref[:, gl] * xs
            y = y * _silu(z_ref[rows, gl].astype(F32))
            y = y * lax.rsqrt(jnp.mean(y * y, axis=-1, keepdims=True) + EPS)
            normed.append((y * nw_ref[:, gl]).astype(BF16))
        normed_chunks.append(jnp.concatenate(normed, axis=1))

    o_ref[...] = x_ref[...] + _dot(jnp.concatenate(normed_chunks, axis=0), wout_ref[...])

    @pl.when(pl.program_id(1) == pl.num_programs(1) - 1)
    def _():
        sout_ref[...] = st_ref[...].T


def _ssd_short_kernel(x_ref, z_ref, xbc_ref, dt_ref, s0_ref, alog_ref, dskip_ref, nw_ref, wout_ref,
                      o_ref, sout_ref):
    q = x_ref.shape[0]
    n_seq = q // SHORT_LEN
    xbc = xbc_ref[...]
    xs = xbc[:, :D_INNER].astype(F32)
    bm = xbc[:, D_INNER:D_INNER + BC_W]
    cm = xbc[:, D_INNER + BC_W:]
    dt = dt_ref[...]
    a = -jnp.exp(alog_ref[...])
    dta = dt * a

    li = lax.broadcasted_iota(jnp.int32, (q, q), 0)
    si = lax.broadcasted_iota(jnp.int32, (q, q), 1)
    same_seq = (li // SHORT_LEN) == (si // SHORT_LEN)
    causal = same_seq & (si <= li)
    cum = _dot_exact_lhs(causal.astype(BF16), dta)
    cum_t = cum.T
    cum_end = _dot_exact_lhs(same_seq.astype(BF16), dta)
    w_end = jnp.exp(cum_end - cum) * dt
    ecum = jnp.exp(cum)
    e_end = jnp.exp(cum_end)

    lane = lax.broadcasted_iota(jnp.int32, (q, LANES), 1)
    lo = lane < HEAD_DIM
    row_seq = lax.broadcasted_iota(jnp.int32, (q, LANES), 0) // SHORT_LEN

    y_off = []
    for g in range(N_GROUPS):
        cg = cm[:, g * D_STATE:(g + 1) * D_STATE].astype(F32)
        acc = jnp.zeros((q, GROUP_W), F32)
        for s in range(n_seq):
            cmask = jnp.where(row_seq == s, cg, 0.0).astype(BF16)
            sg = s0_ref[s, g * GROUP_W:(g + 1) * GROUP_W, :].astype(BF16)
            acc = acc + _dot_nt(cmask, sg)
        y_off.append(acc)

    y_pieces = []
    xw_pieces = []
    da_pieces = []
    for g in range(N_GROUPS):
        bg = bm[:, g * D_STATE:(g + 1) * D_STATE]
        cg = cm[:, g * D_STATE:(g + 1) * D_STATE]
        cb = _dot_nt(cg, bg)
        for pr in range(HEADS_PER_GROUP // 2):
            h0 = g * HEADS_PER_GROUP + 2 * pr
            l0 = h0 * HEAD_DIM
            lhs = []
            for h in (h0, h0 + 1):
                seg = cum[:, h:h + 1] - cum_t[h:h + 1, :]
                lhs.append((cb * jnp.exp(jnp.where(causal, seg, NEG_BIG))).astype(BF16))
            x_pair = xs[:, l0:l0 + LANES]
            xdt = x_pair * _head_cols(dt, h0, lo)
            rhs = jnp.concatenate([jnp.where(lo, xdt, 0.0), jnp.where(lo, 0.0, xdt)], axis=0).astype(BF16)
            y_pair = _dot(jnp.concatenate(lhs, axis=1), rhs)
            y_pair = y_pair + y_off[g][:, 2 * pr * HEAD_DIM:2 * pr * HEAD_DIM + LANES] * _head_cols(ecum, h0, lo)
            y_pieces.append(y_pair)
            xw_pieces.append(x_pair * _head_cols(w_end, h0, lo))
            da_pieces.append(_head_cols(e_end, h0, lo))

    pad = jnp.zeros((LANES - q, D_INNER), F32)
    xw_t = jnp.concatenate([jnp.concatenate(xw_pieces, axis=1), pad], axis=0).T.astype(BF16)
    da_t = jnp.concatenate([jnp.concatenate(da_pieces, axis=1), pad], axis=0).T
    row_seq_k = lax.broadcasted_iota(jnp.int32, (LANES, D_STATE), 0) // SHORT_LEN
    bm_k = jnp.concatenate([bm.astype(F32), jnp.zeros((LANES - q, BC_W), F32)], axis=0)
    for s in range(n_seq):
        col = s * SHORT_LEN + SHORT_LEN - 1
        for g in range(N_GROUPS):
            rows = slice(g * GROUP_W, (g + 1) * GROUP_W)
            bmask = jnp.where(row_seq_k == s, bm_k[:, g * D_STATE:(g + 1) * D_STATE], 0.0).astype(BF16)
            upd = _dot(xw_t[rows, :], bmask)
            sout_ref[s, rows, :] = s0_ref[s, rows, :] * da_t[rows, col:col + 1] + upd

    y = jnp.concatenate(y_pieces, axis=1) + dskip_ref[...] * xs
    y = y * _silu(z_ref[...].astype(F32))
    normed = []
    for g in range(N_GROUPS):
        yg = y[:, g * GROUP_W:(g + 1) * GROUP_W]
        normed.append(yg * lax.rsqrt(jnp.mean(yg * yg, axis=-1, keepdims=True) + EPS))
    yn = (jnp.concatenate(normed, axis=1) * nw_ref[...]).astype(BF16)
    o_ref[...] = x_ref[...] + _dot(yn, wout_ref[...])


def _const_spec(shape):
    return pl.BlockSpec(shape, lambda *_: (0,) * len(shape), pipeline_mode=pl.Buffered(1))


def _layer_spec(arr, layer):
    return pl.BlockSpec((None,) + arr.shape[1:], lambda *_: (layer, 0, 0), pipeline_mode=pl.Buffered(1))


def _token_layout(n_tok, n_seq, tm, short, shared_pre):
    if short:
        grid = (n_tok // tm,)
        tok = lambda i: (i, 0)

        def carried(c, width):
            tail_spec = pl.BlockSpec((tm // SHORT_LEN, width - 1, c), lambda i: (i, 0, 0))
            tail_shape = jax.ShapeDtypeStruct((n_tok // SHORT_LEN, width - 1, c), F32)
            return pl.BlockSpec((tm, c), tok), tail_spec, tail_shape

        return grid, tok, ("arbitrary",), carried
    tiles = n_tok // n_seq // tm
    grid = (n_seq, tiles)
    tok = lambda b, j: (b * tiles + j, 0)
    per_seq = lambda b, j: (b, 0)
    pre_map = (lambda b, j: (0, 0)) if shared_pre else per_seq

    def carried(c, width):
        return (pl.BlockSpec((SUBLANES, c), pre_map), pl.BlockSpec((SUBLANES, c), per_seq),
                jax.ShapeDtypeStruct((n_seq * SUBLANES, c), F32))

    return grid, tok, ("arbitrary", "arbitrary"), carried


def _params(sem):
    return pltpu.CompilerParams(dimension_semantics=sem, vmem_limit_bytes=VMEM_LIMIT)


def _sc_mixer(x, pre, w, layer, *, n_seq, tm, short, shared_pre):
    n_tok, d = x.shape
    grid, tok, sem, carried = _token_layout(n_tok, n_seq, tm, short, shared_pre)
    pre_spec, tail_spec, tail_shape = carried(d, 3)
    return pl.pallas_call(
        functools.partial(_sc_mixer_kernel, short=short),
        grid=grid,
        in_specs=[pl.BlockSpec((tm, d), tok), pre_spec, _layer_spec(w["norm_mix"], 2 * layer),
                  _layer_spec(w["sc_in"], layer), _layer_spec(w["sc_cw"], layer), _layer_spec(w["sc_out"], layer)],
        out_specs=[pl.BlockSpec((tm, d), tok), tail_spec],
        out_shape=[jax.ShapeDtypeStruct((n_tok, d), F32), tail_shape],
        scratch_shapes=[pltpu.VMEM((tm + SUBLANES, d), F32), pltpu.VMEM((tm, d), F32), pltpu.VMEM((tm, d), BF16)],
        compiler_params=_params(sem),
        name="sc_mixer_short" if short else "sc_mixer_long",
    )(x, pre, w["norm_mix"], w["sc_in"], w["sc_cw"], w["sc_out"])


def _conv_ffn(x, pre, w, layer, *, n_seq, tm, short, shared_pre, final_norm):
    n_tok, d = x.shape
    f = w["up"].shape[2]
    grid, tok, sem, carried = _token_layout(n_tok, n_seq, tm, short, shared_pre)
    pre_spec, tail_spec, tail_shape = carried(f, 3)
    return pl.pallas_call(
        functools.partial(_conv_ffn_kernel, short=short, final_norm=final_norm),
        grid=grid,
        in_specs=[pl.BlockSpec((tm, d), tok), pre_spec, _layer_spec(w["norm_ffn"], layer),
                  _layer_spec(w["up"], layer), _layer_spec(w["gate"], layer), _layer_spec(w["ffn_cw"], layer),
                  _layer_spec(w["down"], layer), _const_spec(w["nfin"].shape)],
        out_specs=[pl.BlockSpec((tm, d), tok), tail_spec],
        out_shape=[jax.ShapeDtypeStruct((n_tok, d), F32), tail_shape],
        scratch_shapes=[pltpu.VMEM((tm + SUBLANES, f), F32), pltpu.VMEM((tm, f), F32), pltpu.VMEM((tm, f), BF16)],
        compiler_params=_params(sem),
        name="conv_ffn_short" if short else "conv_ffn_long",
    )(x, pre, w["norm_ffn"], w["up"], w["gate"], w["ffn_cw"], w["down"], w["nfin"])


def _ssm_in(x, pre, w, layer, *, n_seq, tm, short, shared_pre):
    n_tok, d = x.shape
    params = [w["norm_mix"], w["ssm_in"], w["w_dt"], w["ssm_cw"], w["ssm_cb"], w["dtb"], w["alog"]]
    param_specs = [_layer_spec(w["norm_mix"], 2 * layer + 1), _layer_spec(w["ssm_in"], layer),
                   _const_spec(w["w_dt"].shape), _layer_spec(w["ssm_cw"], layer),
                   _const_spec(w["ssm_cb"].shape), _const_spec(w["dtb"].shape), _const_spec(w["alog"].shape)]
    out_shape = [jax.ShapeDtypeStruct((n_tok, D_INNER), BF16), jax.ShapeDtypeStruct((n_tok, CONV_DIM), BF16)]
    if short:
        grid, tok, sem, carried = _token_layout(n_tok, n_seq, tm, True, shared_pre)
        pre_spec, tail_spec, tail_shape = carried(CONV_DIM, 4)
        in_specs = [pl.BlockSpec((tm, d), tok), pre_spec] + param_specs
        out_specs = [pl.BlockSpec((tm, D_INNER), tok), pl.BlockSpec((tm, CONV_DIM), tok),
                     pl.BlockSpec((tm, LANES), tok), tail_spec]
        out_shape += [jax.ShapeDtypeStruct((n_tok, LANES), F32), tail_shape]
    else:
        grid, tok, sem, carried = _token_layout(n_tok, n_seq, tm, False, shared_pre)
        pre_spec, tail_spec, tail_shape = carried(CONV_DIM, 4)
        tiles = n_tok // n_seq // tm
        tok_t = lambda b, j: (0, b * tiles + j)
        ctm = max(tm, CHUNK)
        da_rows = ctm // CHUNK * SUBLANES
        n_rows = n_tok // tm * ctm
        in_specs = [pl.BlockSpec((tm, d), tok), pre_spec] + param_specs
        out_specs = [pl.BlockSpec((tm, D_INNER), tok), pl.BlockSpec((tm, CONV_DIM), tok),
                     pl.BlockSpec((ctm, LANES), tok), pl.BlockSpec((LANES, ctm), tok_t),
                     pl.BlockSpec((ctm, 2 * LANES), tok), pl.BlockSpec((da_rows, D_INNER), tok), tail_spec]
        out_shape += [jax.ShapeDtypeStruct((n_rows, LANES), F32), jax.ShapeDtypeStruct((LANES, n_rows), F32),
                      jax.ShapeDtypeStruct((n_rows, 2 * LANES), BF16),
                      jax.ShapeDtypeStruct((n_rows // CHUNK * SUBLANES, D_INNER), F32), tail_shape]
    return pl.pallas_call(
        functools.partial(_ssm_in_kernel, short=short),
        grid=grid,
        in_specs=in_specs,
        out_specs=out_specs,
        out_shape=out_shape,
        scratch_shapes=[pltpu.VMEM((tm + SUBLANES, CONV_DIM), F32)],
        compiler_params=_params(sem),
        name="ssm_in_short" if short else "ssm_in_long",
    )(x, pre, *params)


def _ssd_long(x, z, xbc, cum, cspt, we, da, s0, w, layer, *, n_seq, tm, shared_s0):
    expand, dskip, nw, w_out = w["expand"], w["dskip"], w["ssm_nw"], w["ssm_out"]
    n_tok, d = x.shape
    tiles = n_tok // n_seq // tm
    tok = lambda b, c: (b * tiles + c, 0)
    tok_t = lambda b, c: (0, b * tiles + c)
    s_spec = pl.BlockSpec((D_INNER, D_STATE), lambda b, c: (b, 0))
    s0_spec = pl.BlockSpec((D_INNER, D_STATE), lambda b, c: (0, 0)) if shared_s0 else s_spec
    return pl.pallas_call(
        _ssd_long_kernel,
        grid=(n_seq, tiles),
        in_specs=[pl.BlockSpec((tm, d), tok), pl.BlockSpec((tm, D_INNER), tok),
                  pl.BlockSpec((tm, CONV_DIM), tok), pl.BlockSpec((tm, LANES), tok),
                  pl.BlockSpec((LANES, tm), tok_t), pl.BlockSpec((tm, 2 * LANES), tok),
                  pl.BlockSpec((tm // CHUNK * SUBLANES, D_INNER), tok), s0_spec,
                  _const_spec(expand.shape), _const_spec(dskip.shape), _const_spec(nw.shape), _layer_spec(w_out, layer)],
        out_specs=[pl.BlockSpec((tm, d), tok), s_spec],
        out_shape=[jax.ShapeDtypeStruct((n_tok, d), F32), jax.ShapeDtypeStruct((n_seq * D_INNER, D_STATE), F32)],
        scratch_shapes=[pltpu.VMEM((D_STATE, D_INNER), F32)],
        compiler_params=_params(("arbitrary", "arbitrary")),
        name="ssd_long",
    )(x, z, xbc, cum, cspt, we, da, s0, expand, dskip, nw, w_out)


def _ssd_short(x, z, xbc, dt, s0, w, layer, *, tm):
    alog, dskip, nw, w_out = w["alog"], w["dskip"], w["ssm_nw"], w["ssm_out"]
    n_tok, d = x.shape
    n_seq = s0.shape[0]
    tok = lambda i: (i, 0)
    s_spec = pl.BlockSpec((tm // SHORT_LEN, D_INNER, D_STATE), lambda i: (i, 0, 0))
    return pl.pallas_call(
        _ssd_short_kernel,
        grid=(n_tok // tm,),
        in_specs=[pl.BlockSpec((tm, d), tok), pl.BlockSpec((tm, D_INNER), tok), pl.BlockSpec((tm, CONV_DIM), tok),
                  pl.BlockSpec((tm, LANES), tok), s_spec,
                  _const_spec(alog.shape), _const_spec(dskip.shape), _const_spec(nw.shape), _layer_spec(w_out, layer)],
        out_specs=[pl.BlockSpec((tm, d), tok), s_spec],
        out_shape=[jax.ShapeDtypeStruct((n_tok, d), F32), jax.ShapeDtypeStruct((n_seq, D_INNER, D_STATE), F32)],
        compiler_params=_params(("arbitrary",)),
        name="ssd_short",
    )(x, z, xbc, dt, s0, alog, dskip, nw, w_out)


def _trunk(x, pre_sc, pre_xbc, s0, pre_ffn, w, *, n_seq, short, tms, shared=False):
    kw = dict(n_seq=n_seq, short=short, shared_pre=shared)
    x1, t_sc = _sc_mixer(x, pre_sc, w, 0, tm=tms[0], **kw)
    x2, t_f0 = _conv_ffn(x1, pre_ffn[0], w, 0, tm=tms[1], final_norm=False, **kw)
    ssm = _ssm_in(x2, pre_xbc, w, 0, tm=tms[2], **kw)
    if short:
        z, xbc, dt, t_xbc = ssm
        x3, s_new = _ssd_short(x2, z, xbc, dt, s0, w, 0, tm=tms[3])
    else:
        z, xbc, cum, cspt, we, da, t_xbc = ssm
        pad_front = cum.shape[0] - x2.shape[0]
        padf = lambda t: jnp.pad(t, ((pad_front, 0), (0, 0))) if pad_front else t
        x3, s_new = _ssd_long(padf(x2), padf(z), padf(xbc), cum, cspt, we, da, s0, w, 0, n_seq=n_seq,
                              tm=max(tms[3], CHUNK), shared_s0=shared)
        x3 = x3[pad_front:]
    y, t_f1 = _conv_ffn(x3, pre_ffn[1], w, 1, tm=tms[4], final_norm=True, **kw)
    return y, t_sc, t_xbc, s_new, (t_f0, t_f1)


def _right_align(cache):
    s, wm1, c = cache.shape
    return jnp.pad(cache, ((0, 0), (SUBLANES - wm1, 0), (0, 0))).reshape(s * SUBLANES, c)


def kernel(x_prompt, x_sample, cache_sc, cache_ssm_conv, state_ssm, cache_ffn_conv, meta_tokens, norm_mix, norm_ffn, norm_final, sc_w_in, sc_conv_w, sc_w_out, ssm_w_in, ssm_conv_w, ssm_conv_b, ssm_dt_bias, ssm_a_log, ssm_d, ssm_norm_w, ssm_w_out, ffn_w_up, ffn_w_gate, ffn_conv_w, ffn_w_down):
    b, seq, d = x_prompt.shape
    n_dec, dec_len, _ = x_sample.shape
    d_ff = ffn_w_up.shape[2]
    assert dec_len == SHORT_LEN and seq % CHUNK == 0 and N_META % SUBLANES == 0 and N_META <= CHUNK

    pad_heads = lambda v: jnp.pad(v.reshape(1, -1).astype(F32), ((0, 0), (0, LANES - N_HEADS)))
    head_of_lane = jnp.arange(D_INNER, dtype=jnp.int32)[None, :] // HEAD_DIM
    w = dict(
        norm_mix=norm_mix.reshape(-1, 1, d), norm_ffn=norm_ffn.reshape(-1, 1, d), nfin=norm_final.reshape(1, d),
        sc_in=sc_w_in.astype(BF16), sc_cw=sc_conv_w, sc_out=sc_w_out.astype(BF16),
        ssm_in=ssm_w_in.astype(BF16),
        w_dt=jnp.pad(ssm_w_in[0][:, D_INNER + CONV_DIM:], ((0, 0), (0, LANES - N_HEADS))).astype(BF16),
        ssm_cw=ssm_conv_w, ssm_cb=ssm_conv_b[0:1], dtb=pad_heads(ssm_dt_bias[0]), alog=pad_heads(ssm_a_log[0]),
        dskip=jnp.repeat(ssm_d[0], HEAD_DIM).reshape(1, -1), ssm_nw=ssm_norm_w[0:1], ssm_out=ssm_w_out.astype(BF16),
        expand=(jnp.arange(LANES, dtype=jnp.int32)[:, None] == head_of_lane).astype(BF16),
        up=ffn_w_up.astype(BF16), gate=ffn_w_gate.astype(BF16), ffn_cw=ffn_conv_w, down=ffn_w_down.astype(BF16),
    )

    zeros8 = lambda c: jnp.zeros((SUBLANES, c), F32)
    _, m_sc, m_xbc, m_state, m_ffn = _trunk(
        meta_tokens.astype(F32), zeros8(d), zeros8(CONV_DIM), jnp.zeros((D_INNER, D_STATE), F32),
        (zeros8(d_ff), zeros8(d_ff)), w, n_seq=1, short=False, tms=(N_META,) * 5)

    yp, p_sc, p_xbc, p_state, p_ffn = _trunk(
        x_prompt.reshape(b * seq, d), m_sc, m_xbc, m_state, m_ffn, w,
        n_seq=b, short=False, shared=True, tms=(512, 512, 512, 512, 512))
    tail = lambda t, k: t.reshape(b, SUBLANES, -1)[:, SUBLANES - k:]
    out_prompt = (
        yp.reshape(b, seq, d),
        tail(p_sc, 2)[None], tail(p_xbc, 3)[None],
        p_state.reshape(1, b, N_HEADS, HEAD_DIM, D_STATE),
        jnp.stack([tail(p_ffn[0], 2), tail(p_ffn[1], 2)]),
    )

    ys, s_sc, s_xbc, s_state, s_ffn = _trunk(
        x_sample.reshape(n_dec * dec_len, d), _right_align(cache_sc[0]), _right_align(cache_ssm_conv[0]),
        state_ssm[0].reshape(n_dec, D_INNER, D_STATE),
        (_right_align(cache_ffn_conv[0]), _right_align(cache_ffn_conv[1])), w,
        n_seq=n_dec, short=True, tms=(256, 256, 256, 64, 256))
    out_sample = (
        ys.reshape(n_dec, dec_len, d),
        s_sc[None], s_xbc[None],
        s_state.reshape(1, n_dec, N_HEADS, HEAD_DIM, D_STATE),
        jnp.stack([s_ffn[0], s_ffn[1]]),
    )
    return (out_prompt[0], out_sample[0]) + out_prompt[1:] + out_sample[1:]
```

```python
import functools

import jax
import jax.numpy as jnp
from jax import lax
from jax.experimental import pallas as pl
from jax.experimental.pallas import tpu as pltpu

F32 = jnp.float32
BF16 = jnp.bfloat16

EPS = 1e-5
N_META = 16
HEAD_DIM = 64
N_HEADS = 32
N_GROUPS = 4
HEADS_PER_GROUP = N_HEADS // N_GROUPS
D_STATE = 128
D_INNER = N_HEADS * HEAD_DIM
GROUP_W = D_INNER // N_GROUPS
BC_W = N_GROUPS * D_STATE
CONV_DIM = D_INNER + 2 * BC_W
CHUNK = 128
SHORT_LEN = 8
SUBLANES = 8
LANES = 128
STRIP = 64
COL_BLOCK = 512
FFN_COL_BLOCK = 768
NEG_BIG = -1e30
NEG_LOG2E = -1.4426950408889634
VMEM_LIMIT = 56 * 1024 * 1024


def _rmsnorm(x, w):
    r = lax.rsqrt(jnp.mean(x * x, axis=-1, keepdims=True) + EPS)
    return x * r * w


def _silu(x):
    return x / (1.0 + jnp.exp2(x * NEG_LOG2E))


def _softplus(x):
    return jnp.maximum(x, 0.0) + jnp.log1p(jnp.exp(-jnp.abs(x)))


def _dot(a, b):
    return jnp.dot(a, b, preferred_element_type=F32)


def _dot_nt(a, b):
    return lax.dot_general(a, b, (((1,), (1,)), ((), ())), preferred_element_type=F32)


def _dot_exact_lhs(m_bf16, x):
    hi = x.astype(BF16)
    r1 = x - hi.astype(F32)
    mid = r1.astype(BF16)
    lo = (r1 - mid.astype(F32)).astype(BF16)
    return _dot(m_bf16, hi) + _dot(m_bf16, mid) + _dot(m_bf16, lo)


def _conv_strips(buf_ref, pre_ref, w_ref, width, tm, col_lo, col_hi, short, emit):
    rows = min(STRIP, tm)
    row_in_seq = lax.broadcasted_iota(jnp.int32, (rows, LANES), 0) & (SHORT_LEN - 1)
    for c0 in range(col_lo, col_hi, LANES):
        cols = slice(c0, c0 + LANES)
        taps = [w_ref[k:k + 1, cols] for k in range(width)]
        for r0 in range(0, tm, rows):
            if short:
                xv = buf_ref[SUBLANES + r0:SUBLANES + r0 + rows, cols]
                pv = pre_ref[r0:r0 + rows, cols]
                y = xv * taps[width - 1]
                for k in range(1, width):
                    xk = jnp.where(row_in_seq < k, pltpu.roll(pv, rows - (SHORT_LEN - k), axis=0),
                                   pltpu.roll(xv, k, axis=0))
                    y = y + xk * taps[width - 1 - k]
            else:
                ext = buf_ref[r0:r0 + rows + SUBLANES, cols]
                y = ext[SUBLANES:] * taps[width - 1]
                for k in range(1, width):
                    y = y + pltpu.roll(ext, k, axis=0)[SUBLANES:] * taps[width - 1 - k]
            emit(r0, cols, y)


def _col_blocks(n_cols, block):
    return [(lo, min(lo + block, n_cols)) for lo in range(0, n_cols, block)]


def _pipelined_conv(buf_ref, pre_ref, tail_ref, cw_ref, width, tm, blocks, short, dots, emit):
    if not short:
        @pl.when(pl.program_id(1) == 0)
        def _():
            buf_ref[0:SUBLANES, :] = pre_ref[...]

    dots(*blocks[0])
    for i, (lo, hi) in enumerate(blocks):
        if i + 1 < len(blocks):
            dots(*blocks[i + 1])
        _conv_strips(buf_ref, pre_ref, cw_ref, width, tm, lo, hi, short, emit)

    if short:
        for s in range(tm // SHORT_LEN):
            last = SUBLANES + (s + 1) * SHORT_LEN
            tail_ref[s] = buf_ref[last - (width - 1):last, :]
    else:
        tail_ref[...] = buf_ref[tm:, :]
        buf_ref[0:SUBLANES, :] = buf_ref[tm:, :]


def _sc_mixer_kernel(x_ref, pre_ref, nw_ref, win_ref, cw_ref, wout_ref, o_ref, tail_ref,
                     buf_ref, b_ref, g_ref, *, short):
    tm, d = x_ref.shape
    h = _rmsnorm(x_ref[...], nw_ref[...]).astype(BF16)

    def dots(lo, hi):
        buf_ref[SUBLANES:, lo:hi] = _dot(h, win_ref[:, d + lo:d + hi]) * _dot(h, win_ref[:, 2 * d + lo:2 * d + hi])
        b_ref[:, lo:hi] = _dot(h, win_ref[:, lo:hi])

    def emit(r0, cols, y):
        rows = slice(r0, r0 + y.shape[0])
        g_ref[rows, cols] = (b_ref[rows, cols] * y).astype(BF16)

    _pipelined_conv(buf_ref, pre_ref, tail_ref, cw_ref, 3, tm, _col_blocks(d, COL_BLOCK), short, dots, emit)
    o_ref[...] = x_ref[...] + _dot(g_ref[...], wout_ref[...])


def _conv_ffn_kernel(x_ref, pre_ref, nw_ref, wup_ref, wgate_ref, cw_ref, wdown_ref, nf_ref,
                     o_ref, tail_ref, buf_ref, g_ref, a_ref, *, short, final_norm):
    tm = x_ref.shape[0]
    f = wup_ref.shape[1]
    h = _rmsnorm(x_ref[...], nw_ref[...]).astype(BF16)

    def dots(lo, hi):
        buf_ref[SUBLANES:, lo:hi] = _dot(h, wup_ref[:, lo:hi])
        g_ref[:, lo:hi] = _dot(h, wgate_ref[:, lo:hi])

    def emit(r0, cols, y):
        rows = slice(r0, r0 + y.shape[0])
        a_ref[rows, cols] = (_silu(y) * g_ref[rows, cols]).astype(BF16)

    _pipelined_conv(buf_ref, pre_ref, tail_ref, cw_ref, 3, tm, _col_blocks(f, FFN_COL_BLOCK), short, dots, emit)
    y = x_ref[...] + _dot(a_ref[...], wdown_ref[...])
    if final_norm:
        y = _rmsnorm(y, nf_ref[...])
    o_ref[...] = y


def _head_cols(v, h0, lo):
    return jnp.where(lo, v[:, h0:h0 + 1], v[:, h0 + 1:h0 + 2])


def _ssm_in_kernel(x_ref, pre_ref, nw_ref, w_ref, wdt_ref, cw_ref, cb_ref, dtb_ref, alog_ref,
                   z_ref, xbc_ref, *rest, short):
    tm = x_ref.shape[0]
    h = _rmsnorm(x_ref[...], nw_ref[...]).astype(BF16)
    cum_ref, csp_ref, we_ref, da_ref, tail_ref, buf_ref = rest
    if not short:

        @pl.when(pl.program_id(1) == 0)
        def _():
            buf_ref[0:SUBLANES, :] = pre_ref[...]

    def dot_xbc(i):
        cols = slice(i * COL_BLOCK, (i + 1) * COL_BLOCK)
        buf_ref[SUBLANES:, cols] = _dot(h, w_ref[:, D_INNER + i * COL_BLOCK:D_INNER + (i + 1) * COL_BLOCK])

    def dot_z(i):
        cols = slice(i * COL_BLOCK, (i + 1) * COL_BLOCK)
        z_ref[:, cols] = _dot(h, w_ref[:, cols]).astype(BF16)

    def emit(r0, cols, y):
        xbc_ref[r0:r0 + y.shape[0], cols] = _silu(y + cb_ref[:, cols]).astype(BF16)

    def conv_block(i):
        _conv_strips(buf_ref, pre_ref, cw_ref, 4, tm, i * COL_BLOCK, (i + 1) * COL_BLOCK, short, emit)

    n_x, n_z = CONV_DIM // COL_BLOCK, D_INNER // COL_BLOCK
    dot_xbc(0)
    for i in range(n_x):
        if i + 1 < n_x:
            dot_xbc(i + 1)
        if i < n_z:
            dot_z(i)
        conv_block(i)
    for i in range(n_x, n_z):
        dot_z(i)
    dt = _softplus(_dot(h, wdt_ref[...]) + dtb_ref[...])

    if short:
        for s in range(tm // SHORT_LEN):
            tail_ref[s] = buf_ref[SUBLANES + (s + 1) * SHORT_LEN - 3:SUBLANES + (s + 1) * SHORT_LEN, :]
        _decay_terms_short(dt, alog_ref, cum_ref, csp_ref, we_ref, da_ref)
    else:
        tail_ref[...] = buf_ref[tm:, :]
        buf_ref[0:SUBLANES, :] = buf_ref[tm:, :]
        _decay_terms(dt, alog_ref, cum_ref, csp_ref, we_ref, da_ref)


def _decay_terms(dt, alog_ref, cum_ref, cspt_ref, we_ref, da_ref):
    tm = dt.shape[0]
    a = -jnp.exp(alog_ref[...])
    li = lax.broadcasted_iota(jnp.int32, (CHUNK, CHUNK), 0)
    si = lax.broadcasted_iota(jnp.int32, (CHUNK, CHUNK), 1)
    tri = (si <= li).astype(BF16)
    lo8 = lax.broadcasted_iota(jnp.int32, (SUBLANES, LANES), 1) < HEAD_DIM
    pad = CHUNK - tm if tm < CHUNK else 0
    for c in range(max(tm // CHUNK, 1)):
        if pad:
            dt_c = jnp.concatenate([jnp.zeros((pad, LANES), F32), dt], axis=0)
        else:
            dt_c = dt[c * CHUNK:(c + 1) * CHUNK]
        rows = slice(c * CHUNK, (c + 1) * CHUNK)
        cum = _dot_exact_lhs(tri, dt_c * a)
        end = cum[CHUNK - 1:CHUNK, :]
        cum_ref[rows, :] = cum
        cspt_ref[:, rows] = (cum - jnp.log(dt_c)).T
        we_ref[rows, 0:LANES] = (jnp.exp(end - cum) * dt_c).astype(BF16)
        we_ref[rows, LANES:2 * LANES] = jnp.exp(cum).astype(BF16)
        e_end = jnp.broadcast_to(jnp.exp(end), (SUBLANES, LANES))
        for pr in range(N_HEADS // 2):
            da_ref[c * SUBLANES:(c + 1) * SUBLANES, pr * LANES:(pr + 1) * LANES] = _head_cols(e_end, 2 * pr, lo8)


def _decay_terms_short(dt, alog_ref, cum_ref, csp_ref, we_ref, da_ref):
    tm = dt.shape[0]
    dta = dt * -jnp.exp(alog_ref[...])
    li = lax.broadcasted_iota(jnp.int32, (tm, tm), 0)
    si = lax.broadcasted_iota(jnp.int32, (tm, tm), 1)
    same_seq = (li // SHORT_LEN) == (si // SHORT_LEN)
    cum = _dot_exact_lhs((same_seq & (si <= li)).astype(BF16), dta)
    end = _dot_exact_lhs(same_seq.astype(BF16), dta)
    cum_ref[...] = cum
    csp_ref[...] = cum - jnp.log(dt)
    we_ref[:, 0:LANES] = (jnp.exp(end - cum) * dt).astype(BF16)
    we_ref[:, LANES:2 * LANES] = jnp.exp(cum).astype(BF16)
    n_seq = tm // SHORT_LEN
    seq_of_tok = lax.broadcasted_iota(jnp.int32, (n_seq, tm), 1) // SHORT_LEN
    member = (seq_of_tok == lax.broadcasted_iota(jnp.int32, (n_seq, tm), 0)).astype(BF16)
    da_ref[...] = jnp.exp(_dot_exact_lhs(member, dta))


def _ssd_long_kernel(x_ref, z_ref, xbc_ref, cum_ref, cspt_ref, we_ref, da_ref, s0_ref, e_ref,
                     dskip_ref, nw_ref, wout_ref, o_ref, sout_ref, st_ref):
    q = CHUNK

    @pl.when(pl.program_id(1) == 0)
    def _():
        st_ref[...] = s0_ref[...].T

    li = lax.broadcasted_iota(jnp.int32, (q, q), 0)
    si = lax.broadcasted_iota(jnp.int32, (q, q), 1)
    causal = si <= li
    lane = lax.broadcasted_iota(jnp.int32, (q, LANES), 1)
    m_lo = (lane < HEAD_DIM).astype(BF16)
    m_hi = (lane >= HEAD_DIM).astype(BF16)

    normed_chunks = []
    for c in range(x_ref.shape[0] // q):
        rows = slice(c * q, (c + 1) * q)
        cum = cum_ref[rows, :]
        cspt = cspt_ref[:, rows]
        w_end = we_ref[rows, 0:LANES]
        e_cum = we_ref[rows, LANES:2 * LANES]
        normed = []
        for g in range(N_GROUPS):
            gl = slice(g * GROUP_W, (g + 1) * GROUP_W)
            bg = xbc_ref[rows, D_INNER + g * D_STATE:D_INNER + (g + 1) * D_STATE]
            cg = xbc_ref[rows, D_INNER + BC_W + g * D_STATE:D_INNER + BC_W + (g + 1) * D_STATE]
            xs_b = xbc_ref[rows, gl]
            xs = xs_b.astype(F32)
            st_g = st_ref[:, gl]
            cb = _dot_nt(cg, bg)
            y_off = _dot(cg, st_g.astype(BF16))
            w_exp = _dot(w_end, e_ref[:, gl])
            e_exp = _dot(e_cum, e_ref[:, gl])
            xw = xs_b * w_exp.astype(BF16)
            bg_t = bg.astype(F32).T.astype(BF16)
            st_ref[:, gl] = st_g * da_ref[c * SUBLANES:c * SUBLANES + 1, gl] + _dot(bg_t, xw)

            ys = []
            for pr in range(HEADS_PER_GROUP // 2):
                h0 = g * HEADS_PER_GROUP + 2 * pr
                ms = []
                for h in (h0, h0 + 1):
                    seg = cum[:, h:h + 1] - cspt[h:h + 1, :]
                    ms.append((cb * jnp.exp(jnp.where(causal, seg, NEG_BIG))).astype(BF16))
                xp = xs_b[:, pr * LANES:(pr + 1) * LANES]
                rhs = jnp.concatenate([xp * m_lo, xp * m_hi], axis=0)
                ys.append(_dot(jnp.concatenate(ms, axis=1), rhs))
            y = jnp.concatenate(ys, axis=1) + y_off * e_exp + dskip_ref[:, gl] * xs
            y = y * _silu(z_ref[rows, gl].astype(F32))
            y = y * lax.rsqrt(jnp.mean(y * y, axis=-1, keepdims=True) + EPS)
            normed.append((y * nw_ref[:, gl]).astype(BF16))
        normed_chunks.append(jnp.concatenate(normed, axis=1))

    o_ref[...] = x_ref[...] + _dot(jnp.concatenate(normed_chunks, axis=0), wout_ref[...])

    @pl.when(pl.program_id(1) == pl.num_programs(1) - 1)
    def _():
        sout_ref[...] = st_ref[...].T


def _ssd_short_kernel(x_ref, z_ref, xbc_ref, cum_ref, csp_ref, we_ref, da_ref, s0_ref, e_ref,
                      dskip_ref, nw_ref, wout_ref, o_ref, sout_ref):
    q = x_ref.shape[0]
    n_seq = q // SHORT_LEN
    cum = cum_ref[...]
    csp_t = csp_ref[...].T
    li = lax.broadcasted_iota(jnp.int32, (q, q), 0)
    si = lax.broadcasted_iota(jnp.int32, (q, q), 1)
    causal = ((li // SHORT_LEN) == (si // SHORT_LEN)) & (si <= li)
    lane = lax.broadcasted_iota(jnp.int32, (q, LANES), 1)
    m_lo = (lane < HEAD_DIM).astype(BF16)
    m_hi = (lane >= HEAD_DIM).astype(BF16)
    row_seq = lax.broadcasted_iota(jnp.int32, (q, LANES), 0) // SHORT_LEN
    row_seq_k = lax.broadcasted_iota(jnp.int32, (LANES, D_STATE), 0) // SHORT_LEN
    w_end = we_ref[:, 0:LANES]
    e_cum = we_ref[:, LANES:2 * LANES]

    normed = []
    for g in range(N_GROUPS):
        gl = slice(g * GROUP_W, (g + 1) * GROUP_W)
        bg = xbc_ref[:, D_INNER + g * D_STATE:D_INNER + (g + 1) * D_STATE]
        cg = xbc_ref[:, D_INNER + BC_W + g * D_STATE:D_INNER + BC_W + (g + 1) * D_STATE]
        xs_b = xbc_ref[:, gl]
        xs = xs_b.astype(F32)
        cb = _dot_nt(cg, bg)
        w_exp = _dot(w_end, e_ref[:, gl])
        e_exp = _dot(e_cum, e_ref[:, gl])

        cg32 = cg.astype(F32)
        y_off = jnp.zeros((q, GROUP_W), F32)
        for s in range(n_seq):
            cmask = jnp.where(row_seq == s, cg32, 0.0).astype(BF16)
            y_off = y_off + _dot_nt(cmask, s0_ref[s, gl, :].astype(BF16))

        ys = []
        for pr in range(HEADS_PER_GROUP // 2):
            h0 = g * HEADS_PER_GROUP + 2 * pr
            ms = []
            for h in (h0, h0 + 1):
                seg = cum[:, h:h + 1] - csp_t[h:h + 1, :]
                ms.append((cb * jnp.exp(jnp.where(causal, seg, NEG_BIG))).astype(BF16))
            xp = xs_b[:, pr * LANES:(pr + 1) * LANES]
            rhs = jnp.concatenate([xp * m_lo, xp * m_hi], axis=0)
            ys.append(_dot(jnp.concatenate(ms, axis=1), rhs))
        y = jnp.concatenate(ys, axis=1) + y_off * e_exp + dskip_ref[:, gl] * xs
        y = y * _silu(z_ref[:, gl].astype(F32))
        y = y * lax.rsqrt(jnp.mean(y * y, axis=-1, keepdims=True) + EPS)
        normed.append((y * nw_ref[:, gl]).astype(BF16))

        xw = (xs_b * w_exp.astype(BF16)).astype(F32)
        xw_t = jnp.concatenate([xw, jnp.zeros((LANES - q, GROUP_W), F32)], axis=0).T.astype(BF16)
        bg_k = jnp.concatenate([bg.astype(F32), jnp.zeros((LANES - q, D_STATE), F32)], axis=0)
        for s in range(n_seq):
            bmask = jnp.where(row_seq_k == s, bg_k, 0.0).astype(BF16)
            upd = _dot(xw_t, bmask)
            for hh in range(HEADS_PER_GROUP):
                h = g * HEADS_PER_GROUP + hh
                rows = slice(h * HEAD_DIM, (h + 1) * HEAD_DIM)
                sout_ref[s, rows, :] = (s0_ref[s, rows, :] * da_ref[s:s + 1, h:h + 1]
                                        + upd[hh * HEAD_DIM:(hh + 1) * HEAD_DIM])

    o_ref[...] = x_ref[...] + _dot(jnp.concatenate(normed, axis=1), wout_ref[...])


def _const_spec(shape):
    return pl.BlockSpec(shape, lambda *_: (0,) * len(shape), pipeline_mode=pl.Buffered(1))


def _layer_spec(arr, layer):
    return pl.BlockSpec((None,) + arr.shape[1:], lambda *_: (layer, 0, 0), pipeline_mode=pl.Buffered(1))


def _token_layout(n_tok, n_seq, tm, short, shared_pre):
    if short:
        grid = (n_tok // tm,)
        tok = lambda i: (i, 0)

        def carried(c, width):
            tail_spec = pl.BlockSpec((tm // SHORT_LEN, width - 1, c), lambda i: (i, 0, 0))
            tail_shape = jax.ShapeDtypeStruct((n_tok // SHORT_LEN, width - 1, c), F32)
            return pl.BlockSpec((tm, c), tok), tail_spec, tail_shape

        return grid, tok, ("arbitrary",), carried
    tiles = n_tok // n_seq // tm
    grid = (n_seq, tiles)
    tok = lambda b, j: (b * tiles + j, 0)
    per_seq = lambda b, j: (b, 0)
    pre_map = (lambda b, j: (0, 0)) if shared_pre else per_seq

    def carried(c, width):
        return (pl.BlockSpec((SUBLANES, c), pre_map), pl.BlockSpec((SUBLANES, c), per_seq),
                jax.ShapeDtypeStruct((n_seq * SUBLANES, c), F32))

    return grid, tok, ("arbitrary", "arbitrary"), carried


def _params(sem):
    return pltpu.CompilerParams(dimension_semantics=sem, vmem_limit_bytes=VMEM_LIMIT)


def _sc_mixer(x, pre, w, layer, *, n_seq, tm, short, shared_pre):
    n_tok, d = x.shape
    grid, tok, sem, carried = _token_layout(n_tok, n_seq, tm, short, shared_pre)
    pre_spec, tail_spec, tail_shape = carried(d, 3)
    return pl.pallas_call(
        functools.partial(_sc_mixer_kernel, short=short),
        grid=grid,
        in_specs=[pl.BlockSpec((tm, d), tok), pre_spec, _layer_spec(w["norm_mix"], 2 * layer),
                  _layer_spec(w["sc_in"], layer), _layer_spec(w["sc_cw"], layer), _layer_spec(w["sc_out"], layer)],
        out_specs=[pl.BlockSpec((tm, d), tok), tail_spec],
        out_shape=[jax.ShapeDtypeStruct((n_tok, d), F32), tail_shape],
        scratch_shapes=[pltpu.VMEM((tm + SUBLANES, d), F32), pltpu.VMEM((tm, d), F32), pltpu.VMEM((tm, d), BF16)],
        compiler_params=_params(sem),
        name="sc_mixer_short" if short else "sc_mixer_long",
    )(x, pre, w["norm_mix"], w["sc_in"], w["sc_cw"], w["sc_out"])


def _conv_ffn(x, pre, w, layer, *, n_seq, tm, short, shared_pre, final_norm):
    n_tok, d = x.shape
    f = w["up"].shape[2]
    grid, tok, sem, carried = _token_layout(n_tok, n_seq, tm, short, shared_pre)
    pre_spec, tail_spec, tail_shape = carried(f, 3)
    return pl.pallas_call(
        functools.partial(_conv_ffn_kernel, short=short, final_norm=final_norm),
        grid=grid,
        in_specs=[pl.BlockSpec((tm, d), tok), pre_spec, _layer_spec(w["norm_ffn"], layer),
                  _layer_spec(w["up"], layer), _layer_spec(w["gate"], layer), _layer_spec(w["ffn_cw"], layer),
                  _layer_spec(w["down"], layer), _const_spec(w["nfin"].shape)],
        out_specs=[pl.BlockSpec((tm, d), tok), tail_spec],
        out_shape=[jax.ShapeDtypeStruct((n_tok, d), F32), tail_shape],
        scratch_shapes=[pltpu.VMEM((tm + SUBLANES, f), F32), pltpu.VMEM((tm, f), F32), pltpu.VMEM((tm, f), BF16)],
        compiler_params=_params(sem),
        name="conv_ffn_short" if short else "conv_ffn_long",
    )(x, pre, w["norm_ffn"], w["up"], w["gate"], w["ffn_cw"], w["down"], w["nfin"])


def _ssm_in(x, pre, w, layer, *, n_seq, tm, short, shared_pre):
    n_tok, d = x.shape
    params = [w["norm_mix"], w["ssm_in"], w["w_dt"], w["ssm_cw"], w["ssm_cb"], w["dtb"], w["alog"]]
    param_specs = [_layer_spec(w["norm_mix"], 2 * layer + 1), _layer_spec(w["ssm_in"], layer),
                   _const_spec(w["w_dt"].shape), _layer_spec(w["ssm_cw"], layer),
                   _const_spec(w["ssm_cb"].shape), _const_spec(w["dtb"].shape), _const_spec(w["alog"].shape)]
    out_shape = [jax.ShapeDtypeStruct((n_tok, D_INNER), BF16), jax.ShapeDtypeStruct((n_tok, CONV_DIM), BF16)]
    if short:
        grid, tok, sem, carried = _token_layout(n_tok, n_seq, tm, True, shared_pre)
        pre_spec, tail_spec, tail_shape = carried(CONV_DIM, 4)
        in_specs = [pl.BlockSpec((tm, d), tok), pre_spec] + param_specs
        out_specs = [pl.BlockSpec((tm, D_INNER), tok), pl.BlockSpec((tm, CONV_DIM), tok),
                     pl.BlockSpec((tm, LANES), tok), pl.BlockSpec((tm, LANES), tok),
                     pl.BlockSpec((tm, 2 * LANES), tok), pl.BlockSpec((tm // SHORT_LEN, LANES), tok), tail_spec]
        out_shape += [jax.ShapeDtypeStruct((n_tok, LANES), F32), jax.ShapeDtypeStruct((n_tok, LANES), F32),
                      jax.ShapeDtypeStruct((n_tok, 2 * LANES), BF16),
                      jax.ShapeDtypeStruct((n_tok // SHORT_LEN, LANES), F32), tail_shape]
    else:
        grid, tok, sem, carried = _token_layout(n_tok, n_seq, tm, False, shared_pre)
        pre_spec, tail_spec, tail_shape = carried(CONV_DIM, 4)
        tiles = n_tok // n_seq // tm
        tok_t = lambda b, j: (0, b * tiles + j)
        ctm = max(tm, CHUNK)
        da_rows = ctm // CHUNK * SUBLANES
        n_rows = n_tok // tm * ctm
        in_specs = [pl.BlockSpec((tm, d), tok), pre_spec] + param_specs
        out_specs = [pl.BlockSpec((tm, D_INNER), tok), pl.BlockSpec((tm, CONV_DIM), tok),
                     pl.BlockSpec((ctm, LANES), tok), pl.BlockSpec((LANES, ctm), tok_t),
                     pl.BlockSpec((ctm, 2 * LANES), tok), pl.BlockSpec((da_rows, D_INNER), tok), tail_spec]
        out_shape += [jax.ShapeDtypeStruct((n_rows, LANES), F32), jax.ShapeDtypeStruct((LANES, n_rows), F32),
                      jax.ShapeDtypeStruct((n_rows, 2 * LANES), BF16),
                      jax.ShapeDtypeStruct((n_rows // CHUNK * SUBLANES, D_INNER), F32), tail_shape]
    return pl.pallas_call(
        functools.partial(_ssm_in_kernel, short=short),
        grid=grid,
        in_specs=in_specs,
        out_specs=out_specs,
        out_shape=out_shape,
        scratch_shapes=[pltpu.VMEM((tm + SUBLANES, CONV_DIM), F32)],
        compiler_params=_params(sem),
        name="ssm_in_short" if short else "ssm_in_long",
    )(x, pre, *params)


def _ssd_long(x, z, xbc, cum, cspt, we, da, s0, w, layer, *, n_seq, tm, shared_s0):
    expand, dskip, nw, w_out = w["expand"], w["dskip"], w["ssm_nw"], w["ssm_out"]
    n_tok, d = x.shape
    tiles = n_tok // n_seq // tm
    tok = lambda b, c: (b * tiles + c, 0)
    tok_t = lambda b, c: (0, b * tiles + c)
    s_spec = pl.BlockSpec((D_INNER, D_STATE), lambda b, c: (b, 0))
    s0_spec = pl.BlockSpec((D_INNER, D_STATE), lambda b, c: (0, 0)) if shared_s0 else s_spec
    return pl.pallas_call(
        _ssd_long_kernel,
        grid=(n_seq, tiles),
        in_specs=[pl.BlockSpec((tm, d), tok), pl.BlockSpec((tm, D_INNER), tok),
                  pl.BlockSpec((tm, CONV_DIM), tok), pl.BlockSpec((tm, LANES), tok),
                  pl.BlockSpec((LANES, tm), tok_t), pl.BlockSpec((tm, 2 * LANES), tok),
                  pl.BlockSpec((tm // CHUNK * SUBLANES, D_INNER), tok), s0_spec,
                  _const_spec(expand.shape), _const_spec(dskip.shape), _const_spec(nw.shape), _layer_spec(w_out, layer)],
        out_specs=[pl.BlockSpec((tm, d), tok), s_spec],
        out_shape=[jax.ShapeDtypeStruct((n_tok, d), F32), jax.ShapeDtypeStruct((n_seq * D_INNER, D_STATE), F32)],
        scratch_shapes=[pltpu.VMEM((D_STATE, D_INNER), F32)],
        compiler_params=_params(("arbitrary", "arbitrary")),
        name="ssd_long",
    )(x, z, xbc, cum, cspt, we, da, s0, expand, dskip, nw, w_out)


def _ssd_short(x, z, xbc, cum, csp, we, da, s0, w, layer, *, tm):
    expand, dskip, nw, w_out = w["expand"], w["dskip"], w["ssm_nw"], w["ssm_out"]
    n_tok, d = x.shape
    n_seq = s0.shape[0]
    tok = lambda i: (i, 0)
    s_spec = pl.BlockSpec((tm // SHORT_LEN, D_INNER, D_STATE), lambda i: (i, 0, 0))
    return pl.pallas_call(
        _ssd_short_kernel,
        grid=(n_tok // tm,),
        in_specs=[pl.BlockSpec((tm, d), tok), pl.BlockSpec((tm, D_INNER), tok), pl.BlockSpec((tm, CONV_DIM), tok),
                  pl.BlockSpec((tm, LANES), tok), pl.BlockSpec((tm, LANES), tok), pl.BlockSpec((tm, 2 * LANES), tok),
                  pl.BlockSpec((tm // SHORT_LEN, LANES), tok), s_spec,
                  _const_spec(expand.shape), _const_spec(dskip.shape), _const_spec(nw.shape), _layer_spec(w_out, layer)],
        out_specs=[pl.BlockSpec((tm, d), tok), s_spec],
        out_shape=[jax.ShapeDtypeStruct((n_tok, d), F32), jax.ShapeDtypeStruct((n_seq, D_INNER, D_STATE), F32)],
        compiler_params=_params(("arbitrary",)),
        name="ssd_short",
    )(x, z, xbc, cum, csp, we, da, s0, expand, dskip, nw, w_out)


def _trunk(x, pre_sc, pre_xbc, s0, pre_ffn, w, *, n_seq, short, tms, shared=False):
    kw = dict(n_seq=n_seq, short=short, shared_pre=shared)
    x1, t_sc = _sc_mixer(x, pre_sc, w, 0, tm=tms[0], **kw)
    x2, t_f0 = _conv_ffn(x1, pre_ffn[0], w, 0, tm=tms[1], final_norm=False, **kw)
    ssm = _ssm_in(x2, pre_xbc, w, 0, tm=tms[2], **kw)
    z, xbc, cum, cspt, we, da, t_xbc = ssm
    if short:
        x3, s_new = _ssd_short(x2, z, xbc, cum, cspt, we, da, s0, w, 0, tm=tms[3])
    else:
        pad_front = cum.shape[0] - x2.shape[0]
        padf = lambda t: jnp.pad(t, ((pad_front, 0), (0, 0))) if pad_front else t
        x3, s_new = _ssd_long(padf(x2), padf(z), padf(xbc), cum, cspt, we, da, s0, w, 0, n_seq=n_seq,
                              tm=max(tms[3], CHUNK), shared_s0=shared)
        x3 = x3[pad_front:]
    y, t_f1 = _conv_ffn(x3, pre_ffn[1], w, 1, tm=tms[4], final_norm=True, **kw)
    return y, t_sc, t_xbc, s_new, (t_f0, t_f1)


def _right_align(cache):
    s, wm1, c = cache.shape
    return jnp.pad(cache, ((0, 0), (SUBLANES - wm1, 0), (0, 0))).reshape(s * SUBLANES, c)


def kernel(x_prompt, x_sample, cache_sc, cache_ssm_conv, state_ssm, cache_ffn_conv, meta_tokens, norm_mix, norm_ffn, norm_final, sc_w_in, sc_conv_w, sc_w_out, ssm_w_in, ssm_conv_w, ssm_conv_b, ssm_dt_bias, ssm_a_log, ssm_d, ssm_norm_w, ssm_w_out, ffn_w_up, ffn_w_gate, ffn_conv_w, ffn_w_down):
    b, seq, d = x_prompt.shape
    n_dec, dec_len, _ = x_sample.shape
    d_ff = ffn_w_up.shape[2]
    assert dec_len == SHORT_LEN and seq % CHUNK == 0 and N_META % SUBLANES == 0 and N_META <= CHUNK

    pad_heads = lambda v: jnp.pad(v.reshape(1, -1).astype(F32), ((0, 0), (0, LANES - N_HEADS)))
    head_of_lane = jnp.arange(D_INNER, dtype=jnp.int32)[None, :] // HEAD_DIM
    w = dict(
        norm_mix=norm_mix.reshape(-1, 1, d), norm_ffn=norm_ffn.reshape(-1, 1, d), nfin=norm_final.reshape(1, d),
        sc_in=sc_w_in.astype(BF16), sc_cw=sc_conv_w, sc_out=sc_w_out.astype(BF16),
        ssm_in=ssm_w_in.astype(BF16),
        w_dt=jnp.pad(ssm_w_in[0][:, D_INNER + CONV_DIM:], ((0, 0), (0, LANES - N_HEADS))).astype(BF16),
        ssm_cw=ssm_conv_w, ssm_cb=ssm_conv_b[0:1], dtb=pad_heads(ssm_dt_bias[0]), alog=pad_heads(ssm_a_log[0]),
        dskip=jnp.repeat(ssm_d[0], HEAD_DIM).reshape(1, -1), ssm_nw=ssm_norm_w[0:1], ssm_out=ssm_w_out.astype(BF16),
        expand=(jnp.arange(LANES, dtype=jnp.int32)[:, None] == head_of_lane).astype(BF16),
        up=ffn_w_up.astype(BF16), gate=ffn_w_gate.astype(BF16), ffn_cw=ffn_conv_w, down=ffn_w_down.astype(BF16),
    )

    zeros8 = lambda c: jnp.zeros((SUBLANES, c), F32)
    _, m_sc, m_xbc, m_state, m_ffn = _trunk(
        meta_tokens.astype(F32), zeros8(d), zeros8(CONV_DIM), jnp.zeros((D_INNER, D_STATE), F32),
        (zeros8(d_ff), zeros8(d_ff)), w, n_seq=1, short=False, tms=(N_META,) * 5)

    yp, p_sc, p_xbc, p_state, p_ffn = _trunk(
        x_prompt.reshape(b * seq, d), m_sc, m_xbc, m_state, m_ffn, w,
        n_seq=b, short=False, shared=True, tms=(1024, 512, 512, 512, 512))
    tail = lambda t, k: t.reshape(b, SUBLANES, -1)[:, SUBLANES - k:]
    out_prompt = (
        yp.reshape(b, seq, d),
        tail(p_sc, 2)[None], tail(p_xbc, 3)[None],
        p_state.reshape(1, b, N_HEADS, HEAD_DIM, D_STATE),
        jnp.stack([tail(p_ffn[0], 2), tail(p_ffn[1], 2)]),
    )

    ys, s_sc, s_xbc, s_state, s_ffn = _trunk(
        x_sample.reshape(n_dec * dec_len, d), _right_align(cache_sc[0]), _right_align(cache_ssm_conv[0]),
        state_ssm[0].reshape(n_dec, D_INNER, D_STATE),
        (_right_align(cache_ffn_conv[0]), _right_align(cache_ffn_conv[1])), w,
        n_seq=n_dec, short=True, tms=(256, 256, 256, 64, 256))
    out_sample = (
        ys.reshape(n_dec, dec_len, d),
        s_sc[None], s_xbc[None],
        s_state.reshape(1, n_dec, N_HEADS, HEAD_DIM, D_STATE),
        jnp.stack([s_ffn[0], s_ffn[1]]),
    )
    return (out_prompt[0], out_sample[0]) + out_prompt[1:] + out_sample[1:]
```

```python
import functools

import jax
import jax.numpy as jnp
from jax import lax
from jax.experimental import pallas as pl
from jax.experimental.pallas import tpu as pltpu

F32 = jnp.float32
BF16 = jnp.bfloat16

EPS = 1e-5
N_META = 16
HEAD_DIM = 64
N_HEADS = 32
N_GROUPS = 4
HEADS_PER_GROUP = N_HEADS // N_GROUPS
D_STATE = 128
D_INNER = N_HEADS * HEAD_DIM
GROUP_W = D_INNER // N_GROUPS
BC_W = N_GROUPS * D_STATE
CONV_DIM = D_INNER + 2 * BC_W
CHUNK = 128
SHORT_LEN = 8
SUBLANES = 8
LANES = 128
STRIP = 64
COL_BLOCK = 512
FFN_COL_BLOCK = 768
NEG_BIG = -1e30
NEG_LOG2E = -1.4426950408889634
VMEM_LIMIT = 56 * 1024 * 1024


def _rmsnorm(x, w):
    r = lax.rsqrt(jnp.mean(x * x, axis=-1, keepdims=True) + EPS)
    return x * r * w


def _silu(x):
    return x / (1.0 + jnp.exp2(x * NEG_LOG2E))


def _softplus(x):
    return jnp.maximum(x, 0.0) + jnp.log1p(jnp.exp(-jnp.abs(x)))


def _dot(a, b):
    return jnp.dot(a, b, preferred_element_type=F32)


def _dot_nt(a, b):
    return lax.dot_general(a, b, (((1,), (1,)), ((), ())), preferred_element_type=F32)


def _dot_exact_lhs(m_bf16, x):
    hi = x.astype(BF16)
    r1 = x - hi.astype(F32)
    mid = r1.astype(BF16)
    lo = (r1 - mid.astype(F32)).astype(BF16)
    return _dot(m_bf16, hi) + _dot(m_bf16, mid) + _dot(m_bf16, lo)


def _conv_strips(buf_ref, pre_ref, w_ref, width, tm, col_lo, col_hi, short, emit):
    rows = min(STRIP, tm)
    row_in_seq = lax.broadcasted_iota(jnp.int32, (rows, LANES), 0) & (SHORT_LEN - 1)
    row8 = lax.broadcasted_iota(jnp.int32, (SHORT_LEN, LANES), 0)
    for c0 in range(col_lo, col_hi, LANES):
        cols = slice(c0, c0 + LANES)
        taps = [w_ref[k:k + 1, cols] for k in range(width)]
        for r0 in range(0, tm, rows):
            if short:
                xv = buf_ref[SUBLANES + r0:SUBLANES + r0 + rows, cols]
                y = xv * taps[width - 1]
                for k in range(1, width):
                    cached = []
                    for s in range(r0 // SHORT_LEN, (r0 + rows) // SHORT_LEN):
                        piece = jnp.broadcast_to(pre_ref[s, width - 2:width - 1, cols], (SHORT_LEN, LANES))
                        for j in range(k - 2, -1, -1):
                            row = jnp.broadcast_to(pre_ref[s, width - 1 - k + j:width - k + j, cols], (SHORT_LEN, LANES))
                            piece = jnp.where(row8 == j, row, piece)
                        cached.append(piece)
                    xk = jnp.where(row_in_seq < k, jnp.concatenate(cached, axis=0), pltpu.roll(xv, k, axis=0))
                    y = y + xk * taps[width - 1 - k]
            else:
                ext = buf_ref[r0:r0 + rows + SUBLANES, cols]
                y = ext[SUBLANES:] * taps[width - 1]
                for k in range(1, width):
                    y = y + pltpu.roll(ext, k, axis=0)[SUBLANES:] * taps[width - 1 - k]
            emit(r0, cols, y)


def _col_blocks(n_cols, block):
    return [(lo, min(lo + block, n_cols)) for lo in range(0, n_cols, block)]


def _pipelined_conv(buf_ref, pre_ref, tail_ref, cw_ref, width, tm, blocks, short, dots, emit):
    if not short:
        @pl.when(pl.program_id(1) == 0)
        def _():
            buf_ref[0:SUBLANES, :] = pre_ref[...]

    dots(*blocks[0])
    for i, (lo, hi) in enumerate(blocks):
        if i + 1 < len(blocks):
            dots(*blocks[i + 1])
        _conv_strips(buf_ref, pre_ref, cw_ref, width, tm, lo, hi, short, emit)

    if short:
        for s in range(tm // SHORT_LEN):
            last = SUBLANES + (s + 1) * SHORT_LEN
            tail_ref[s] = buf_ref[last - (width - 1):last, :]
    else:
        tail_ref[...] = buf_ref[tm:, :]
        buf_ref[0:SUBLANES, :] = buf_ref[tm:, :]


def _sc_mixer_kernel(x_ref, pre_ref, nw_ref, win_ref, cw_ref, wout_ref, o_ref, tail_ref,
                     buf_ref, b_ref, g_ref, *, short):
    tm, d = x_ref.shape
    h = _rmsnorm(x_ref[...], nw_ref[...]).astype(BF16)

    def dots(lo, hi):
        buf_ref[SUBLANES:, lo:hi] = _dot(h, win_ref[:, d + lo:d + hi]) * _dot(h, win_ref[:, 2 * d + lo:2 * d + hi])
        b_ref[:, lo:hi] = _dot(h, win_ref[:, lo:hi])

    def emit(r0, cols, y):
        rows = slice(r0, r0 + y.shape[0])
        g_ref[rows, cols] = (b_ref[rows, cols] * y).astype(BF16)

    _pipelined_conv(buf_ref, pre_ref, tail_ref, cw_ref, 3, tm, _col_blocks(d, COL_BLOCK), short, dots, emit)
    o_ref[...] = x_ref[...] + _dot(g_ref[...], wout_ref[...])


def _conv_ffn_kernel(x_ref, pre_ref, nw_ref, wup_ref, wgate_ref, cw_ref, wdown_ref, nf_ref,
                     o_ref, tail_ref, buf_ref, g_ref, a_ref, *, short, final_norm):
    tm = x_ref.shape[0]
    f = wup_ref.shape[1]
    h = _rmsnorm(x_ref[...], nw_ref[...]).astype(BF16)

    def dots(lo, hi):
        buf_ref[SUBLANES:, lo:hi] = _dot(h, wup_ref[:, lo:hi])
        g_ref[:, lo:hi] = _dot(h, wgate_ref[:, lo:hi])

    def emit(r0, cols, y):
        rows = slice(r0, r0 + y.shape[0])
        a_ref[rows, cols] = (_silu(y) * g_ref[rows, cols]).astype(BF16)

    _pipelined_conv(buf_ref, pre_ref, tail_ref, cw_ref, 3, tm, _col_blocks(f, FFN_COL_BLOCK), short, dots, emit)
    y = x_ref[...] + _dot(a_ref[...], wdown_ref[...])
    if final_norm:
        y = _rmsnorm(y, nf_ref[...])
    o_ref[...] = y


def _head_cols(v, h0, lo):
    return jnp.where(lo, v[:, h0:h0 + 1], v[:, h0 + 1:h0 + 2])


def _ssm_in_kernel(x_ref, pre_ref, nw_ref, w_ref, wdt_ref, cw_ref, cb_ref, dtb_ref, alog_ref,
                   z_ref, xbc_ref, *rest, short):
    tm = x_ref.shape[0]
    h = _rmsnorm(x_ref[...], nw_ref[...]).astype(BF16)
    cum_ref, csp_ref, we_ref, da_ref, tail_ref, buf_ref = rest
    if not short:

        @pl.when(pl.program_id(1) == 0)
        def _():
            buf_ref[0:SUBLANES, :] = pre_ref[...]

    def dot_xbc(i):
        cols = slice(i * COL_BLOCK, (i + 1) * COL_BLOCK)
        buf_ref[SUBLANES:, cols] = _dot(h, w_ref[:, D_INNER + i * COL_BLOCK:D_INNER + (i + 1) * COL_BLOCK])

    def dot_z(i):
        cols = slice(i * COL_BLOCK, (i + 1) * COL_BLOCK)
        z_ref[:, cols] = _dot(h, w_ref[:, cols]).astype(BF16)

    def emit(r0, cols, y):
        xbc_ref[r0:r0 + y.shape[0], cols] = _silu(y + cb_ref[:, cols]).astype(BF16)

    def conv_block(i):
        _conv_strips(buf_ref, pre_ref, cw_ref, 4, tm, i * COL_BLOCK, (i + 1) * COL_BLOCK, short, emit)

    n_x, n_z = CONV_DIM // COL_BLOCK, D_INNER // COL_BLOCK
    dot_xbc(0)
    for i in range(n_x):
        if i + 1 < n_x:
            dot_xbc(i + 1)
        if i < n_z:
            dot_z(i)
        conv_block(i)
    for i in range(n_x, n_z):
        dot_z(i)
    dt = _softplus(_dot(h, wdt_ref[...]) + dtb_ref[...])

    if short:
        for s in range(tm // SHORT_LEN):
            tail_ref[s] = buf_ref[SUBLANES + (s + 1) * SHORT_LEN - 3:SUBLANES + (s + 1) * SHORT_LEN, :]
        _decay_terms_short(dt, alog_ref, cum_ref, csp_ref, we_ref, da_ref)
    else:
        tail_ref[...] = buf_ref[tm:, :]
        buf_ref[0:SUBLANES, :] = buf_ref[tm:, :]
        _decay_terms(dt, alog_ref, cum_ref, csp_ref, we_ref, da_ref)


def _decay_terms(dt, alog_ref, cum_ref, cspt_ref, we_ref, da_ref):
    tm = dt.shape[0]
    a = -jnp.exp(alog_ref[...])
    li = lax.broadcasted_iota(jnp.int32, (CHUNK, CHUNK), 0)
    si = lax.broadcasted_iota(jnp.int32, (CHUNK, CHUNK), 1)
    tri = (si <= li).astype(BF16)
    lo8 = lax.broadcasted_iota(jnp.int32, (SUBLANES, LANES), 1) < HEAD_DIM
    pad = CHUNK - tm if tm < CHUNK else 0
    for c in range(max(tm // CHUNK, 1)):
        if pad:
            dt_c = jnp.concatenate([jnp.zeros((pad, LANES), F32), dt], axis=0)
        else:
            dt_c = dt[c * CHUNK:(c + 1) * CHUNK]
        rows = slice(c * CHUNK, (c + 1) * CHUNK)
        cum = _dot_exact_lhs(tri, dt_c * a)
        end = cum[CHUNK - 1:CHUNK, :]
        cum_ref[rows, :] = cum
        cspt_ref[:, rows] = (cum - jnp.log(dt_c)).T
        we_ref[rows, 0:LANES] = (jnp.exp(end - cum) * dt_c).astype(BF16)
        we_ref[rows, LANES:2 * LANES] = jnp.exp(cum).astype(BF16)
        e_end = jnp.broadcast_to(jnp.exp(end), (SUBLANES, LANES))
        for pr in range(N_HEADS // 2):
            da_ref[c * SUBLANES:(c + 1) * SUBLANES, pr * LANES:(pr + 1) * LANES] = _head_cols(e_end, 2 * pr, lo8)


def _decay_terms_short(dt, alog_ref, cum_ref, csp_ref, we_ref, da_ref):
    tm = dt.shape[0]
    dta = dt * -jnp.exp(alog_ref[...])
    li = lax.broadcasted_iota(jnp.int32, (tm, tm), 0)
    si = lax.broadcasted_iota(jnp.int32, (tm, tm), 1)
    same_seq = (li // SHORT_LEN) == (si // SHORT_LEN)
    cum = _dot_exact_lhs((same_seq & (si <= li)).astype(BF16), dta)
    end = _dot_exact_lhs(same_seq.astype(BF16), dta)
    cum_ref[...] = cum
    csp_ref[...] = cum - jnp.log(dt)
    we_ref[:, 0:LANES] = (jnp.exp(end - cum) * dt).astype(BF16)
    we_ref[:, LANES:2 * LANES] = jnp.exp(cum).astype(BF16)
    n_seq = tm // SHORT_LEN
    seq_of_tok = lax.broadcasted_iota(jnp.int32, (n_seq, tm), 1) // SHORT_LEN
    member = (seq_of_tok == lax.broadcasted_iota(jnp.int32, (n_seq, tm), 0)).astype(BF16)
    da_ref[...] = jnp.exp(_dot_exact_lhs(member, dta))


def _ssd_long_kernel(x_ref, z_ref, xbc_ref, cum_ref, cspt_ref, we_ref, da_ref, s0_ref, e_ref,
                     dskip_ref, nw_ref, wout_ref, o_ref, sout_ref, st_ref):
    q = CHUNK

    @pl.when(pl.program_id(1) == 0)
    def _():
        st_ref[...] = s0_ref[...].T

    li = lax.broadcasted_iota(jnp.int32, (q, q), 0)
    si = lax.broadcasted_iota(jnp.int32, (q, q), 1)
    causal = si <= li
    lane = lax.broadcasted_iota(jnp.int32, (q, LANES), 1)
    m_lo = (lane < HEAD_DIM).astype(BF16)
    m_hi = (lane >= HEAD_DIM).astype(BF16)

    normed_chunks = []
    for c in range(x_ref.shape[0] // q):
        rows = slice(c * q, (c + 1) * q)
        cum = cum_ref[rows, :]
        cspt = cspt_ref[:, rows]
        w_end = we_ref[rows, 0:LANES]
        e_cum = we_ref[rows, LANES:2 * LANES]
        normed = []
        for g in range(N_GROUPS):
            gl = slice(g * GROUP_W, (g + 1) * GROUP_W)
            bg = xbc_ref[rows, D_INNER + g * D_STATE:D_INNER + (g + 1) * D_STATE]
            cg = xbc_ref[rows, D_INNER + BC_W + g * D_STATE:D_INNER + BC_W + (g + 1) * D_STATE]
            xs_b = xbc_ref[rows, gl]
            xs = xs_b.astype(F32)
            st_g = st_ref[:, gl]
            cb = _dot_nt(cg, bg)
            y_off = _dot(cg, st_g.astype(BF16))
            w_exp = _dot(w_end, e_ref[:, gl])
            e_exp = _dot(e_cum, e_ref[:, gl])
            xw = xs_b * w_exp.astype(BF16)
            bg_t = bg.astype(F32).T.astype(BF16)
            st_ref[:, gl] = st_g * da_ref[c * SUBLANES:c * SUBLANES + 1, gl] + _dot(bg_t, xw)

            ys = []
            for pr in range(HEADS_PER_GROUP // 2):
                h0 = g * HEADS_PER_GROUP + 2 * pr
                ms = []
                for h in (h0, h0 + 1):
                    seg = cum[:, h:h + 1] - cspt[h:h + 1, :]
                    ms.append((cb * jnp.exp(jnp.where(causal, seg, NEG_BIG))).astype(BF16))
                xp = xs_b[:, pr * LANES:(pr + 1) * LANES]
                rhs = jnp.concatenate([xp * m_lo, xp * m_hi], axis=0)
                ys.append(_dot(jnp.concatenate(ms, axis=1), rhs))
            y = jnp.concatenate(ys, axis=1) + y_off * e_exp + dskip_ref[:, gl] * xs
            y = y * _silu(z_ref[rows, gl].astype(F32))
            y = y * lax.rsqrt(jnp.mean(y * y, axis=-1, keepdims=True) + EPS)
            normed.append((y * nw_ref[:, gl]).astype(BF16))
        normed_chunks.append(jnp.concatenate(normed, axis=1))

    o_ref[...] = x_ref[...] + _dot(jnp.concatenate(normed_chunks, axis=0), wout_ref[...])

    @pl.when(pl.program_id(1) == pl.num_programs(1) - 1)
    def _():
        sout_ref[...] = st_ref[...].T


def _ssd_short_kernel(x_ref, z_ref, xbc_ref, cum_ref, csp_ref, we_ref, da_ref, s0_ref, e_ref,
                      dskip_ref, nw_ref, wout_ref, o_ref, sout_ref):
    q = x_ref.shape[0]
    n_seq = q // SHORT_LEN
    cum = cum_ref[...]
    csp_t = csp_ref[...].T
    li = lax.broadcasted_iota(jnp.int32, (q, q), 0)
    si = lax.broadcasted_iota(jnp.int32, (q, q), 1)
    causal = ((li // SHORT_LEN) == (si // SHORT_LEN)) & (si <= li)
    lane = lax.broadcasted_iota(jnp.int32, (q, LANES), 1)
    m_lo = (lane < HEAD_DIM).astype(BF16)
    m_hi = (lane >= HEAD_DIM).astype(BF16)
    row_seq = lax.broadcasted_iota(jnp.int32, (q, LANES), 0) // SHORT_LEN
    row_seq_k = lax.broadcasted_iota(jnp.int32, (LANES, D_STATE), 0) // SHORT_LEN
    w_end = we_ref[:, 0:LANES]
    e_cum = we_ref[:, LANES:2 * LANES]

    normed = []
    for g in range(N_GROUPS):
        gl = slice(g * GROUP_W, (g + 1) * GROUP_W)
        bg = xbc_ref[:, D_INNER + g * D_STATE:D_INNER + (g + 1) * D_STATE]
        cg = xbc_ref[:, D_INNER + BC_W + g * D_STATE:D_INNER + BC_W + (g + 1) * D_STATE]
        xs_b = xbc_ref[:, gl]
        xs = xs_b.astype(F32)
        cb = _dot_nt(cg, bg)
        w_exp = _dot(w_end, e_ref[:, gl])
        e_exp = _dot(e_cum, e_ref[:, gl])

        cg32 = cg.astype(F32)
        y_off = jnp.zeros((q, GROUP_W), F32)
        for s in range(n_seq):
            cmask = jnp.where(row_seq == s, cg32, 0.0).astype(BF16)
            y_off = y_off + _dot_nt(cmask, s0_ref[s, gl, :].astype(BF16))

        ys = []
        for pr in range(HEADS_PER_GROUP // 2):
            h0 = g * HEADS_PER_GROUP + 2 * pr
            ms = []
            for h in (h0, h0 + 1):
                seg = cum[:, h:h + 1] - csp_t[h:h + 1, :]
                ms.append((cb * jnp.exp(jnp.where(causal, seg, NEG_BIG))).astype(BF16))
            xp = xs_b[:, pr * LANES:(pr + 1) * LANES]
            rhs = jnp.concatenate([xp * m_lo, xp * m_hi], axis=0)
            ys.append(_dot(jnp.concatenate(ms, axis=1), rhs))
        y = jnp.concatenate(ys, axis=1) + y_off * e_exp + dskip_ref[:, gl] * xs
        y = y * _silu(z_ref[:, gl].astype(F32))
        y = y * lax.rsqrt(jnp.mean(y * y, axis=-1, keepdims=True) + EPS)
        normed.append((y * nw_ref[:, gl]).astype(BF16))

        xw = (xs_b * w_exp.astype(BF16)).astype(F32)
        xw_t = jnp.concatenate([xw, jnp.zeros((LANES - q, GROUP_W), F32)], axis=0).T.astype(BF16)
        bg_k = jnp.concatenate([bg.astype(F32), jnp.zeros((LANES - q, D_STATE), F32)], axis=0)
        for s in range(n_seq):
            bmask = jnp.where(row_seq_k == s, bg_k, 0.0).astype(BF16)
            upd = _dot(xw_t, bmask)
            for hh in range(HEADS_PER_GROUP):
                h = g * HEADS_PER_GROUP + hh
                rows = slice(h * HEAD_DIM, (h + 1) * HEAD_DIM)
                sout_ref[s, rows, :] = (s0_ref[s, rows, :] * da_ref[s:s + 1, h:h + 1]
                                        + upd[hh * HEAD_DIM:(hh + 1) * HEAD_DIM])

    o_ref[...] = x_ref[...] + _dot(jnp.concatenate(normed, axis=1), wout_ref[...])


def _const_spec(shape):
    return pl.BlockSpec(shape, lambda *_: (0,) * len(shape), pipeline_mode=pl.Buffered(1))


def _layer_spec(arr, layer):
    return pl.BlockSpec((None,) + arr.shape[1:], lambda *_: (layer, 0, 0), pipeline_mode=pl.Buffered(1))


def _token_layout(n_tok, n_seq, tm, short, shared_pre):
    if short:
        grid = (n_tok // tm,)
        tok = lambda i: (i, 0)

        def carried(c, width, layer):
            block = (tm // SHORT_LEN, width - 1, c)
            pre_spec = pl.BlockSpec((None,) + block, lambda i: (layer, i, 0, 0))
            tail_shape = jax.ShapeDtypeStruct((n_tok // SHORT_LEN, width - 1, c), F32)
            return pre_spec, pl.BlockSpec(block, lambda i: (i, 0, 0)), tail_shape

        return grid, tok, ("arbitrary",), carried
    tiles = n_tok // n_seq // tm
    grid = (n_seq, tiles)
    tok = lambda b, j: (b * tiles + j, 0)
    per_seq = lambda b, j: (b, 0)
    pre_map = (lambda b, j: (0, 0)) if shared_pre else per_seq

    def carried(c, width, layer):
        return (pl.BlockSpec((SUBLANES, c), pre_map), pl.BlockSpec((SUBLANES, c), per_seq),
                jax.ShapeDtypeStruct((n_seq * SUBLANES, c), F32))

    return grid, tok, ("arbitrary", "arbitrary"), carried


def _params(sem):
    return pltpu.CompilerParams(dimension_semantics=sem, vmem_limit_bytes=VMEM_LIMIT)


def _cast_kernel(x_ref, o_ref):
    o_ref[...] = x_ref[...].astype(o_ref.dtype)


def _cast_rows(w, block, dtype):
    n_layers, rows, cols = w.shape
    spec = pl.BlockSpec((None, block, cols), lambda l, j: (l, j, 0))
    return pl.pallas_call(
        _cast_kernel, grid=(n_layers, rows // block), in_specs=[spec], out_specs=spec,
        out_shape=jax.ShapeDtypeStruct(w.shape, dtype), name="cast_rows",
    )(w)


def _sc_mixer(x, pre, w, layer, *, n_seq, tm, short, shared_pre):
    n_tok, d = x.shape
    grid, tok, sem, carried = _token_layout(n_tok, n_seq, tm, short, shared_pre)
    pre_spec, tail_spec, tail_shape = carried(d, 3, layer)
    return pl.pallas_call(
        functools.partial(_sc_mixer_kernel, short=short),
        grid=grid,
        in_specs=[pl.BlockSpec((tm, d), tok), pre_spec, _layer_spec(w["norm_mix"], 2 * layer),
                  _layer_spec(w["sc_in"], layer), _layer_spec(w["sc_cw"], layer), _layer_spec(w["sc_out"], layer)],
        out_specs=[pl.BlockSpec((tm, d), tok), tail_spec],
        out_shape=[jax.ShapeDtypeStruct((n_tok, d), F32), tail_shape],
        scratch_shapes=[pltpu.VMEM((tm + SUBLANES, d), F32), pltpu.VMEM((tm, d), F32), pltpu.VMEM((tm, d), BF16)],
        compiler_params=_params(sem),
        name="sc_mixer_short" if short else "sc_mixer_long",
    )(x, pre, w["norm_mix"], w["sc_in"], w["sc_cw"], w["sc_out"])


def _conv_ffn(x, pre, w, layer, *, n_seq, tm, short, shared_pre, final_norm):
    n_tok, d = x.shape
    f = w["up"].shape[2]
    grid, tok, sem, carried = _token_layout(n_tok, n_seq, tm, short, shared_pre)
    pre_spec, tail_spec, tail_shape = carried(f, 3, layer)
    return pl.pallas_call(
        functools.partial(_conv_ffn_kernel, short=short, final_norm=final_norm),
        grid=grid,
        in_specs=[pl.BlockSpec((tm, d), tok), pre_spec, _layer_spec(w["norm_ffn"], layer),
                  _layer_spec(w["up"], layer), _layer_spec(w["gate"], layer), _layer_spec(w["ffn_cw"], layer),
                  _layer_spec(w["down"], layer), _const_spec(w["nfin"].shape)],
        out_specs=[pl.BlockSpec((tm, d), tok), tail_spec],
        out_shape=[jax.ShapeDtypeStruct((n_tok, d), F32), tail_shape],
        scratch_shapes=[pltpu.VMEM((tm + SUBLANES, f), F32), pltpu.VMEM((tm, f), F32), pltpu.VMEM((tm, f), BF16)],
        compiler_params=_params(sem),
        name="conv_ffn_short" if short else "conv_ffn_long",
    )(x, pre, w["norm_ffn"], w["up"], w["gate"], w["ffn_cw"], w["down"], w["nfin"])


def _ssm_in(x, pre, w, layer, *, n_seq, tm, short, shared_pre):
    n_tok, d = x.shape
    params = [w["norm_mix"], w["ssm_in"], w["w_dt"], w["ssm_cw"], w["ssm_cb"], w["dtb"], w["alog"]]
    param_specs = [_layer_spec(w["norm_mix"], 2 * layer + 1), _layer_spec(w["ssm_in"], layer),
                   _const_spec(w["w_dt"].shape), _layer_spec(w["ssm_cw"], layer),
                   _const_spec(w["ssm_cb"].shape), _const_spec(w["dtb"].shape), _const_spec(w["alog"].shape)]
    out_shape = [jax.ShapeDtypeStruct((n_tok, D_INNER), BF16), jax.ShapeDtypeStruct((n_tok, CONV_DIM), BF16)]
    if short:
        grid, tok, sem, carried = _token_layout(n_tok, n_seq, tm, True, shared_pre)
        pre_spec, tail_spec, tail_shape = carried(CONV_DIM, 4, layer)
        in_specs = [pl.BlockSpec((tm, d), tok), pre_spec] + param_specs
        out_specs = [pl.BlockSpec((tm, D_INNER), tok), pl.BlockSpec((tm, CONV_DIM), tok),
                     pl.BlockSpec((tm, LANES), tok), pl.BlockSpec((tm, LANES), tok),
                     pl.BlockSpec((tm, 2 * LANES), tok), pl.BlockSpec((tm // SHORT_LEN, LANES), tok), tail_spec]
        out_shape += [jax.ShapeDtypeStruct((n_tok, LANES), F32), jax.ShapeDtypeStruct((n_tok, LANES), F32),
                      jax.ShapeDtypeStruct((n_tok, 2 * LANES), BF16),
                      jax.ShapeDtypeStruct((n_tok // SHORT_LEN, LANES), F32), tail_shape]
    else:
        grid, tok, sem, carried = _token_layout(n_tok, n_seq, tm, False, shared_pre)
        pre_spec, tail_spec, tail_shape = carried(CONV_DIM, 4, layer)
        tiles = n_tok // n_seq // tm
        tok_t = lambda b, j: (0, b * tiles + j)
        ctm = max(tm, CHUNK)
        da_rows = ctm // CHUNK * SUBLANES
        n_rows = n_tok // tm * ctm
        in_specs = [pl.BlockSpec((tm, d), tok), pre_spec] + param_specs
        out_specs = [pl.BlockSpec((tm, D_INNER), tok), pl.BlockSpec((tm, CONV_DIM), tok),
                     pl.BlockSpec((ctm, LANES), tok), pl.BlockSpec((LANES, ctm), tok_t),
                     pl.BlockSpec((ctm, 2 * LANES), tok), pl.BlockSpec((da_rows, D_INNER), tok), tail_spec]
        out_shape += [jax.ShapeDtypeStruct((n_rows, LANES), F32), jax.ShapeDtypeStruct((LANES, n_rows), F32),
                      jax.ShapeDtypeStruct((n_rows, 2 * LANES), BF16),
                      jax.ShapeDtypeStruct((n_rows // CHUNK * SUBLANES, D_INNER), F32), tail_shape]
    return pl.pallas_call(
        functools.partial(_ssm_in_kernel, short=short),
        grid=grid,
        in_specs=in_specs,
        out_specs=out_specs,
        out_shape=out_shape,
        scratch_shapes=[pltpu.VMEM((tm + SUBLANES, CONV_DIM), F32)],
        compiler_params=_params(sem),
        name="ssm_in_short" if short else "ssm_in_long",
    )(x, pre, *params)


def _ssd_long(x, z, xbc, cum, cspt, we, da, s0, w, layer, *, n_seq, tm, shared_s0):
    expand, dskip, nw, w_out = w["expand"], w["dskip"], w["ssm_nw"], w["ssm_out"]
    n_tok, d = x.shape
    tiles = n_tok // n_seq // tm
    tok = lambda b, c: (b * tiles + c, 0)
    tok_t = lambda b, c: (0, b * tiles + c)
    s_spec = pl.BlockSpec((D_INNER, D_STATE), lambda b, c: (b, 0))
    s0_spec = pl.BlockSpec((D_INNER, D_STATE), lambda b, c: (0, 0)) if shared_s0 else s_spec
    return pl.pallas_call(
        _ssd_long_kernel,
        grid=(n_seq, tiles),
        in_specs=[pl.BlockSpec((tm, d), tok), pl.BlockSpec((tm, D_INNER), tok),
                  pl.BlockSpec((tm, CONV_DIM), tok), pl.BlockSpec((tm, LANES), tok),
                  pl.BlockSpec((LANES, tm), tok_t), pl.BlockSpec((tm, 2 * LANES), tok),
                  pl.BlockSpec((tm // CHUNK * SUBLANES, D_INNER), tok), s0_spec,
                  _const_spec(expand.shape), _const_spec(dskip.shape), _const_spec(nw.shape), _layer_spec(w_out, layer)],
        out_specs=[pl.BlockSpec((tm, d), tok), s_spec],
        out_shape=[jax.ShapeDtypeStruct((n_tok, d), F32), jax.ShapeDtypeStruct((n_seq * D_INNER, D_STATE), F32)],
        scratch_shapes=[pltpu.VMEM((D_STATE, D_INNER), F32)],
        compiler_params=_params(("arbitrary", "arbitrary")),
        name="ssd_long",
    )(x, z, xbc, cum, cspt, we, da, s0, expand, dskip, nw, w_out)


def _ssd_short(x, z, xbc, cum, csp, we, da, s0, w, layer, *, tm):
    expand, dskip, nw, w_out = w["expand"], w["dskip"], w["ssm_nw"], w["ssm_out"]
    n_tok, d = x.shape
    n_seq = s0.shape[0]
    tok = lambda i: (i, 0)
    s_spec = pl.BlockSpec((tm // SHORT_LEN, D_INNER, D_STATE), lambda i: (i, 0, 0))
    return pl.pallas_call(
        _ssd_short_kernel,
        grid=(n_tok // tm,),
        in_specs=[pl.BlockSpec((tm, d), tok), pl.BlockSpec((tm, D_INNER), tok), pl.BlockSpec((tm, CONV_DIM), tok),
                  pl.BlockSpec((tm, LANES), tok), pl.BlockSpec((tm, LANES), tok), pl.BlockSpec((tm, 2 * LANES), tok),
                  pl.BlockSpec((tm // SHORT_LEN, LANES), tok), s_spec,
                  _const_spec(expand.shape), _const_spec(dskip.shape), _const_spec(nw.shape), _layer_spec(w_out, layer)],
        out_specs=[pl.BlockSpec((tm, d), tok), s_spec],
        out_shape=[jax.ShapeDtypeStruct((n_tok, d), F32), jax.ShapeDtypeStruct((n_seq, D_INNER, D_STATE), F32)],
        compiler_params=_params(("arbitrary",)),
        name="ssd_short",
    )(x, z, xbc, cum, csp, we, da, s0, expand, dskip, nw, w_out)


def _trunk(x, pre_sc, pre_xbc, s0, pre_ffn, w, *, n_seq, short, tms, shared=False):
    kw = dict(n_seq=n_seq, short=short, shared_pre=shared)
    x1, t_sc = _sc_mixer(x, pre_sc, w, 0, tm=tms[0], **kw)
    x2, t_f0 = _conv_ffn(x1, pre_ffn[0], w, 0, tm=tms[1], final_norm=False, **kw)
    ssm = _ssm_in(x2, pre_xbc, w, 0, tm=tms[2], **kw)
    z, xbc, cum, cspt, we, da, t_xbc = ssm
    if short:
        x3, s_new = _ssd_short(x2, z, xbc, cum, cspt, we, da, s0, w, 0, tm=tms[3])
    else:
        pad_front = cum.shape[0] - x2.shape[0]
        padf = lambda t: jnp.pad(t, ((pad_front, 0), (0, 0))) if pad_front else t
        x3, s_new = _ssd_long(padf(x2), padf(z), padf(xbc), cum, cspt, we, da, s0, w, 0, n_seq=n_seq,
                              tm=max(tms[3], CHUNK), shared_s0=shared)
        x3 = x3[pad_front:]
    y, t_f1 = _conv_ffn(x3, pre_ffn[1], w, 1, tm=tms[4], final_norm=True, **kw)
    return y, t_sc, t_xbc, s_new, (t_f0, t_f1)


def kernel(x_prompt, x_sample, cache_sc, cache_ssm_conv, state_ssm, cache_ffn_conv, meta_tokens, norm_mix, norm_ffn, norm_final, sc_w_in, sc_conv_w, sc_w_out, ssm_w_in, ssm_conv_w, ssm_conv_b, ssm_dt_bias, ssm_a_log, ssm_d, ssm_norm_w, ssm_w_out, ffn_w_up, ffn_w_gate, ffn_conv_w, ffn_w_down):
    b, seq, d = x_prompt.shape
    n_dec, dec_len, _ = x_sample.shape
    d_ff = ffn_w_up.shape[2]
    assert dec_len == SHORT_LEN and seq % CHUNK == 0 and N_META % SUBLANES == 0 and N_META <= CHUNK

    pad_heads = lambda v: jnp.pad(v.reshape(1, -1).astype(F32), ((0, 0), (0, LANES - N_HEADS)))
    head_of_lane = jnp.arange(D_INNER, dtype=jnp.int32)[None, :] // HEAD_DIM
    w = dict(
        norm_mix=norm_mix.reshape(-1, 1, d), norm_ffn=norm_ffn.reshape(-1, 1, d), nfin=norm_final.reshape(1, d),
        sc_in=sc_w_in.astype(BF16), sc_cw=sc_conv_w, sc_out=sc_w_out.astype(BF16),
        ssm_in=_cast_rows(ssm_w_in, CHUNK, BF16),
        w_dt=jnp.pad(ssm_w_in[0][:, D_INNER + CONV_DIM:], ((0, 0), (0, LANES - N_HEADS))).astype(BF16),
        ssm_cw=ssm_conv_w, ssm_cb=ssm_conv_b[0:1], dtb=pad_heads(ssm_dt_bias[0]), alog=pad_heads(ssm_a_log[0]),
        dskip=jnp.repeat(ssm_d[0], HEAD_DIM).reshape(1, -1), ssm_nw=ssm_norm_w[0:1], ssm_out=ssm_w_out.astype(BF16),
        expand=(jnp.arange(LANES, dtype=jnp.int32)[:, None] == head_of_lane).astype(BF16),
        up=ffn_w_up.astype(BF16), gate=ffn_w_gate.astype(BF16), ffn_cw=ffn_conv_w, down=ffn_w_down.astype(BF16),
    )

    zeros8 = lambda c: jnp.zeros((SUBLANES, c), F32)
    _, m_sc, m_xbc, m_state, m_ffn = _trunk(
        meta_tokens.astype(F32), zeros8(d), zeros8(CONV_DIM), jnp.zeros((D_INNER, D_STATE), F32),
        (zeros8(d_ff), zeros8(d_ff)), w, n_seq=1, short=False, tms=(N_META,) * 5)

    yp, p_sc, p_xbc, p_state, p_ffn = _trunk(
        x_prompt.reshape(b * seq, d), m_sc, m_xbc, m_state, m_ffn, w,
        n_seq=b, short=False, shared=True, tms=(1024, 512, 512, 512, 512))
    tail = lambda t, k: t.reshape(b, SUBLANES, -1)[:, SUBLANES - k:]
    out_prompt = (
        yp.reshape(b, seq, d),
        tail(p_sc, 2)[None], tail(p_xbc, 3)[None],
        p_state.reshape(1, b, N_HEADS, HEAD_DIM, D_STATE),
        jnp.stack([tail(p_ffn[0], 2), tail(p_ffn[1], 2)]),
    )

    ys, s_sc, s_xbc, s_state, s_ffn = _trunk(
        x_sample.reshape(n_dec * dec_len, d), cache_sc, cache_ssm_conv,
        state_ssm[0].reshape(n_dec, D_INNER, D_STATE), (cache_ffn_conv, cache_ffn_conv), w,
        n_seq=n_dec, short=True, tms=(512, 512, 512, 64, 512))
    out_sample = (
        ys.reshape(n_dec, dec_len, d),
        s_sc[None], s_xbc[None],
        s_state.reshape(1, n_dec, N_HEADS, HEAD_DIM, D_STATE),
        jnp.stack([s_ffn[0], s_ffn[1]]),
    )
    return (out_prompt[0], out_sample[0]) + out_prompt[1:] + out_sample[1:]
```

```python
import functools

import jax
import jax.numpy as jnp
from jax import lax
from jax.experimental import pallas as pl
from jax.experimental.pallas import tpu as pltpu

F32 = jnp.float32
BF16 = jnp.bfloat16

EPS = 1e-5
N_META = 16
HEAD_DIM = 64
N_HEADS = 32
N_GROUPS = 4
HEADS_PER_GROUP = N_HEADS // N_GROUPS
D_STATE = 128
D_INNER = N_HEADS * HEAD_DIM
GROUP_W = D_INNER // N_GROUPS
BC_W = N_GROUPS * D_STATE
CONV_DIM = D_INNER + 2 * BC_W
CHUNK = 128
SHORT_LEN = 8
SUBLANES = 8
LANES = 128
STRIP = 64
COL_BLOCK = 512
FFN_COL_BLOCK = 768
NEG_BIG = -1e30
NEG_LOG2E = -1.4426950408889634
VMEM_LIMIT = 56 * 1024 * 1024


def _rmsnorm(x, w):
    r = lax.rsqrt(jnp.mean(x * x, axis=-1, keepdims=True) + EPS)
    return x * r * w


def _silu(x):
    return x / (1.0 + jnp.exp2(x * NEG_LOG2E))


def _softplus(x):
    return jnp.maximum(x, 0.0) + jnp.log1p(jnp.exp(-jnp.abs(x)))


def _dot(a, b):
    return jnp.dot(a, b, preferred_element_type=F32)


def _dot_nt(a, b):
    return lax.dot_general(a, b, (((1,), (1,)), ((), ())), preferred_element_type=F32)


def _dot_exact_lhs(m_bf16, x):
    hi = x.astype(BF16)
    r1 = x - hi.astype(F32)
    mid = r1.astype(BF16)
    lo = (r1 - mid.astype(F32)).astype(BF16)
    return _dot(m_bf16, hi) + _dot(m_bf16, mid) + _dot(m_bf16, lo)


def _conv_strips(buf_ref, pre_ref, w_ref, width, tm, col_lo, col_hi, short, emit):
    rows = min(STRIP, tm)
    row_in_seq = lax.broadcasted_iota(jnp.int32, (rows, LANES), 0) & (SHORT_LEN - 1)
    row8 = lax.broadcasted_iota(jnp.int32, (SHORT_LEN, LANES), 0)
    for c0 in range(col_lo, col_hi, LANES):
        cols = slice(c0, c0 + LANES)
        taps = [w_ref[k:k + 1, cols] for k in range(width)]
        for r0 in range(0, tm, rows):
            if short:
                xv = buf_ref[SUBLANES + r0:SUBLANES + r0 + rows, cols]
                y = xv * taps[width - 1]
                for k in range(1, width):
                    cached = []
                    for s in range(r0 // SHORT_LEN, (r0 + rows) // SHORT_LEN):
                        piece = jnp.broadcast_to(pre_ref[s, width - 2:width - 1, cols], (SHORT_LEN, LANES))
                        for j in range(k - 2, -1, -1):
                            row = jnp.broadcast_to(pre_ref[s, width - 1 - k + j:width - k + j, cols], (SHORT_LEN, LANES))
                            piece = jnp.where(row8 == j, row, piece)
                        cached.append(piece)
                    xk = jnp.where(row_in_seq < k, jnp.concatenate(cached, axis=0), pltpu.roll(xv, k, axis=0))
                    y = y + xk * taps[width - 1 - k]
            else:
                ext = buf_ref[r0:r0 + rows + SUBLANES, cols]
                y = ext[SUBLANES:] * taps[width - 1]
                for k in range(1, width):
                    y = y + pltpu.roll(ext, k, axis=0)[SUBLANES:] * taps[width - 1 - k]
            emit(r0, cols, y)


def _col_blocks(n_cols, block):
    return [(lo, min(lo + block, n_cols)) for lo in range(0, n_cols, block)]


def _pipelined_conv(buf_ref, pre_ref, tail_ref, cw_ref, width, tm, blocks, short, dots, emit):
    if not short:
        @pl.when(pl.program_id(1) == 0)
        def _():
            buf_ref[0:SUBLANES, :] = pre_ref[...]

    dots(*blocks[0])
    for i, (lo, hi) in enumerate(blocks):
        if i + 1 < len(blocks):
            dots(*blocks[i + 1])
        _conv_strips(buf_ref, pre_ref, cw_ref, width, tm, lo, hi, short, emit)

    if short:
        for s in range(tm // SHORT_LEN):
            last = SUBLANES + (s + 1) * SHORT_LEN
            tail_ref[s] = buf_ref[last - (width - 1):last, :]
    else:
        tail_ref[...] = buf_ref[tm:, :]
        buf_ref[0:SUBLANES, :] = buf_ref[tm:, :]


def _sc_mixer_kernel(x_ref, pre_ref, nw_ref, win_ref, cw_ref, wout_ref, o_ref, tail_ref,
                     buf_ref, b_ref, g_ref, *, short):
    tm, d = x_ref.shape
    h = _rmsnorm(x_ref[...], nw_ref[...]).astype(BF16)

    def dots(lo, hi):
        buf_ref[SUBLANES:, lo:hi] = _dot(h, win_ref[:, d + lo:d + hi]) * _dot(h, win_ref[:, 2 * d + lo:2 * d + hi])
        b_ref[:, lo:hi] = _dot(h, win_ref[:, lo:hi])

    def emit(r0, cols, y):
        rows = slice(r0, r0 + y.shape[0])
        g_ref[rows, cols] = (b_ref[rows, cols] * y).astype(BF16)

    _pipelined_conv(buf_ref, pre_ref, tail_ref, cw_ref, 3, tm, _col_blocks(d, COL_BLOCK), short, dots, emit)
    o_ref[...] = x_ref[...] + _dot(g_ref[...], wout_ref[...])


def _conv_ffn_kernel(x_ref, pre_ref, nw_ref, wup_ref, wgate_ref, cw_ref, wdown_ref, nf_ref, *rest,
                     short, final_norm, n_prev_tails):
    if n_prev_tails:
        prev_tails_ref, o_ref, tails_ref, buf_ref, g_ref, a_ref = rest
        tails_ref[0:n_prev_tails] = prev_tails_ref[...]
        tail_ref = tails_ref.at[n_prev_tails]
    else:
        o_ref, tail_ref, buf_ref, g_ref, a_ref = rest
    tm = x_ref.shape[0]
    f = wup_ref.shape[1]
    h = _rmsnorm(x_ref[...], nw_ref[...]).astype(BF16)

    def dots(lo, hi):
        buf_ref[SUBLANES:, lo:hi] = _dot(h, wup_ref[:, lo:hi])
        g_ref[:, lo:hi] = _dot(h, wgate_ref[:, lo:hi])

    def emit(r0, cols, y):
        rows = slice(r0, r0 + y.shape[0])
        a_ref[rows, cols] = (_silu(y) * g_ref[rows, cols]).astype(BF16)

    _pipelined_conv(buf_ref, pre_ref, tail_ref, cw_ref, 3, tm, _col_blocks(f, FFN_COL_BLOCK), short, dots, emit)
    y = x_ref[...] + _dot(a_ref[...], wdown_ref[...])
    if final_norm:
        y = _rmsnorm(y, nf_ref[...])
    o_ref[...] = y


def _head_cols(v, h0, lo):
    return jnp.where(lo, v[:, h0:h0 + 1], v[:, h0 + 1:h0 + 2])


def _ssm_in_kernel(x_ref, pre_ref, nw_ref, w_ref, wdt_ref, cw_ref, cb_ref, dtb_ref, alog_ref,
                   z_ref, xbc_ref, *rest, short):
    tm = x_ref.shape[0]
    h = _rmsnorm(x_ref[...], nw_ref[...]).astype(BF16)
    cum_ref, csp_ref, we_ref, da_ref, tail_ref, buf_ref = rest
    if not short:
        @pl.when(pl.program_id(1) == 0)
        def _():
            buf_ref[0:SUBLANES, :] = pre_ref[...]

    def dot_xbc(i):
        cols = slice(i * COL_BLOCK, (i + 1) * COL_BLOCK)
        buf_ref[SUBLANES:, cols] = _dot(h, w_ref[:, D_INNER + i * COL_BLOCK:D_INNER + (i + 1) * COL_BLOCK])

    def dot_z(i):
        cols = slice(i * COL_BLOCK, (i + 1) * COL_BLOCK)
        z_ref[:, cols] = _dot(h, w_ref[:, cols]).astype(BF16)

    def emit(r0, cols, y):
        xbc_ref[r0:r0 + y.shape[0], cols] = _silu(y + cb_ref[:, cols]).astype(BF16)

    def conv_block(i):
        _conv_strips(buf_ref, pre_ref, cw_ref, 4, tm, i * COL_BLOCK, (i + 1) * COL_BLOCK, short, emit)

    n_x, n_z = CONV_DIM // COL_BLOCK, D_INNER // COL_BLOCK
    dot_xbc(0)
    for i in range(n_x):
        if i + 1 < n_x:
            dot_xbc(i + 1)
        if i < n_z:
            dot_z(i)
        conv_block(i)
    for i in range(n_x, n_z):
        dot_z(i)
    dt = _softplus(_dot(h, wdt_ref[...]) + dtb_ref[...])

    if short:
        for s in range(tm // SHORT_LEN):
            tail_ref[s] = buf_ref[SUBLANES + (s + 1) * SHORT_LEN - 3:SUBLANES + (s + 1) * SHORT_LEN, :]
        _decay_terms_short(dt, alog_ref, cum_ref, csp_ref, we_ref, da_ref)
    else:
        tail_ref[...] = buf_ref[tm:, :]
        buf_ref[0:SUBLANES, :] = buf_ref[tm:, :]
        _decay_terms(dt, alog_ref, cum_ref, csp_ref, we_ref, da_ref)


def _decay_terms(dt, alog_ref, cum_ref, cspt_ref, we_ref, da_ref):
    tm = dt.shape[0]
    a = -jnp.exp(alog_ref[...])
    li = lax.broadcasted_iota(jnp.int32, (CHUNK, CHUNK), 0)
    si = lax.broadcasted_iota(jnp.int32, (CHUNK, CHUNK), 1)
    tri = (si <= li).astype(BF16)
    lo8 = lax.broadcasted_iota(jnp.int32, (SUBLANES, LANES), 1) < HEAD_DIM
    pad = CHUNK - tm if tm < CHUNK else 0
    for c in range(max(tm // CHUNK, 1)):
        if pad:
            dt_c = jnp.concatenate([jnp.zeros((pad, LANES), F32), dt], axis=0)
        else:
            dt_c = dt[c * CHUNK:(c + 1) * CHUNK]
        rows = slice(c * CHUNK, (c + 1) * CHUNK)
        cum = _dot_exact_lhs(tri, dt_c * a)
        end = cum[CHUNK - 1:CHUNK, :]
        cum_ref[rows, :] = cum
        cspt_ref[:, rows] = (cum - jnp.log(dt_c)).T
        we_ref[rows, 0:LANES] = (jnp.exp(end - cum) * dt_c).astype(BF16)
        we_ref[rows, LANES:2 * LANES] = jnp.exp(cum).astype(BF16)
        e_end = jnp.broadcast_to(jnp.exp(end), (SUBLANES, LANES))
        for pr in range(N_HEADS // 2):
            da_ref[c * SUBLANES:(c + 1) * SUBLANES, pr * LANES:(pr + 1) * LANES] = _head_cols(e_end, 2 * pr, lo8)


def _decay_terms_short(dt, alog_ref, cum_ref, csp_ref, we_ref, da_ref):
    tm = dt.shape[0]
    dta = dt * -jnp.exp(alog_ref[...])
    li = lax.broadcasted_iota(jnp.int32, (tm, tm), 0)
    si = lax.broadcasted_iota(jnp.int32, (tm, tm), 1)
    same_seq = (li // SHORT_LEN) == (si // SHORT_LEN)
    cum = _dot_exact_lhs((same_seq & (si <= li)).astype(BF16), dta)
    end = _dot_exact_lhs(same_seq.astype(BF16), dta)
    cum_ref[...] = cum
    csp_ref[...] = cum - jnp.log(dt)
    we_ref[:, 0:LANES] = (jnp.exp(end - cum) * dt).astype(BF16)
    we_ref[:, LANES:2 * LANES] = jnp.exp(cum).astype(BF16)
    n_seq = tm // SHORT_LEN
    seq_of_tok = lax.broadcasted_iota(jnp.int32, (n_seq, tm), 1) // SHORT_LEN
    member = (seq_of_tok == lax.broadcasted_iota(jnp.int32, (n_seq, tm), 0)).astype(BF16)
    da_ref[...] = jnp.exp(_dot_exact_lhs(member, dta))


def _ssd_long_kernel(x_ref, z_ref, xbc_ref, cum_ref, cspt_ref, we_ref, da_ref, s0_ref, e_ref,
                     dskip_ref, nw_ref, wout_ref, o_ref, sout_ref, st_ref):
    q = CHUNK

    @pl.when(pl.program_id(1) == 0)
    def _():
        st_ref[...] = s0_ref[...].T

    li = lax.broadcasted_iota(jnp.int32, (q, q), 0)
    si = lax.broadcasted_iota(jnp.int32, (q, q), 1)
    causal = si <= li
    lane = lax.broadcasted_iota(jnp.int32, (q, LANES), 1)
    m_lo = (lane < HEAD_DIM).astype(BF16)
    m_hi = (lane >= HEAD_DIM).astype(BF16)

    normed_chunks = []
    for c in range(x_ref.shape[0] // q):
        rows = slice(c * q, (c + 1) * q)
        cum = cum_ref[rows, :]
        cspt = cspt_ref[:, rows]
        w_end = we_ref[rows, 0:LANES]
        e_cum = we_ref[rows, LANES:2 * LANES]
        normed = []
        for g in range(N_GROUPS):
            gl = slice(g * GROUP_W, (g + 1) * GROUP_W)
            bg = xbc_ref[rows, D_INNER + g * D_STATE:D_INNER + (g + 1) * D_STATE]
            cg = xbc_ref[rows, D_INNER + BC_W + g * D_STATE:D_INNER + BC_W + (g + 1) * D_STATE]
            xs_b = xbc_ref[rows, gl]
            xs = xs_b.astype(F32)
            st_g = st_ref[:, gl]
            cb = _dot_nt(cg, bg)
            y_off = _dot(cg, st_g.astype(BF16))
            w_exp = _dot(w_end, e_ref[:, gl])
            e_exp = _dot(e_cum, e_ref[:, gl])
            xw = xs_b * w_exp.astype(BF16)
            bg_t = bg.astype(F32).T.astype(BF16)
            st_ref[:, gl] = st_g * da_ref[c * SUBLANES:c * SUBLANES + 1, gl] + _dot(bg_t, xw)

            ys = []
            for pr in range(HEADS_PER_GROUP // 2):
                h0 = g * HEADS_PER_GROUP + 2 * pr
                ms = []
                for h in (h0, h0 + 1):
                    seg = cum[:, h:h + 1] - cspt[h:h + 1, :]
                    ms.append((cb * jnp.exp(jnp.where(causal, seg, NEG_BIG))).astype(BF16))
                xp = xs_b[:, pr * LANES:(pr + 1) * LANES]
                rhs = jnp.concatenate([xp * m_lo, xp * m_hi], axis=0)
                ys.append(_dot(jnp.concatenate(ms, axis=1), rhs))
            y = jnp.concatenate(ys, axis=1) + y_off * e_exp + dskip_ref[:, gl] * xs
            y = y * _silu(z_ref[rows, gl].astype(F32))
            y = y * lax.rsqrt(jnp.mean(y * y, axis=-1, keepdims=True) + EPS)
            normed.append((y * nw_ref[:, gl]).astype(BF16))
        normed_chunks.append(jnp.concatenate(normed, axis=1))

    o_ref[...] = x_ref[...] + _dot(jnp.concatenate(normed_chunks, axis=0), wout_ref[...])

    @pl.when(pl.program_id(1) == pl.num_programs(1) - 1)
    def _():
        sout_ref[...] = st_ref[...].T


def _ssd_short_kernel(x_ref, z_ref, xbc_ref, cum_ref, csp_ref, we_ref, da_ref, s0_ref, e_ref,
                      dskip_ref, nw_ref, wout_ref, o_ref, sout_ref):
    q = x_ref.shape[0]
    n_seq = q // SHORT_LEN
    cum = cum_ref[...]
    csp_t = csp_ref[...].T
    li = lax.broadcasted_iota(jnp.int32, (q, q), 0)
    si = lax.broadcasted_iota(jnp.int32, (q, q), 1)
    causal = ((li // SHORT_LEN) == (si // SHORT_LEN)) & (si <= li)
    lane = lax.broadcasted_iota(jnp.int32, (q, LANES), 1)
    m_lo = (lane < HEAD_DIM).astype(BF16)
    m_hi = (lane >= HEAD_DIM).astype(BF16)
    row_seq = lax.broadcasted_iota(jnp.int32, (q, LANES), 0) // SHORT_LEN
    row_seq_k = lax.broadcasted_iota(jnp.int32, (LANES, D_STATE), 0) // SHORT_LEN
    w_end = we_ref[:, 0:LANES]
    e_cum = we_ref[:, LANES:2 * LANES]

    normed = []
    for g in range(N_GROUPS):
        gl = slice(g * GROUP_W, (g + 1) * GROUP_W)
        bg = xbc_ref[:, D_INNER + g * D_STATE:D_INNER + (g + 1) * D_STATE]
        cg = xbc_ref[:, D_INNER + BC_W + g * D_STATE:D_INNER + BC_W + (g + 1) * D_STATE]
        xs_b = xbc_ref[:, gl]
        xs = xs_b.astype(F32)
        cb = _dot_nt(cg, bg)
        w_exp = _dot(w_end, e_ref[:, gl])
        e_exp = _dot(e_cum, e_ref[:, gl])

        cg32 = cg.astype(F32)
        y_off = jnp.zeros((q, GROUP_W), F32)
        for s in range(n_seq):
            cmask = jnp.where(row_seq == s, cg32, 0.0).astype(BF16)
            y_off = y_off + _dot_nt(cmask, s0_ref[s, gl, :].astype(BF16))

        ys = []
        for pr in range(HEADS_PER_GROUP // 2):
            h0 = g * HEADS_PER_GROUP + 2 * pr
            ms = []
            for h in (h0, h0 + 1):
                seg = cum[:, h:h + 1] - csp_t[h:h + 1, :]
                ms.append((cb * jnp.exp(jnp.where(causal, seg, NEG_BIG))).astype(BF16))
            xp = xs_b[:, pr * LANES:(pr + 1) * LANES]
            rhs = jnp.concatenate([xp * m_lo, xp * m_hi], axis=0)
            ys.append(_dot(jnp.concatenate(ms, axis=1), rhs))
        y = jnp.concatenate(ys, axis=1) + y_off * e_exp + dskip_ref[:, gl] * xs
        y = y * _silu(z_ref[:, gl].astype(F32))
        y = y * lax.rsqrt(jnp.mean(y * y, axis=-1, keepdims=True) + EPS)
        normed.append((y * nw_ref[:, gl]).astype(BF16))

        xw = (xs_b * w_exp.astype(BF16)).astype(F32)
        xw_t = jnp.concatenate([xw, jnp.zeros((LANES - q, GROUP_W), F32)], axis=0).T.astype(BF16)
        bg_k = jnp.concatenate([bg.astype(F32), jnp.zeros((LANES - q, D_STATE), F32)], axis=0)
        for s in range(n_seq):
            bmask = jnp.where(row_seq_k == s, bg_k, 0.0).astype(BF16)
            upd = _dot(xw_t, bmask)
            for hh in range(HEADS_PER_GROUP):
                h = g * HEADS_PER_GROUP + hh
                rows = slice(h * HEAD_DIM, (h + 1) * HEAD_DIM)
                sout_ref[s, rows, :] = (s0_ref[s, rows, :] * da_ref[s:s + 1, h:h + 1]
                                        + upd[hh * HEAD_DIM:(hh + 1) * HEAD_DIM])

    o_ref[...] = x_ref[...] + _dot(jnp.concatenate(normed, axis=1), wout_ref[...])


def _const_spec(shape):
    return pl.BlockSpec(shape, lambda *_: (0,) * len(shape), pipeline_mode=pl.Buffered(1))


def _layer_spec(arr, layer):
    return pl.BlockSpec((None,) + arr.shape[1:], lambda *_: (layer, 0, 0), pipeline_mode=pl.Buffered(1))


def _token_layout(n_tok, n_seq, tm, short, shared_pre):
    if short:
        grid = (n_tok // tm,)
        tok = lambda i: (i, 0)

        def carried(c, width, layer):
            block = (tm // SHORT_LEN, width - 1, c)
            pre_spec = pl.BlockSpec((None,) + block, lambda i: (layer, i, 0, 0))
            tail_shape = jax.ShapeDtypeStruct((n_tok // SHORT_LEN, width - 1, c), F32)
            return pre_spec, pl.BlockSpec(block, lambda i: (i, 0, 0)), tail_shape

        return grid, tok, ("arbitrary",), carried
    tiles = n_tok // n_seq // tm
    grid = (n_seq, tiles)
    tok = lambda b, j: (b * tiles + j, 0)
    per_seq = lambda b, j: (b, 0)
    pre_map = (lambda b, j: (0, 0)) if shared_pre else per_seq

    def carried(c, width, layer):
        return (pl.BlockSpec((SUBLANES, c), pre_map), pl.BlockSpec((SUBLANES, c), per_seq),
                jax.ShapeDtypeStruct((n_seq * SUBLANES, c), F32))

    return grid, tok, ("arbitrary", "arbitrary"), carried


def _params(sem):
    return pltpu.CompilerParams(dimension_semantics=sem, vmem_limit_bytes=VMEM_LIMIT)


def _sc_mixer(x, pre, w, layer, *, n_seq, tm, short, shared_pre):
    n_tok, d = x.shape
    grid, tok, sem, carried = _token_layout(n_tok, n_seq, tm, short, shared_pre)
    pre_spec, tail_spec, tail_shape = carried(d, 3, layer)
    return pl.pallas_call(
        functools.partial(_sc_mixer_kernel, short=short),
        grid=grid,
        in_specs=[pl.BlockSpec((tm, d), tok), pre_spec, _layer_spec(w["norm_mix"], 2 * layer),
                  _layer_spec(w["sc_in"], layer), _layer_spec(w["sc_cw"], layer), _layer_spec(w["sc_out"], layer)],
        out_specs=[pl.BlockSpec((tm, d), tok), tail_spec],
        out_shape=[jax.ShapeDtypeStruct((n_tok, d), F32), tail_shape],
        scratch_shapes=[pltpu.VMEM((tm + SUBLANES, d), F32), pltpu.VMEM((tm, d), F32), pltpu.VMEM((tm, d), BF16)],
        compiler_params=_params(sem),
        name="sc_mixer_short" if short else "sc_mixer_long",
    )(x, pre, w["norm_mix"], w["sc_in"], w["sc_cw"], w["sc_out"])


def _conv_ffn(x, pre, w, layer, *, n_seq, tm, short, shared_pre, final_norm, prev_tails=None):
    n_tok, d = x.shape
    f = w["up"].shape[2]
    grid, tok, sem, carried = _token_layout(n_tok, n_seq, tm, short, shared_pre)
    pre_spec, tail_spec, tail_shape = carried(f, 3, layer)
    in_specs = [pl.BlockSpec((tm, d), tok), pre_spec, _layer_spec(w["norm_ffn"], layer),
                _layer_spec(w["up"], layer), _layer_spec(w["gate"], layer), _layer_spec(w["ffn_cw"], layer),
                _layer_spec(w["down"], layer), _const_spec(w["nfin"].shape)]
    operands = [x, pre, w["norm_ffn"], w["up"], w["gate"], w["ffn_cw"], w["down"], w["nfin"]]
    n_prev = 0 if prev_tails is None else prev_tails.shape[0]
    if n_prev:
        s_tile = tm // SHORT_LEN
        in_specs.append(pl.BlockSpec((n_prev, s_tile, 2, f), lambda i: (0, i, 0, 0)))
        operands.append(prev_tails)
        tail_spec = pl.BlockSpec((n_prev + 1, s_tile, 2, f), lambda i: (0, i, 0, 0))
        tail_shape = jax.ShapeDtypeStruct((n_prev + 1,) + prev_tails.shape[1:], F32)
    return pl.pallas_call(
        functools.partial(_conv_ffn_kernel, short=short, final_norm=final_norm, n_prev_tails=n_prev),
        grid=grid,
        in_specs=in_specs,
        out_specs=[pl.BlockSpec((tm, d), tok), tail_spec],
        out_shape=[jax.ShapeDtypeStruct((n_tok, d), F32), tail_shape],
        scratch_shapes=[pltpu.VMEM((tm + SUBLANES, f), F32), pltpu.VMEM((tm, f), F32), pltpu.VMEM((tm, f), BF16)],
        compiler_params=_params(sem),
        name="conv_ffn_short" if short else "conv_ffn_long",
    )(*operands)


def _ssm_in(x, pre, w, layer, *, n_seq, tm, short, shared_pre):
    n_tok, d = x.shape
    params = [w["norm_mix"], w["ssm_in"], w["w_dt"], w["ssm_cw"], w["ssm_cb"], w["dtb"], w["alog"]]
    param_specs = [_layer_spec(w["norm_mix"], 2 * layer + 1), _layer_spec(w["ssm_in"], layer),
                   _const_spec(w["w_dt"].shape), _layer_spec(w["ssm_cw"], layer),
                   _const_spec(w["ssm_cb"].shape), _const_spec(w["dtb"].shape), _const_spec(w["alog"].shape)]
    out_shape = [jax.ShapeDtypeStruct((n_tok, D_INNER), BF16), jax.ShapeDtypeStruct((n_tok, CONV_DIM), BF16)]
    if short:
        grid, tok, sem, carried = _token_layout(n_tok, n_seq, tm, True, shared_pre)
        pre_spec, tail_spec, tail_shape = carried(CONV_DIM, 4, layer)
        in_specs = [pl.BlockSpec((tm, d), tok), pre_spec] + param_specs
        out_specs = [pl.BlockSpec((tm, D_INNER), tok), pl.BlockSpec((tm, CONV_DIM), tok),
                     pl.BlockSpec((tm, LANES), tok), pl.BlockSpec((tm, LANES), tok),
                     pl.BlockSpec((tm, 2 * LANES), tok), pl.BlockSpec((tm // SHORT_LEN, LANES), tok), tail_spec]
        out_shape += [jax.ShapeDtypeStruct((n_tok, LANES), F32), jax.ShapeDtypeStruct((n_tok, LANES), F32),
                      jax.ShapeDtypeStruct((n_tok, 2 * LANES), BF16),
                      jax.ShapeDtypeStruct((n_tok // SHORT_LEN, LANES), F32), tail_shape]
    else:
        grid, tok, sem, carried = _token_layout(n_tok, n_seq, tm, False, shared_pre)
        pre_spec, tail_spec, tail_shape = carried(CONV_DIM, 4, layer)
        tiles = n_tok // n_seq // tm
        tok_t = lambda b, j: (0, b * tiles + j)
        ctm = max(tm, CHUNK)
        da_rows = ctm // CHUNK * SUBLANES
        n_rows = n_tok // tm * ctm
        in_specs = [pl.BlockSpec((tm, d), tok), pre_spec] + param_specs
        out_specs = [pl.BlockSpec((tm, D_INNER), tok), pl.BlockSpec((tm, CONV_DIM), tok),
                     pl.BlockSpec((ctm, LANES), tok), pl.BlockSpec((LANES, ctm), tok_t),
                     pl.BlockSpec((ctm, 2 * LANES), tok), pl.BlockSpec((da_rows, D_INNER), tok), tail_spec]
        out_shape += [jax.ShapeDtypeStruct((n_rows, LANES), F32), jax.ShapeDtypeStruct((LANES, n_rows), F32),
                      jax.ShapeDtypeStruct((n_rows, 2 * LANES), BF16),
                      jax.ShapeDtypeStruct((n_rows // CHUNK * SUBLANES, D_INNER), F32), tail_shape]
    return pl.pallas_call(
        functools.partial(_ssm_in_kernel, short=short),
        grid=grid,
        in_specs=in_specs,
        out_specs=out_specs,
        out_shape=out_shape,
        scratch_shapes=[pltpu.VMEM((tm + SUBLANES, CONV_DIM), F32)],
        compiler_params=_params(sem),
        name="ssm_in_short" if short else "ssm_in_long",
    )(x, pre, *params)


def _ssd_long(x, z, xbc, cum, cspt, we, da, s0, w, layer, *, n_seq, tm, shared_s0):
    expand, dskip, nw, w_out = w["expand"], w["dskip"], w["ssm_nw"], w["ssm_out"]
    n_tok, d = x.shape
    tiles = n_tok // n_seq // tm
    tok = lambda b, c: (b * tiles + c, 0)
    tok_t = lambda b, c: (0, b * tiles + c)
    s_spec = pl.BlockSpec((D_INNER, D_STATE), lambda b, c: (b, 0))
    s0_spec = pl.BlockSpec((D_INNER, D_STATE), lambda b, c: (0, 0)) if shared_s0 else s_spec
    return pl.pallas_call(
        _ssd_long_kernel,
        grid=(n_seq, tiles),
        in_specs=[pl.BlockSpec((tm, d), tok), pl.BlockSpec((tm, D_INNER), tok),
                  pl.BlockSpec((tm, CONV_DIM), tok), pl.BlockSpec((tm, LANES), tok),
                  pl.BlockSpec((LANES, tm), tok_t), pl.BlockSpec((tm, 2 * LANES), tok),
                  pl.BlockSpec((tm // CHUNK * SUBLANES, D_INNER), tok), s0_spec,
                  _const_spec(expand.shape), _const_spec(dskip.shape), _const_spec(nw.shape), _layer_spec(w_out, layer)],
        out_specs=[pl.BlockSpec((tm, d), tok), s_spec],
        out_shape=[jax.ShapeDtypeStruct((n_tok, d), F32), jax.ShapeDtypeStruct((n_seq * D_INNER, D_STATE), F32)],
        scratch_shapes=[pltpu.VMEM((D_STATE, D_INNER), F32)],
        compiler_params=_params(("arbitrary", "arbitrary")),
        name="ssd_long",
    )(x, z, xbc, cum, cspt, we, da, s0, expand, dskip, nw, w_out)


def _ssd_short(x, z, xbc, cum, csp, we, da, s0, w, layer, *, tm):
    expand, dskip, nw, w_out = w["expand"], w["dskip"], w["ssm_nw"], w["ssm_out"]
    n_tok, d = x.shape
    n_seq = s0.shape[0]
    tok = lambda i: (i, 0)
    s_spec = pl.BlockSpec((tm // SHORT_LEN, D_INNER, D_STATE), lambda i: (i, 0, 0))
    return pl.pallas_call(
        _ssd_short_kernel,
        grid=(n_tok // tm,),
        in_specs=[pl.BlockSpec((tm, d), tok), pl.BlockSpec((tm, D_INNER), tok), pl.BlockSpec((tm, CONV_DIM), tok),
                  pl.BlockSpec((tm, LANES), tok), pl.BlockSpec((tm, LANES), tok), pl.BlockSpec((tm, 2 * LANES), tok),
                  pl.BlockSpec((tm // SHORT_LEN, LANES), tok), s_spec,
                  _const_spec(expand.shape), _const_spec(dskip.shape), _const_spec(nw.shape), _layer_spec(w_out, layer)],
        out_specs=[pl.BlockSpec((tm, d), tok), s_spec],
        out_shape=[jax.ShapeDtypeStruct((n_tok, d), F32), jax.ShapeDtypeStruct((n_seq, D_INNER, D_STATE), F32)],
        compiler_params=_params(("arbitrary",)),
        name="ssd_short",
    )(x, z, xbc, cum, csp, we, da, s0, expand, dskip, nw, w_out)


def _trunk(x, pre_sc, pre_xbc, s0, pre_ffn, w, *, n_seq, short, tms, shared=False):
    kw = dict(n_seq=n_seq, short=short, shared_pre=shared)
    x1, t_sc = _sc_mixer(x, pre_sc, w, 0, tm=tms[0], **kw)
    x2, t_f0 = _conv_ffn(x1, pre_ffn[0], w, 0, tm=tms[1], final_norm=False, **kw)
    ssm = _ssm_in(x2, pre_xbc, w, 0, tm=tms[2], **kw)
    z, xbc, cum, cspt, we, da, t_xbc = ssm
    if short:
        x3, s_new = _ssd_short(x2, z, xbc, cum, cspt, we, da, s0, w, 0, tm=tms[3])
    else:
        pad_front = cum.shape[0] - x2.shape[0]
        padf = lambda t: jnp.pad(t, ((pad_front, 0), (0, 0))) if pad_front else t
        x3, s_new = _ssd_long(padf(x2), padf(z), padf(xbc), cum, cspt, we, da, s0, w, 0, n_seq=n_seq,
                              tm=max(tms[3], CHUNK), shared_s0=shared)
        x3 = x3[pad_front:]
    if short:
        y, t_ffn = _conv_ffn(x3, pre_ffn[1], w, 1, tm=tms[4], final_norm=True, prev_tails=t_f0[None], **kw)
    else:
        y, t_f1 = _conv_ffn(x3, pre_ffn[1], w, 1, tm=tms[4], final_norm=True, **kw)
        t_ffn = (t_f0, t_f1)
    return y, t_sc, t_xbc, s_new, t_ffn


def kernel(x_prompt, x_sample, cache_sc, cache_ssm_conv, state_ssm, cache_ffn_conv, meta_tokens, norm_mix, norm_ffn, norm_final, sc_w_in, sc_conv_w, sc_w_out, ssm_w_in, ssm_conv_w, ssm_conv_b, ssm_dt_bias, ssm_a_log, ssm_d, ssm_norm_w, ssm_w_out, ffn_w_up, ffn_w_gate, ffn_conv_w, ffn_w_down):
    b, seq, d = x_prompt.shape
    n_dec, dec_len, _ = x_sample.shape
    d_ff = ffn_w_up.shape[2]
    assert dec_len == SHORT_LEN and seq % CHUNK == 0 and N_META % SUBLANES == 0 and N_META <= CHUNK

    pad_heads = lambda v: jnp.pad(v.reshape(1, -1).astype(F32), ((0, 0), (0, LANES - N_HEADS)))
    head_of_lane = jnp.arange(D_INNER, dtype=jnp.int32)[None, :] // HEAD_DIM
    w = dict(
        norm_mix=norm_mix.reshape(-1, 1, d), norm_ffn=norm_ffn.reshape(-1, 1, d), nfin=norm_final.reshape(1, d),
        sc_in=sc_w_in.astype(BF16), sc_cw=sc_conv_w, sc_out=sc_w_out.astype(BF16),
        ssm_in=ssm_w_in.astype(BF16),
        w_dt=jnp.pad(ssm_w_in[0][:, D_INNER + CONV_DIM:], ((0, 0), (0, LANES - N_HEADS))).astype(BF16),
        ssm_cw=ssm_conv_w, ssm_cb=ssm_conv_b[0:1], dtb=pad_heads(ssm_dt_bias[0]), alog=pad_heads(ssm_a_log[0]),
        dskip=jnp.repeat(ssm_d[0], HEAD_DIM).reshape(1, -1), ssm_nw=ssm_norm_w[0:1], ssm_out=ssm_w_out.astype(BF16),
        expand=(jnp.arange(LANES, dtype=jnp.int32)[:, None] == head_of_lane).astype(BF16),
        up=ffn_w_up.astype(BF16), gate=ffn_w_gate.astype(BF16), ffn_cw=ffn_conv_w, down=ffn_w_down.astype(BF16),
    )

    zeros8 = lambda c: jnp.zeros((SUBLANES, c), F32)
    _, m_sc, m_xbc, m_state, m_ffn = _trunk(
        meta_tokens.astype(F32), zeros8(d), zeros8(CONV_DIM), jnp.zeros((D_INNER, D_STATE), F32),
        (zeros8(d_ff), zeros8(d_ff)), w, n_seq=1, short=False, tms=(N_META,) * 5)

    yp, p_sc, p_xbc, p_state, p_ffn = _trunk(
        x_prompt.reshape(b * seq, d), m_sc, m_xbc, m_state, m_ffn, w,
        n_seq=b, short=False, shared=True, tms=(1024, 512, 512, 512, 512))
    tail = lambda t, k: t.reshape(b, SUBLANES, -1)[:, SUBLANES - k:]
    out_prompt = (
        yp.reshape(b, seq, d),
        tail(p_sc, 2)[None], tail(p_xbc, 3)[None],
        p_state.reshape(1, b, N_HEADS, HEAD_DIM, D_STATE),
        jnp.stack([tail(p_ffn[0], 2), tail(p_ffn[1], 2)]),
    )

    ys, s_sc, s_xbc, s_state, s_ffn = _trunk(
        x_sample.reshape(n_dec * dec_len, d), cache_sc, cache_ssm_conv,
        state_ssm[0].reshape(n_dec, D_INNER, D_STATE), (cache_ffn_conv, cache_ffn_conv), w,
        n_seq=n_dec, short=True, tms=(512, 512, 512, 64, 512))
    out_sample = (
        ys.reshape(n_dec, dec_len, d),
        s_sc[None], s_xbc[None],
        s_state.reshape(1, n_dec, N_HEADS, HEAD_DIM, D_STATE),
        s_ffn,
    )
    return (out_prompt[0], out_sample[0]) + out_prompt[1:] + out_sample[1:]
```

```python
import functools

import jax
import jax.numpy as jnp
from jax import lax
from jax.experimental import pallas as pl
from jax.experimental.pallas import tpu as pltpu

F32 = jnp.float32
BF16 = jnp.bfloat16

EPS = 1e-5
N_META = 16
HEAD_DIM = 64
N_HEADS = 32
N_GROUPS = 4
HEADS_PER_GROUP = N_HEADS // N_GROUPS
D_STATE = 128
D_INNER = N_HEADS * HEAD_DIM
GROUP_W = D_INNER // N_GROUPS
BC_W = N_GROUPS * D_STATE
CONV_DIM = D_INNER + 2 * BC_W
CHUNK = 128
SHORT_LEN = 8
SUBLANES = 8
LANES = 128
STRIP = 64
COL_BLOCK = 512
FFN_COL_BLOCK = 768
NEG_BIG = -1e30
NEG_LOG2E = -1.4426950408889634
VMEM_LIMIT = 56 * 1024 * 1024


def _rmsnorm(x, w):
    r = lax.rsqrt(jnp.mean(x * x, axis=-1, keepdims=True) + EPS)
    return x * r * w


def _silu(x):
    return x / (1.0 + jnp.exp2(x * NEG_LOG2E))


def _softplus(x):
    return jnp.maximum(x, 0.0) + jnp.log1p(jnp.exp(-jnp.abs(x)))


def _dot(a, b):
    return jnp.dot(a, b, preferred_element_type=F32)


def _dot_nt(a, b):
    return lax.dot_general(a, b, (((1,), (1,)), ((), ())), preferred_element_type=F32)


def _dot_exact_lhs(m_bf16, x):
    hi = x.astype(BF16)
    r1 = x - hi.astype(F32)
    mid = r1.astype(BF16)
    lo = (r1 - mid.astype(F32)).astype(BF16)
    return _dot(m_bf16, hi) + _dot(m_bf16, mid) + _dot(m_bf16, lo)


def _conv_strips(buf_ref, pre_ref, w_ref, width, tm, col_lo, col_hi, short, emit):
    rows = min(STRIP, tm)
    row_in_seq = lax.broadcasted_iota(jnp.int32, (rows, LANES), 0) & (SHORT_LEN - 1)
    row8 = lax.broadcasted_iota(jnp.int32, (SHORT_LEN, LANES), 0)
    for c0 in range(col_lo, col_hi, LANES):
        cols = slice(c0, c0 + LANES)
        taps = [w_ref[k:k + 1, cols] for k in range(width)]
        for r0 in range(0, tm, rows):
            if short:
                xv = buf_ref[SUBLANES + r0:SUBLANES + r0 + rows, cols]
                y = xv * taps[width - 1]
                for k in range(1, width):
                    cached = []
                    for s in range(r0 // SHORT_LEN, (r0 + rows) // SHORT_LEN):
                        piece = jnp.broadcast_to(pre_ref[s, width - 2:width - 1, cols], (SHORT_LEN, LANES))
                        for j in range(k - 2, -1, -1):
                            row = jnp.broadcast_to(pre_ref[s, width - 1 - k + j:width - k + j, cols], (SHORT_LEN, LANES))
                            piece = jnp.where(row8 == j, row, piece)
                        cached.append(piece)
                    xk = jnp.where(row_in_seq < k, jnp.concatenate(cached, axis=0), pltpu.roll(xv, k, axis=0))
                    y = y + xk * taps[width - 1 - k]
            else:
                ext = buf_ref[r0:r0 + rows + SUBLANES, cols]
                y = ext[SUBLANES:] * taps[width - 1]
                for k in range(1, width):
                    y = y + pltpu.roll(ext, k, axis=0)[SUBLANES:] * taps[width - 1 - k]
            emit(r0, cols, y)


def _col_blocks(n_cols, block):
    return [(lo, min(lo + block, n_cols)) for lo in range(0, n_cols, block)]


def _pipelined_conv(buf_ref, pre_ref, tail_ref, cw_ref, width, tm, blocks, short, dots, emit, first_of_seq=None):
    if not short:
        @pl.when(pl.program_id(1) == 0 if first_of_seq is None else first_of_seq)
        def _():
            buf_ref[0:SUBLANES, :] = pre_ref[...]

    dots(*blocks[0])
    for i, (lo, hi) in enumerate(blocks):
        if i + 1 < len(blocks):
            dots(*blocks[i + 1])
        _conv_strips(buf_ref, pre_ref, cw_ref, width, tm, lo, hi, short, emit)

    if short:
        for s in range(tm // SHORT_LEN):
            last = SUBLANES + (s + 1) * SHORT_LEN
            tail_ref[s] = buf_ref[last - (width - 1):last, :]
    else:
        tail_ref[...] = buf_ref[tm:tm + SUBLANES, :]
        buf_ref[0:SUBLANES, :] = buf_ref[tm:tm + SUBLANES, :]


def _sc_mixer_kernel(x_ref, pre_ref, nw_ref, win_ref, cw_ref, wout_ref, o_ref, tail_ref,
                     buf_ref, b_ref, g_ref, *, short):
    tm, d = x_ref.shape
    h = _rmsnorm(x_ref[...], nw_ref[...]).astype(BF16)

    def dots(lo, hi):
        buf_ref[SUBLANES:, lo:hi] = _dot(h, win_ref[:, d + lo:d + hi]) * _dot(h, win_ref[:, 2 * d + lo:2 * d + hi])
        b_ref[:, lo:hi] = _dot(h, win_ref[:, lo:hi])

    def emit(r0, cols, y):
        rows = slice(r0, r0 + y.shape[0])
        g_ref[rows, cols] = (b_ref[rows, cols] * y).astype(BF16)

    _pipelined_conv(buf_ref, pre_ref, tail_ref, cw_ref, 3, tm, _col_blocks(d, COL_BLOCK), short, dots, emit)
    o_ref[...] = x_ref[...] + _dot(g_ref[...], wout_ref[...])


def _conv_ffn_both_kernel(xl_ref, prel_ref, xs_ref, pres_ref, nw_ref, wup_ref, wgate_ref, cw_ref, wdown_ref,
                          nf_ref, *rest, n_long_tiles, tiles_per_seq, final_norm, n_prev_tails):
    params = (nw_ref, wup_ref, wgate_ref, cw_ref, wdown_ref, nf_ref)
    prev_tails, rest = rest[:1 if n_prev_tails else 0], rest[1 if n_prev_tails else 0:]
    ol_ref, taill_ref, os_ref, tails_ref, buf_ref, g_ref, a_ref = rest
    step = pl.program_id(0)

    @pl.when(step < n_long_tiles)
    def _():
        _conv_ffn_kernel(xl_ref, prel_ref, *params, ol_ref, taill_ref, buf_ref, g_ref, a_ref, short=False,
                         final_norm=final_norm, n_prev_tails=0, first_of_seq=step % tiles_per_seq == 0)

    @pl.when(step >= n_long_tiles)
    def _():
        _conv_ffn_kernel(xs_ref, pres_ref, *params, *prev_tails, os_ref, tails_ref, buf_ref, g_ref, a_ref,
                         short=True, final_norm=final_norm, n_prev_tails=n_prev_tails)


def _conv_ffn_kernel(x_ref, pre_ref, nw_ref, wup_ref, wgate_ref, cw_ref, wdown_ref, nf_ref, *rest,
                     short, final_norm, n_prev_tails, first_of_seq=None):
    if n_prev_tails:
        prev_tails_ref, o_ref, tails_ref, buf_ref, g_ref, a_ref = rest
        tails_ref[0:n_prev_tails] = prev_tails_ref[...]
        tail_ref = tails_ref.at[n_prev_tails]
    else:
        o_ref, tail_ref, buf_ref, g_ref, a_ref = rest
    tm = x_ref.shape[0]
    f = wup_ref.shape[1]
    h = _rmsnorm(x_ref[...], nw_ref[...]).astype(BF16)

    def dots(lo, hi):
        buf_ref[SUBLANES:SUBLANES + tm, lo:hi] = _dot(h, wup_ref[:, lo:hi])
        g_ref[0:tm, lo:hi] = _dot(h, wgate_ref[:, lo:hi])

    def emit(r0, cols, y):
        rows = slice(r0, r0 + y.shape[0])
        a_ref[rows, cols] = (_silu(y) * g_ref[rows, cols]).astype(BF16)

    _pipelined_conv(buf_ref, pre_ref, tail_ref, cw_ref, 3, tm, _col_blocks(f, FFN_COL_BLOCK), short, dots, emit,
                    first_of_seq)
    y = x_ref[...] + _dot(a_ref[0:tm, :], wdown_ref[...])
    if final_norm:
        y = _rmsnorm(y, nf_ref[...])
    o_ref[...] = y


def _head_cols(v, h0, lo):
    return jnp.where(lo, v[:, h0:h0 + 1], v[:, h0 + 1:h0 + 2])


def _ssm_in_kernel(x_ref, pre_ref, nw_ref, w_ref, wdt_ref, cw_ref, cb_ref, dtb_ref, alog_ref,
                   z_ref, xbc_ref, *rest, short):
    tm = x_ref.shape[0]
    h = _rmsnorm(x_ref[...], nw_ref[...]).astype(BF16)
    cum_ref, csp_ref, we_ref, da_ref, tail_ref, buf_ref = rest
    if not short:
        @pl.when(pl.program_id(1) == 0)
        def _():
            buf_ref[0:SUBLANES, :] = pre_ref[...]

    def dot_xbc(i):
        cols = slice(i * COL_BLOCK, (i + 1) * COL_BLOCK)
        buf_ref[SUBLANES:, cols] = _dot(h, w_ref[:, D_INNER + i * COL_BLOCK:D_INNER + (i + 1) * COL_BLOCK])

    def dot_z(i):
        cols = slice(i * COL_BLOCK, (i + 1) * COL_BLOCK)
        z_ref[:, cols] = _dot(h, w_ref[:, cols]).astype(BF16)

    def emit(r0, cols, y):
        xbc_ref[r0:r0 + y.shape[0], cols] = _silu(y + cb_ref[:, cols]).astype(BF16)

    def conv_block(i):
        _conv_strips(buf_ref, pre_ref, cw_ref, 4, tm, i * COL_BLOCK, (i + 1) * COL_BLOCK, short, emit)

    n_x, n_z = CONV_DIM // COL_BLOCK, D_INNER // COL_BLOCK
    dot_xbc(0)
    for i in range(n_x):
        if i + 1 < n_x:
            dot_xbc(i + 1)
        if i < n_z:
            dot_z(i)
        conv_block(i)
    for i in range(n_x, n_z):
        dot_z(i)
    dt = _softplus(_dot(h, wdt_ref[...]) + dtb_ref[...])

    if short:
        for s in range(tm // SHORT_LEN):
            tail_ref[s] = buf_ref[SUBLANES + (s + 1) * SHORT_LEN - 3:SUBLANES + (s + 1) * SHORT_LEN, :]
        _decay_terms_short(dt, alog_ref, cum_ref, csp_ref, we_ref, da_ref)
    else:
        tail_ref[...] = buf_ref[tm:, :]
        buf_ref[0:SUBLANES, :] = buf_ref[tm:, :]
        _decay_terms(dt, alog_ref, cum_ref, csp_ref, we_ref, da_ref)


def _decay_terms(dt, alog_ref, cum_ref, cspt_ref, we_ref, da_ref):
    tm = dt.shape[0]
    a = -jnp.exp(alog_ref[...])
    li = lax.broadcasted_iota(jnp.int32, (CHUNK, CHUNK), 0)
    si = lax.broadcasted_iota(jnp.int32, (CHUNK, CHUNK), 1)
    tri = (si <= li).astype(BF16)
    lo8 = lax.broadcasted_iota(jnp.int32, (SUBLANES, LANES), 1) < HEAD_DIM
    pad = CHUNK - tm if tm < CHUNK else 0
    for c in range(max(tm // CHUNK, 1)):
        if pad:
            dt_c = jnp.concatenate([jnp.zeros((pad, LANES), F32), dt], axis=0)
        else:
            dt_c = dt[c * CHUNK:(c + 1) * CHUNK]
        rows = slice(c * CHUNK, (c + 1) * CHUNK)
        cum = _dot_exact_lhs(tri, dt_c * a)
        end = cum[CHUNK - 1:CHUNK, :]
        cum_ref[rows, :] = cum
        cspt_ref[:, rows] = (cum - jnp.log(dt_c)).T
        we_ref[rows, 0:LANES] = (jnp.exp(end - cum) * dt_c).astype(BF16)
        we_ref[rows, LANES:2 * LANES] = jnp.exp(cum).astype(BF16)
        e_end = jnp.broadcast_to(jnp.exp(end), (SUBLANES, LANES))
        for pr in range(N_HEADS // 2):
            da_ref[c * SUBLANES:(c + 1) * SUBLANES, pr * LANES:(pr + 1) * LANES] = _head_cols(e_end, 2 * pr, lo8)


def _decay_terms_short(dt, alog_ref, cum_ref, csp_ref, we_ref, da_ref):
    tm = dt.shape[0]
    dta = dt * -jnp.exp(alog_ref[...])
    li = lax.broadcasted_iota(jnp.int32, (tm, tm), 0)
    si = lax.broadcasted_iota(jnp.int32, (tm, tm), 1)
    same_seq = (li // SHORT_LEN) == (si // SHORT_LEN)
    cum = _dot_exact_lhs((same_seq & (si <= li)).astype(BF16), dta)
    end = _dot_exact_lhs(same_seq.astype(BF16), dta)
    cum_ref[...] = cum
    csp_ref[...] = cum - jnp.log(dt)
    we_ref[:, 0:LANES] = (jnp.exp(end - cum) * dt).astype(BF16)
    we_ref[:, LANES:2 * LANES] = jnp.exp(cum).astype(BF16)
    n_seq = tm // SHORT_LEN
    seq_of_tok = lax.broadcasted_iota(jnp.int32, (n_seq, tm), 1) // SHORT_LEN
    member = (seq_of_tok == lax.broadcasted_iota(jnp.int32, (n_seq, tm), 0)).astype(BF16)
    da_ref[...] = jnp.exp(_dot_exact_lhs(member, dta))


def _ssd_long_kernel(x_ref, z_ref, xbc_ref, cum_ref, cspt_ref, we_ref, da_ref, s0_ref, e_ref,
                     dskip_ref, nw_ref, wout_ref, o_ref, sout_ref, st_ref):
    q = CHUNK

    @pl.when(pl.program_id(1) == 0)
    def _():
        st_ref[...] = s0_ref[...].T

    li = lax.broadcasted_iota(jnp.int32, (q, q), 0)
    si = lax.broadcasted_iota(jnp.int32, (q, q), 1)
    causal = si <= li
    lane = lax.broadcasted_iota(jnp.int32, (q, LANES), 1)
    m_lo = (lane < HEAD_DIM).astype(BF16)
    m_hi = (lane >= HEAD_DIM).astype(BF16)

    normed_chunks = []
    for c in range(x_ref.shape[0] // q):
        rows = slice(c * q, (c + 1) * q)
        cum = cum_ref[rows, :]
        cspt = cspt_ref[:, rows]
        w_end = we_ref[rows, 0:LANES]
        e_cum = we_ref[rows, LANES:2 * LANES]
        normed = []
        for g in range(N_GROUPS):
            gl = slice(g * GROUP_W, (g + 1) * GROUP_W)
            bg = xbc_ref[rows, D_INNER + g * D_STATE:D_INNER + (g + 1) * D_STATE]
            cg = xbc_ref[rows, D_INNER + BC_W + g * D_STATE:D_INNER + BC_W + (g + 1) * D_STATE]
            xs_b = xbc_ref[rows, gl]
            xs = xs_b.astype(F32)
            st_g = st_ref[:, gl]
            cb = _dot_nt(cg, bg)
            y_off = _dot(cg, st_g.astype(BF16))
            w_exp = _dot(w_end, e_ref[:, gl])
            e_exp = _dot(e_cum, e_ref[:, gl])
            xw = xs_b * w_exp.astype(BF16)
            bg_t = bg.astype(F32).T.astype(BF16)
            st_ref[:, gl] = st_g * da_ref[c * SUBLANES:c * SUBLANES + 1, gl] + _dot(bg_t, xw)

            ys = []
            for pr in range(HEADS_PER_GROUP // 2):
                h0 = g * HEADS_PER_GROUP + 2 * pr
                ms = []
                for h in (h0, h0 + 1):
                    seg = cum[:, h:h + 1] - cspt[h:h + 1, :]
                    ms.append((cb * jnp.exp(jnp.where(causal, seg, NEG_BIG))).astype(BF16))
                xp = xs_b[:, pr * LANES:(pr + 1) * LANES]
                rhs = jnp.concatenate([xp * m_lo, xp * m_hi], axis=0)
                ys.append(_dot(jnp.concatenate(ms, axis=1), rhs))
            y = jnp.concatenate(ys, axis=1) + y_off * e_exp + dskip_ref[:, gl] * xs
            y = y * _silu(z_ref[rows, gl].astype(F32))
            y = y * lax.rsqrt(jnp.mean(y * y, axis=-1, keepdims=True) + EPS)
            normed.append((y * nw_ref[:, gl]).astype(BF16))
        normed_chunks.append(jnp.concatenate(normed, axis=1))

    o_ref[...] = x_ref[...] + _dot(jnp.concatenate(normed_chunks, axis=0), wout_ref[...])

    @pl.when(pl.program_id(1) == pl.num_programs(1) - 1)
    def _():
        sout_ref[...] = st_ref[...].T


def _ssd_short_kernel(x_ref, z_ref, xbc_ref, cum_ref, csp_ref, we_ref, da_ref, s0_ref, e_ref,
                      dskip_ref, nw_ref, wout_ref, o_ref, sout_ref):
    q = x_ref.shape[0]
    n_seq = q // SHORT_LEN
    cum = cum_ref[...]
    csp_t = csp_ref[...].T
    li = lax.broadcasted_iota(jnp.int32, (q, q), 0)
    si = lax.broadcasted_iota(jnp.int32, (q, q), 1)
    causal = ((li // SHORT_LEN) == (si // SHORT_LEN)) & (si <= li)
    lane = lax.broadcasted_iota(jnp.int32, (q, LANES), 1)
    m_lo = (lane < HEAD_DIM).astype(BF16)
    m_hi = (lane >= HEAD_DIM).astype(BF16)
    row_seq = lax.broadcasted_iota(jnp.int32, (q, LANES), 0) // SHORT_LEN
    row_seq_k = lax.broadcasted_iota(jnp.int32, (LANES, D_STATE), 0) // SHORT_LEN
    w_end = we_ref[:, 0:LANES]
    e_cum = we_ref[:, LANES:2 * LANES]

    normed = []
    for g in range(N_GROUPS):
        gl = slice(g * GROUP_W, (g + 1) * GROUP_W)
        bg = xbc_ref[:, D_INNER + g * D_STATE:D_INNER + (g + 1) * D_STATE]
        cg = xbc_ref[:, D_INNER + BC_W + g * D_STATE:D_INNER + BC_W + (g + 1) * D_STATE]
        xs_b = xbc_ref[:, gl]
        xs = xs_b.astype(F32)
        cb = _dot_nt(cg, bg)
        w_exp = _dot(w_end, e_ref[:, gl])
        e_exp = _dot(e_cum, e_ref[:, gl])

        cg32 = cg.astype(F32)
        y_off = jnp.zeros((q, GROUP_W), F32)
        for s in range(n_seq):
            cmask = jnp.where(row_seq == s, cg32, 0.0).astype(BF16)
            y_off = y_off + _dot_nt(cmask, s0_ref[s, gl, :].astype(BF16))

        ys = []
        for pr in range(HEADS_PER_GROUP // 2):
            h0 = g * HEADS_PER_GROUP + 2 * pr
            ms = []
            for h in (h0, h0 + 1):
                seg = cum[:, h:h + 1] - csp_t[h:h + 1, :]
                ms.append((cb * jnp.exp(jnp.where(causal, seg, NEG_BIG))).astype(BF16))
            xp = xs_b[:, pr * LANES:(pr + 1) * LANES]
            rhs = jnp.concatenate([xp * m_lo, xp * m_hi], axis=0)
            ys.append(_dot(jnp.concatenate(ms, axis=1), rhs))
        y = jnp.concatenate(ys, axis=1) + y_off * e_exp + dskip_ref[:, gl] * xs
        y = y * _silu(z_ref[:, gl].astype(F32))
        y = y * lax.rsqrt(jnp.mean(y * y, axis=-1, keepdims=True) + EPS)
        normed.append((y * nw_ref[:, gl]).astype(BF16))

        xw = (xs_b * w_exp.astype(BF16)).astype(F32)
        xw_t = jnp.concatenate([xw, jnp.zeros((LANES - q, GROUP_W), F32)], axis=0).T.astype(BF16)
        bg_k = jnp.concatenate([bg.astype(F32), jnp.zeros((LANES - q, D_STATE), F32)], axis=0)
        for s in range(n_seq):
            bmask = jnp.where(row_seq_k == s, bg_k, 0.0).astype(BF16)
            upd = _dot(xw_t, bmask)
            for hh in range(HEADS_PER_GROUP):
                h = g * HEADS_PER_GROUP + hh
                rows = slice(h * HEAD_DIM, (h + 1) * HEAD_DIM)
                sout_ref[s, rows, :] = (s0_ref[s, rows, :] * da_ref[s:s + 1, h:h + 1]
                                        + upd[hh * HEAD_DIM:(hh + 1) * HEAD_DIM])

    o_ref[...] = x_ref[...] + _dot(jnp.concatenate(normed, axis=1), wout_ref[...])


def _const_spec(shape):
    return pl.BlockSpec(shape, lambda *_: (0,) * len(shape), pipeline_mode=pl.Buffered(1))


def _layer_spec(arr, layer):
    return pl.BlockSpec((None,) + arr.shape[1:], lambda *_: (layer, 0, 0), pipeline_mode=pl.Buffered(1))


def _token_layout(n_tok, n_seq, tm, short, shared_pre):
    if short:
        grid = (n_tok // tm,)
        tok = lambda i: (i, 0)

        def carried(c, width, layer):
            block = (tm // SHORT_LEN, width - 1, c)
            pre_spec = pl.BlockSpec((None,) + block, lambda i: (layer, i, 0, 0))
            tail_shape = jax.ShapeDtypeStruct((n_tok // SHORT_LEN, width - 1, c), F32)
            return pre_spec, pl.BlockSpec(block, lambda i: (i, 0, 0)), tail_shape

        return grid, tok, ("arbitrary",), carried
    tiles = n_tok // n_seq // tm
    grid = (n_seq, tiles)
    tok = lambda b, j: (b * tiles + j, 0)
    per_seq = lambda b, j: (b, 0)
    pre_map = (lambda b, j: (0, 0)) if shared_pre else per_seq

    def carried(c, width, layer):
        return (pl.BlockSpec((SUBLANES, c), pre_map), pl.BlockSpec((SUBLANES, c), per_seq),
                jax.ShapeDtypeStruct((n_seq * SUBLANES, c), F32))

    return grid, tok, ("arbitrary", "arbitrary"), carried


def _params(sem):
    return pltpu.CompilerParams(dimension_semantics=sem, vmem_limit_bytes=VMEM_LIMIT)


def _sc_mixer(x, pre, w, layer, *, n_seq, tm, short, shared_pre):
    n_tok, d = x.shape
    grid, tok, sem, carried = _token_layout(n_tok, n_seq, tm, short, shared_pre)
    pre_spec, tail_spec, tail_shape = carried(d, 3, layer)
    return pl.pallas_call(
        functools.partial(_sc_mixer_kernel, short=short),
        grid=grid,
        in_specs=[pl.BlockSpec((tm, d), tok), pre_spec, _layer_spec(w["norm_mix"], 2 * layer),
                  _layer_spec(w["sc_in"], layer), _layer_spec(w["sc_cw"], layer), _layer_spec(w["sc_out"], layer)],
        out_specs=[pl.BlockSpec((tm, d), tok), tail_spec],
        out_shape=[jax.ShapeDtypeStruct((n_tok, d), F32), tail_shape],
        scratch_shapes=[pltpu.VMEM((tm + SUBLANES, d), F32), pltpu.VMEM((tm, d), F32), pltpu.VMEM((tm, d), BF16)],
        compiler_params=_params(sem),
        name="sc_mixer_short" if short else "sc_mixer_long",
    )(x, pre, w["norm_mix"], w["sc_in"], w["sc_cw"], w["sc_out"])


def _conv_ffn(x, pre, w, layer, *, n_seq, tm, short, shared_pre, final_norm, prev_tails=None):
    n_tok, d = x.shape
    f = w["up"].shape[2]
    grid, tok, sem, carried = _token_layout(n_tok, n_seq, tm, short, shared_pre)
    pre_spec, tail_spec, tail_shape = carried(f, 3, layer)
    in_specs = [pl.BlockSpec((tm, d), tok), pre_spec, _layer_spec(w["norm_ffn"], layer),
                _layer_spec(w["up"], layer), _layer_spec(w["gate"], layer), _layer_spec(w["ffn_cw"], layer),
                _layer_spec(w["down"], layer), _const_spec(w["nfin"].shape)]
    operands = [x, pre, w["norm_ffn"], w["up"], w["gate"], w["ffn_cw"], w["down"], w["nfin"]]
    n_prev = 0 if prev_tails is None else prev_tails.shape[0]
    if n_prev:
        s_tile = tm // SHORT_LEN
        in_specs.append(pl.BlockSpec((n_prev, s_tile, 2, f), lambda i: (0, i, 0, 0)))
        operands.append(prev_tails)
        tail_spec = pl.BlockSpec((n_prev + 1, s_tile, 2, f), lambda i: (0, i, 0, 0))
        tail_shape = jax.ShapeDtypeStruct((n_prev + 1,) + prev_tails.shape[1:], F32)
    return pl.pallas_call(
        functools.partial(_conv_ffn_kernel, short=short, final_norm=final_norm, n_prev_tails=n_prev),
        grid=grid,
        in_specs=in_specs,
        out_specs=[pl.BlockSpec((tm, d), tok), tail_spec],
        out_shape=[jax.ShapeDtypeStruct((n_tok, d), F32), tail_shape],
        scratch_shapes=[pltpu.VMEM((tm + SUBLANES, f), F32), pltpu.VMEM((tm, f), F32), pltpu.VMEM((tm, f), BF16)],
        compiler_params=_params(sem),
        name="conv_ffn_short" if short else "conv_ffn_long",
    )(*operands)


def _conv_ffn_both(x_long, pre_long, x_short, cache_short, w, layer, *, n_seq_long, tm, tm_short, final_norm,
                   prev_tails=None):
    (n_long, d), n_short = x_long.shape, x_short.shape[0]
    f = w["up"].shape[2]
    assert tm_short <= tm
    long_tiles, short_tiles = n_long // tm, n_short // tm_short
    tiles_per_seq = long_tiles // n_seq_long
    long_tile = lambda i: jnp.minimum(i, long_tiles - 1)
    short_tile = lambda i: jnp.maximum(i - long_tiles, 0)
    s_tile = tm_short // SHORT_LEN
    n_dec = n_short // SHORT_LEN
    in_specs = [pl.BlockSpec((tm, d), lambda i: (long_tile(i), 0)), pl.BlockSpec((SUBLANES, f), lambda i: (0, 0)),
                pl.BlockSpec((tm_short, d), lambda i: (short_tile(i), 0)),
                pl.BlockSpec((None, s_tile, 2, f), lambda i: (layer, short_tile(i), 0, 0)),
                _layer_spec(w["norm_ffn"], layer), _layer_spec(w["up"], layer), _layer_spec(w["gate"], layer),
                _layer_spec(w["ffn_cw"], layer), _layer_spec(w["down"], layer), _const_spec(w["nfin"].shape)]
    operands = [x_long, pre_long, x_short, cache_short, w["norm_ffn"], w["up"], w["gate"], w["ffn_cw"], w["down"], w["nfin"]]
    n_prev = 0 if prev_tails is None else prev_tails.shape[0]
    if n_prev:
        in_specs.append(pl.BlockSpec((n_prev, s_tile, 2, f), lambda i: (0, short_tile(i), 0, 0)))
        operands.append(prev_tails)
        tail_short = (pl.BlockSpec((n_prev + 1, s_tile, 2, f), lambda i: (0, short_tile(i), 0, 0)),
                      jax.ShapeDtypeStruct((n_prev + 1, n_dec, 2, f), F32))
    else:
        tail_short = (pl.BlockSpec((s_tile, 2, f), lambda i: (short_tile(i), 0, 0)),
                      jax.ShapeDtypeStruct((n_dec, 2, f), F32))
    return pl.pallas_call(
        functools.partial(_conv_ffn_both_kernel, n_long_tiles=long_tiles, tiles_per_seq=tiles_per_seq,
                          final_norm=final_norm, n_prev_tails=n_prev),
        grid=(long_tiles + short_tiles,),
        in_specs=in_specs,
        out_specs=[pl.BlockSpec((tm, d), lambda i: (long_tile(i), 0)),
                   pl.BlockSpec((SUBLANES, f), lambda i: (long_tile(i) // tiles_per_seq, 0)),
                   pl.BlockSpec((tm_short, d), lambda i: (short_tile(i), 0)), tail_short[0]],
        out_shape=[jax.ShapeDtypeStruct((n_long, d), F32), jax.ShapeDtypeStruct((n_seq_long * SUBLANES, f), F32),
                   jax.ShapeDtypeStruct((n_short, d), F32), tail_short[1]],
        scratch_shapes=[pltpu.VMEM((tm + SUBLANES, f), F32), pltpu.VMEM((tm, f), F32), pltpu.VMEM((tm, f), BF16)],
        compiler_params=_params(("arbitrary",)),
        name="conv_ffn_both",
    )(*operands)


def _ssm_in(x, pre, w, layer, *, n_seq, tm, short, shared_pre):
    n_tok, d = x.shape
    params = [w["norm_mix"], w["ssm_in"], w["w_dt"], w["ssm_cw"], w["ssm_cb"], w["dtb"], w["alog"]]
    param_specs = [_layer_spec(w["norm_mix"], 2 * layer + 1), _layer_spec(w["ssm_in"], layer),
                   _const_spec(w["w_dt"].shape), _layer_spec(w["ssm_cw"], layer),
                   _const_spec(w["ssm_cb"].shape), _const_spec(w["dtb"].shape), _const_spec(w["alog"].shape)]
    out_shape = [jax.ShapeDtypeStruct((n_tok, D_INNER), BF16), jax.ShapeDtypeStruct((n_tok, CONV_DIM), BF16)]
    if short:
        grid, tok, sem, carried = _token_layout(n_tok, n_seq, tm, True, shared_pre)
        pre_spec, tail_spec, tail_shape = carried(CONV_DIM, 4, layer)
        in_specs = [pl.BlockSpec((tm, d), tok), pre_spec] + param_specs
        out_specs = [pl.BlockSpec((tm, D_INNER), tok), pl.BlockSpec((tm, CONV_DIM), tok),
                     pl.BlockSpec((tm, LANES), tok), pl.BlockSpec((tm, LANES), tok),
                     pl.BlockSpec((tm, 2 * LANES), tok), pl.BlockSpec((tm // SHORT_LEN, LANES), tok), tail_spec]
        out_shape += [jax.ShapeDtypeStruct((n_tok, LANES), F32), jax.ShapeDtypeStruct((n_tok, LANES), F32),
                      jax.ShapeDtypeStruct((n_tok, 2 * LANES), BF16),
                      jax.ShapeDtypeStruct((n_tok // SHORT_LEN, LANES), F32), tail_shape]
    else:
        grid, tok, sem, carried = _token_layout(n_tok, n_seq, tm, False, shared_pre)
        pre_spec, tail_spec, tail_shape = carried(CONV_DIM, 4, layer)
        tiles = n_tok // n_seq // tm
        tok_t = lambda b, j: (0, b * tiles + j)
        ctm = max(tm, CHUNK)
        da_rows = ctm // CHUNK * SUBLANES
        n_rows = n_tok // tm * ctm
        in_specs = [pl.BlockSpec((tm, d), tok), pre_spec] + param_specs
        out_specs = [pl.BlockSpec((tm, D_INNER), tok), pl.BlockSpec((tm, CONV_DIM), tok),
                     pl.BlockSpec((ctm, LANES), tok), pl.BlockSpec((LANES, ctm), tok_t),
                     pl.BlockSpec((ctm, 2 * LANES), tok), pl.BlockSpec((da_rows, D_INNER), tok), tail_spec]
        out_shape += [jax.ShapeDtypeStruct((n_rows, LANES), F32), jax.ShapeDtypeStruct((LANES, n_rows), F32),
                      jax.ShapeDtypeStruct((n_rows, 2 * LANES), BF16),
                      jax.ShapeDtypeStruct((n_rows // CHUNK * SUBLANES, D_INNER), F32), tail_shape]
    return pl.pallas_call(
        functools.partial(_ssm_in_kernel, short=short),
        grid=grid,
        in_specs=in_specs,
        out_specs=out_specs,
        out_shape=out_shape,
        scratch_shapes=[pltpu.VMEM((tm + SUBLANES, CONV_DIM), F32)],
        compiler_params=_params(sem),
        name="ssm_in_short" if short else "ssm_in_long",
    )(x, pre, *params)


def _ssd_long(x, z, xbc, cum, cspt, we, da, s0, w, layer, *, n_seq, tm, shared_s0):
    expand, dskip, nw, w_out = w["expand"], w["dskip"], w["ssm_nw"], w["ssm_out"]
    n_tok, d = x.shape
    tiles = n_tok // n_seq // tm
    tok = lambda b, c: (b * tiles + c, 0)
    tok_t = lambda b, c: (0, b * tiles + c)
    s_spec = pl.BlockSpec((D_INNER, D_STATE), lambda b, c: (b, 0))
    s0_spec = pl.BlockSpec((D_INNER, D_STATE), lambda b, c: (0, 0)) if shared_s0 else s_spec
    return pl.pallas_call(
        _ssd_long_kernel,
        grid=(n_seq, tiles),
        in_specs=[pl.BlockSpec((tm, d), tok), pl.BlockSpec((tm, D_INNER), tok),
                  pl.BlockSpec((tm, CONV_DIM), tok), pl.BlockSpec((tm, LANES), tok),
                  pl.BlockSpec((LANES, tm), tok_t), pl.BlockSpec((tm, 2 * LANES), tok),
                  pl.BlockSpec((tm // CHUNK * SUBLANES, D_INNER), tok), s0_spec,
                  _const_spec(expand.shape), _const_spec(dskip.shape), _const_spec(nw.shape), _layer_spec(w_out, layer)],
        out_specs=[pl.BlockSpec((tm, d), tok), s_spec],
        out_shape=[jax.ShapeDtypeStruct((n_tok, d), F32), jax.ShapeDtypeStruct((n_seq * D_INNER, D_STATE), F32)],
        scratch_shapes=[pltpu.VMEM((D_STATE, D_INNER), F32)],
        compiler_params=_params(("arbitrary", "arbitrary")),
        name="ssd_long",
    )(x, z, xbc, cum, cspt, we, da, s0, expand, dskip, nw, w_out)


def _ssd_short(x, z, xbc, cum, csp, we, da, s0, w, layer, *, tm):
    expand, dskip, nw, w_out = w["expand"], w["dskip"], w["ssm_nw"], w["ssm_out"]
    n_tok, d = x.shape
    n_seq = s0.shape[0]
    tok = lambda i: (i, 0)
    s_spec = pl.BlockSpec((tm // SHORT_LEN, D_INNER, D_STATE), lambda i: (i, 0, 0))
    return pl.pallas_call(
        _ssd_short_kernel,
        grid=(n_tok // tm,),
        in_specs=[pl.BlockSpec((tm, d), tok), pl.BlockSpec((tm, D_INNER), tok), pl.BlockSpec((tm, CONV_DIM), tok),
                  pl.BlockSpec((tm, LANES), tok), pl.BlockSpec((tm, LANES), tok), pl.BlockSpec((tm, 2 * LANES), tok),
                  pl.BlockSpec((tm // SHORT_LEN, LANES), tok), s_spec,
                  _const_spec(expand.shape), _const_spec(dskip.shape), _const_spec(nw.shape), _layer_spec(w_out, layer)],
        out_specs=[pl.BlockSpec((tm, d), tok), s_spec],
        out_shape=[jax.ShapeDtypeStruct((n_tok, d), F32), jax.ShapeDtypeStruct((n_seq, D_INNER, D_STATE), F32)],
        compiler_params=_params(("arbitrary",)),
        name="ssd_short",
    )(x, z, xbc, cum, csp, we, da, s0, expand, dskip, nw, w_out)


def _trunk_single(x, pre_sc, pre_xbc, s0, pre_ffn, w, *, tm):
    kw = dict(n_seq=1, short=False, shared_pre=False, tm=tm)
    x1, t_sc = _sc_mixer(x, pre_sc, w, 0, **kw)
    x2, t_f0 = _conv_ffn(x1, pre_ffn[0], w, 0, final_norm=False, **kw)
    z, xbc, cum, cspt, we, da, t_xbc = _ssm_in(x2, pre_xbc, w, 0, **kw)
    pad_front = cum.shape[0] - x2.shape[0]
    padf = lambda t: jnp.pad(t, ((pad_front, 0), (0, 0))) if pad_front else t
    x3, s_new = _ssd_long(padf(x2), padf(z), padf(xbc), cum, cspt, we, da, s0, w, 0, n_seq=1,
                          tm=max(tm, CHUNK), shared_s0=False)
    y, t_f1 = _conv_ffn(x3[pad_front:], pre_ffn[1], w, 1, final_norm=True, **kw)
    return y, t_sc, t_xbc, s_new, (t_f0, t_f1)


def _trunk_pair(x_long, start, x_short, caches, states_short, w, *, n_seq_long, tm, tm_sc, tm_ssd_short):
    m_sc, m_xbc, m_state, m_ffn = start
    cache_sc, cache_xbc, cache_ffn = caches
    n_seq_short = x_short.shape[0] // SHORT_LEN
    long_kw = dict(n_seq=n_seq_long, short=False, shared_pre=True)
    short_kw = dict(n_seq=n_seq_short, short=True, shared_pre=False, tm=tm)
    x1l, sc_l = _sc_mixer(x_long, m_sc, w, 0, tm=tm_sc, **long_kw)
    x1s, sc_s = _sc_mixer(x_short, cache_sc, w, 0, **short_kw)
    x2l, f0_l, x2s, f0_s = _conv_ffn_both(x1l, m_ffn[0], x1s, cache_ffn, w, 0, n_seq_long=n_seq_long, tm=tm,
                                          tm_short=tm // 2, final_norm=False)
    zl, xbcl, cuml, csptl, wel, dal, xbc_l = _ssm_in(x2l, m_xbc, w, 0, tm=tm, **long_kw)
    zs, xbcs, cums, csps, wes, das, xbc_s = _ssm_in(x2s, cache_xbc, w, 0, **short_kw)
    x3l, state_l = _ssd_long(x2l, zl, xbcl, cuml, csptl, wel, dal, m_state, w, 0, n_seq=n_seq_long, tm=tm,
                             shared_s0=True)
    x3s, state_s = _ssd_short(x2s, zs, xbcs, cums, csps, wes, das, states_short, w, 0, tm=tm_ssd_short)
    yl, f1_l, ys, ffn_s = _conv_ffn_both(x3l, m_ffn[1], x3s, cache_ffn, w, 1, n_seq_long=n_seq_long, tm=tm,
                                         tm_short=tm // 2, final_norm=True, prev_tails=f0_s[None])
    return (yl, sc_l, xbc_l, state_l, (f0_l, f1_l)), (ys, sc_s, xbc_s, state_s, ffn_s)


def kernel(x_prompt, x_sample, cache_sc, cache_ssm_conv, state_ssm, cache_ffn_conv, meta_tokens, norm_mix, norm_ffn, norm_final, sc_w_in, sc_conv_w, sc_w_out, ssm_w_in, ssm_conv_w, ssm_conv_b, ssm_dt_bias, ssm_a_log, ssm_d, ssm_norm_w, ssm_w_out, ffn_w_up, ffn_w_gate, ffn_conv_w, ffn_w_down):
    b, seq, d = x_prompt.shape
    n_dec, dec_len, _ = x_sample.shape
    d_ff = ffn_w_up.shape[2]
    assert dec_len == SHORT_LEN and seq % CHUNK == 0 and N_META % SUBLANES == 0 and N_META <= CHUNK

    pad_heads = lambda v: jnp.pad(v.reshape(1, -1).astype(F32), ((0, 0), (0, LANES - N_HEADS)))
    head_of_lane = jnp.arange(D_INNER, dtype=jnp.int32)[None, :] // HEAD_DIM
    w = dict(
        norm_mix=norm_mix.reshape(-1, 1, d), norm_ffn=norm_ffn.reshape(-1, 1, d), nfin=norm_final.reshape(1, d),
        sc_in=sc_w_in.astype(BF16), sc_cw=sc_conv_w, sc_out=sc_w_out.astype(BF16),
        ssm_in=ssm_w_in.astype(BF16),
        w_dt=jnp.pad(ssm_w_in[0][:, D_INNER + CONV_DIM:], ((0, 0), (0, LANES - N_HEADS))).astype(BF16),
        ssm_cw=ssm_conv_w, ssm_cb=ssm_conv_b[0:1], dtb=pad_heads(ssm_dt_bias[0]), alog=pad_heads(ssm_a_log[0]),
        dskip=jnp.repeat(ssm_d[0], HEAD_DIM).reshape(1, -1), ssm_nw=ssm_norm_w[0:1], ssm_out=ssm_w_out.astype(BF16),
        expand=(jnp.arange(LANES, dtype=jnp.int32)[:, None] == head_of_lane).astype(BF16),
        up=ffn_w_up.astype(BF16), gate=ffn_w_gate.astype(BF16), ffn_cw=ffn_conv_w, down=ffn_w_down.astype(BF16),
    )

    zeros8 = lambda c: jnp.zeros((SUBLANES, c), F32)
    _, m_sc, m_xbc, m_state, m_ffn = _trunk_single(
        meta_tokens.astype(F32), zeros8(d), zeros8(CONV_DIM), jnp.zeros((D_INNER, D_STATE), F32),
        (zeros8(d_ff), zeros8(d_ff)), w, tm=N_META)

    (yp, p_sc, p_xbc, p_state, p_ffn), (ys, s_sc, s_xbc, s_state, s_ffn) = _trunk_pair(
        x_prompt.reshape(b * seq, d), (m_sc, m_xbc, m_state, m_ffn),
        x_sample.reshape(n_dec * dec_len, d), (cache_sc, cache_ssm_conv, cache_ffn_conv),
        state_ssm[0].reshape(n_dec, D_INNER, D_STATE), w,
        n_seq_long=b, tm=512, tm_sc=1024, tm_ssd_short=64)
    tail = lambda t, k: t.reshape(b, SUBLANES, -1)[:, SUBLANES - k:]
    out_prompt = (
        yp.reshape(b, seq, d),
        tail(p_sc, 2)[None], tail(p_xbc, 3)[None],
        p_state.reshape(1, b, N_HEADS, HEAD_DIM, D_STATE),
        jnp.stack([tail(p_ffn[0], 2), tail(p_ffn[1], 2)]),
    )

    out_sample = (
        ys.reshape(n_dec, dec_len, d),
        s_sc[None], s_xbc[None],
        s_state.reshape(1, n_dec, N_HEADS, HEAD_DIM, D_STATE),
        s_ffn,
    )
    return (out_prompt[0], out_sample[0]) + out_prompt[1:] + out_sample[1:]
```

```python
import functools

import jax
import jax.numpy as jnp
from jax import lax
from jax.experimental import pallas as pl
from jax.experimental.pallas import tpu as pltpu

F32 = jnp.float32
BF16 = jnp.bfloat16

EPS = 1e-5
N_META = 16
HEAD_DIM = 64
N_HEADS = 32
N_GROUPS = 4
HEADS_PER_GROUP = N_HEADS // N_GROUPS
D_STATE = 128
D_INNER = N_HEADS * HEAD_DIM
GROUP_W = D_INNER // N_GROUPS
BC_W = N_GROUPS * D_STATE
CONV_DIM = D_INNER + 2 * BC_W
CHUNK = 128
SHORT_LEN = 8
SUBLANES = 8
LANES = 128
STRIP = 64
COL_BLOCK = 512
FFN_COL_BLOCK = 768
NEG_BIG = -1e30
NEG_LOG2E = -1.4426950408889634
VMEM_LIMIT = 56 * 1024 * 1024


def _rmsnorm(x, w):
    r = lax.rsqrt(jnp.mean(x * x, axis=-1, keepdims=True) + EPS)
    return x * r * w


def _silu(x):
    return x / (1.0 + jnp.exp2(x * NEG_LOG2E))


def _softplus(x):
    return jnp.maximum(x, 0.0) + jnp.log1p(jnp.exp(-jnp.abs(x)))


def _dot(a, b):
    return jnp.dot(a, b, preferred_element_type=F32)


def _dot_nt(a, b):
    return lax.dot_general(a, b, (((1,), (1,)), ((), ())), preferred_element_type=F32)


def _dot_exact_lhs(m_bf16, x):
    hi = x.astype(BF16)
    r1 = x - hi.astype(F32)
    mid = r1.astype(BF16)
    lo = (r1 - mid.astype(F32)).astype(BF16)
    return _dot(m_bf16, hi) + _dot(m_bf16, mid) + _dot(m_bf16, lo)


def _conv_strips(buf_ref, pre_ref, w_ref, width, tm, col_lo, col_hi, short, emit):
    rows = min(STRIP, tm)
    row_in_seq = lax.broadcasted_iota(jnp.int32, (rows, LANES), 0) & (SHORT_LEN - 1)
    row8 = lax.broadcasted_iota(jnp.int32, (SHORT_LEN, LANES), 0)
    for c0 in range(col_lo, col_hi, LANES):
        cols = slice(c0, c0 + LANES)
        taps = [w_ref[k:k + 1, cols] for k in range(width)]
        for r0 in range(0, tm, rows):
            if short:
                xv = buf_ref[SUBLANES + r0:SUBLANES + r0 + rows, cols]
                y = xv * taps[width - 1]
                for k in range(1, width):
                    cached = []
                    for s in range(r0 // SHORT_LEN, (r0 + rows) // SHORT_LEN):
                        piece = jnp.broadcast_to(pre_ref[s, width - 2:width - 1, cols], (SHORT_LEN, LANES))
                        for j in range(k - 2, -1, -1):
                            row = jnp.broadcast_to(pre_ref[s, width - 1 - k + j:width - k + j, cols], (SHORT_LEN, LANES))
                            piece = jnp.where(row8 == j, row, piece)
                        cached.append(piece)
                    xk = jnp.where(row_in_seq < k, jnp.concatenate(cached, axis=0), pltpu.roll(xv, k, axis=0))
                    y = y + xk * taps[width - 1 - k]
            else:
                ext = buf_ref[r0:r0 + rows + SUBLANES, cols]
                y = ext[SUBLANES:] * taps[width - 1]
                for k in range(1, width):
                    y = y + pltpu.roll(ext, k, axis=0)[SUBLANES:] * taps[width - 1 - k]
            emit(r0, cols, y)


def _col_blocks(n_cols, block):
    return [(lo, min(lo + block, n_cols)) for lo in range(0, n_cols, block)]


def _pipelined_conv(buf_ref, pre_ref, tail_ref, cw_ref, width, tm, blocks, short, dots, emit, first_of_seq=None):
    if not short:
        @pl.when(pl.program_id(1) == 0 if first_of_seq is None else first_of_seq)
        def _():
            buf_ref[0:SUBLANES, :] = pre_ref[...]

    dots(*blocks[0])
    for i, (lo, hi) in enumerate(blocks):
        if i + 1 < len(blocks):
            dots(*blocks[i + 1])
        _conv_strips(buf_ref, pre_ref, cw_ref, width, tm, lo, hi, short, emit)

    if short:
        for s in range(tm // SHORT_LEN):
            last = SUBLANES + (s + 1) * SHORT_LEN
            tail_ref[s] = buf_ref[last - (width - 1):last, :]
    else:
        tail_ref[...] = buf_ref[tm:tm + SUBLANES, :]
        buf_ref[0:SUBLANES, :] = buf_ref[tm:tm + SUBLANES, :]


def _both_layouts_kernel(xl_ref, prel_ref, xs_ref, pres_ref, *refs, body, n_params, n_out_long, n_out_short,
                         n_long_tiles, tiles_per_seq):
    params, refs = refs[:n_params], refs[n_params:]
    outs_long, outs_short, scratch = refs[:n_out_long], refs[n_out_long:n_out_long + n_out_short], \
        refs[n_out_long + n_out_short:]
    step = pl.program_id(0)

    @pl.when(step < n_long_tiles)
    def _():
        body(xl_ref, prel_ref, *params, *outs_long, *scratch, short=False, first_of_seq=step % tiles_per_seq == 0)

    @pl.when(step >= n_long_tiles)
    def _():
        body(xs_ref, pres_ref, *params, *outs_short, *scratch, short=True)


def _sc_mixer_kernel(x_ref, pre_ref, nw_ref, win_ref, cw_ref, wout_ref, o_ref, tail_ref,
                     buf_ref, b_ref, g_ref, *, short, first_of_seq=None):
    tm, d = x_ref.shape
    h = _rmsnorm(x_ref[...], nw_ref[...]).astype(BF16)

    def dots(lo, hi):
        buf_ref[SUBLANES:SUBLANES + tm, lo:hi] = (_dot(h, win_ref[:, d + lo:d + hi])
                                                   * _dot(h, win_ref[:, 2 * d + lo:2 * d + hi]))
        b_ref[0:tm, lo:hi] = _dot(h, win_ref[:, lo:hi])

    def emit(r0, cols, y):
        rows = slice(r0, r0 + y.shape[0])
        g_ref[rows, cols] = (b_ref[rows, cols] * y).astype(BF16)

    _pipelined_conv(buf_ref, pre_ref, tail_ref, cw_ref, 3, tm, _col_blocks(d, COL_BLOCK), short, dots, emit,
                    first_of_seq)
    o_ref[...] = x_ref[...] + _dot(g_ref[0:tm, :], wout_ref[...])


def _conv_ffn_both_kernel(xl_ref, prel_ref, xs_ref, pres_ref, nw_ref, wup_ref, wgate_ref, cw_ref, wdown_ref,
                          nf_ref, *rest, n_long_tiles, tiles_per_seq, final_norm, n_prev_tails):
    params = (nw_ref, wup_ref, wgate_ref, cw_ref, wdown_ref, nf_ref)
    prev_tails, rest = rest[:1 if n_prev_tails else 0], rest[1 if n_prev_tails else 0:]
    ol_ref, taill_ref, os_ref, tails_ref, buf_ref, g_ref, a_ref = rest
    step = pl.program_id(0)

    @pl.when(step < n_long_tiles)
    def _():
        _conv_ffn_kernel(xl_ref, prel_ref, *params, ol_ref, taill_ref, buf_ref, g_ref, a_ref, short=False,
                         final_norm=final_norm, n_prev_tails=0, first_of_seq=step % tiles_per_seq == 0)

    @pl.when(step >= n_long_tiles)
    def _():
        _conv_ffn_kernel(xs_ref, pres_ref, *params, *prev_tails, os_ref, tails_ref, buf_ref, g_ref, a_ref,
                         short=True, final_norm=final_norm, n_prev_tails=n_prev_tails)


def _conv_ffn_kernel(x_ref, pre_ref, nw_ref, wup_ref, wgate_ref, cw_ref, wdown_ref, nf_ref, *rest,
                     short, final_norm, n_prev_tails, first_of_seq=None):
    if n_prev_tails:
        prev_tails_ref, o_ref, tails_ref, buf_ref, g_ref, a_ref = rest
        tails_ref[0:n_prev_tails] = prev_tails_ref[...]
        tail_ref = tails_ref.at[n_prev_tails]
    else:
        o_ref, tail_ref, buf_ref, g_ref, a_ref = rest
    tm = x_ref.shape[0]
    f = wup_ref.shape[1]
    h = _rmsnorm(x_ref[...], nw_ref[...]).astype(BF16)

    def dots(lo, hi):
        buf_ref[SUBLANES:SUBLANES + tm, lo:hi] = _dot(h, wup_ref[:, lo:hi])
        g_ref[0:tm, lo:hi] = _dot(h, wgate_ref[:, lo:hi])

    def emit(r0, cols, y):
        rows = slice(r0, r0 + y.shape[0])
        a_ref[rows, cols] = (_silu(y) * g_ref[rows, cols]).astype(BF16)

    _pipelined_conv(buf_ref, pre_ref, tail_ref, cw_ref, 3, tm, _col_blocks(f, FFN_COL_BLOCK), short, dots, emit,
                    first_of_seq)
    y = x_ref[...] + _dot(a_ref[0:tm, :], wdown_ref[...])
    if final_norm:
        y = _rmsnorm(y, nf_ref[...])
    o_ref[...] = y


def _head_cols(v, h0, lo):
    return jnp.where(lo, v[:, h0:h0 + 1], v[:, h0 + 1:h0 + 2])


def _ssm_in_kernel(x_ref, pre_ref, nw_ref, w_ref, wdt_ref, cw_ref, cb_ref, dtb_ref, alog_ref,
                   z_ref, xbc_ref, *rest, short, first_of_seq=None):
    tm = x_ref.shape[0]
    h = _rmsnorm(x_ref[...], nw_ref[...]).astype(BF16)
    cum_ref, csp_ref, we_ref, da_ref, tail_ref, buf_ref = rest
    if not short:
        @pl.when(pl.program_id(1) == 0 if first_of_seq is None else first_of_seq)
        def _():
            buf_ref[0:SUBLANES, :] = pre_ref[...]

    def dot_xbc(i):
        cols = slice(i * COL_BLOCK, (i + 1) * COL_BLOCK)
        buf_ref[SUBLANES:SUBLANES + tm, cols] = _dot(h, w_ref[:, D_INNER + i * COL_BLOCK:D_INNER + (i + 1) * COL_BLOCK])

    def dot_z(i):
        cols = slice(i * COL_BLOCK, (i + 1) * COL_BLOCK)
        z_ref[:, cols] = _dot(h, w_ref[:, cols]).astype(BF16)

    def emit(r0, cols, y):
        xbc_ref[r0:r0 + y.shape[0], cols] = _silu(y + cb_ref[:, cols]).astype(BF16)

    def conv_block(i):
        _conv_strips(buf_ref, pre_ref, cw_ref, 4, tm, i * COL_BLOCK, (i + 1) * COL_BLOCK, short, emit)

    n_x, n_z = CONV_DIM // COL_BLOCK, D_INNER // COL_BLOCK
    dot_xbc(0)
    for i in range(n_x):
        if i + 1 < n_x:
            dot_xbc(i + 1)
        if i < n_z:
            dot_z(i)
        conv_block(i)
    for i in range(n_x, n_z):
        dot_z(i)
    dt = _softplus(_dot(h, wdt_ref[...]) + dtb_ref[...])

    if short:
        for s in range(tm // SHORT_LEN):
            tail_ref[s] = buf_ref[SUBLANES + (s + 1) * SHORT_LEN - 3:SUBLANES + (s + 1) * SHORT_LEN, :]
        _decay_terms_short(dt, alog_ref, cum_ref, csp_ref, we_ref, da_ref)
    else:
        tail_ref[...] = buf_ref[tm:tm + SUBLANES, :]
        buf_ref[0:SUBLANES, :] = buf_ref[tm:tm + SUBLANES, :]
        _decay_terms(dt, alog_ref, cum_ref, csp_ref, we_ref, da_ref)


def _decay_terms(dt, alog_ref, cum_ref, cspt_ref, we_ref, da_ref):
    tm = dt.shape[0]
    a = -jnp.exp(alog_ref[...])
    li = lax.broadcasted_iota(jnp.int32, (CHUNK, CHUNK), 0)
    si = lax.broadcasted_iota(jnp.int32, (CHUNK, CHUNK), 1)
    tri = (si <= li).astype(BF16)
    lo8 = lax.broadcasted_iota(jnp.int32, (SUBLANES, LANES), 1) < HEAD_DIM
    pad = CHUNK - tm if tm < CHUNK else 0
    for c in range(max(tm // CHUNK, 1)):
        if pad:
            dt_c = jnp.concatenate([jnp.zeros((pad, LANES), F32), dt], axis=0)
        else:
            dt_c = dt[c * CHUNK:(c + 1) * CHUNK]
        rows = slice(c * CHUNK, (c + 1) * CHUNK)
        cum = _dot_exact_lhs(tri, dt_c * a)
        end = cum[CHUNK - 1:CHUNK, :]
        cum_ref[rows, :] = cum
        cspt_ref[:, rows] = (cum - jnp.log(dt_c)).T
        we_ref[rows, 0:LANES] = (jnp.exp(end - cum) * dt_c).astype(BF16)
        we_ref[rows, LANES:2 * LANES] = jnp.exp(cum).astype(BF16)
        e_end = jnp.broadcast_to(jnp.exp(end), (SUBLANES, LANES))
        for pr in range(N_HEADS // 2):
            da_ref[c * SUBLANES:(c + 1) * SUBLANES, pr * LANES:(pr + 1) * LANES] = _head_cols(e_end, 2 * pr, lo8)


def _decay_terms_short(dt, alog_ref, cum_ref, csp_ref, we_ref, da_ref):
    tm = dt.shape[0]
    dta = dt * -jnp.exp(alog_ref[...])
    li = lax.broadcasted_iota(jnp.int32, (tm, tm), 0)
    si = lax.broadcasted_iota(jnp.int32, (tm, tm), 1)
    same_seq = (li // SHORT_LEN) == (si // SHORT_LEN)
    cum = _dot_exact_lhs((same_seq & (si <= li)).astype(BF16), dta)
    end = _dot_exact_lhs(same_seq.astype(BF16), dta)
    cum_ref[...] = cum
    csp_ref[...] = cum - jnp.log(dt)
    we_ref[:, 0:LANES] = (jnp.exp(end - cum) * dt).astype(BF16)
    we_ref[:, LANES:2 * LANES] = jnp.exp(cum).astype(BF16)
    n_seq = tm // SHORT_LEN
    seq_of_tok = lax.broadcasted_iota(jnp.int32, (n_seq, tm), 1) // SHORT_LEN
    member = (seq_of_tok == lax.broadcasted_iota(jnp.int32, (n_seq, tm), 0)).astype(BF16)
    da_ref[...] = jnp.exp(_dot_exact_lhs(member, dta))


def _ssd_long_kernel(x_ref, z_ref, xbc_ref, cum_ref, cspt_ref, we_ref, da_ref, s0_ref, e_ref,
                     dskip_ref, nw_ref, wout_ref, o_ref, sout_ref, st_ref):
    q = CHUNK

    @pl.when(pl.program_id(1) == 0)
    def _():
        st_ref[...] = s0_ref[...].T

    li = lax.broadcasted_iota(jnp.int32, (q, q), 0)
    si = lax.broadcasted_iota(jnp.int32, (q, q), 1)
    causal = si <= li
    lane = lax.broadcasted_iota(jnp.int32, (q, LANES), 1)
    m_lo = (lane < HEAD_DIM).astype(BF16)
    m_hi = (lane >= HEAD_DIM).astype(BF16)

    normed_chunks = []
    for c in range(x_ref.shape[0] // q):
        rows = slice(c * q, (c + 1) * q)
        cum = cum_ref[rows, :]
        cspt = cspt_ref[:, rows]
        w_end = we_ref[rows, 0:LANES]
        e_cum = we_ref[rows, LANES:2 * LANES]
        normed = []
        for g in range(N_GROUPS):
            gl = slice(g * GROUP_W, (g + 1) * GROUP_W)
            bg = xbc_ref[rows, D_INNER + g * D_STATE:D_INNER + (g + 1) * D_STATE]
            cg = xbc_ref[rows, D_INNER + BC_W + g * D_STATE:D_INNER + BC_W + (g + 1) * D_STATE]
            xs_b = xbc_ref[rows, gl]
            xs = xs_b.astype(F32)
            st_g = st_ref[:, gl]
            cb = _dot_nt(cg, bg)
            y_off = _dot(cg, st_g.astype(BF16))
            w_exp = _dot(w_end, e_ref[:, gl])
            e_exp = _dot(e_cum, e_ref[:, gl])
            xw = xs_b * w_exp.astype(BF16)
            bg_t = bg.astype(F32).T.astype(BF16)
            st_ref[:, gl] = st_g * da_ref[c * SUBLANES:c * SUBLANES + 1, gl] + _dot(bg_t, xw)

            ys = []
            for pr in range(HEADS_PER_GROUP // 2):
                h0 = g * HEADS_PER_GROUP + 2 * pr
                ms = []
                for h in (h0, h0 + 1):
                    seg = cum[:, h:h + 1] - cspt[h:h + 1, :]
                    ms.append((cb * jnp.exp(jnp.where(causal, seg, NEG_BIG))).astype(BF16))
                xp = xs_b[:, pr * LANES:(pr + 1) * LANES]
                rhs = jnp.concatenate([xp * m_lo, xp * m_hi], axis=0)
                ys.append(_dot(jnp.concatenate(ms, axis=1), rhs))
            y = jnp.concatenate(ys, axis=1) + y_off * e_exp + dskip_ref[:, gl] * xs
            y = y * _silu(z_ref[rows, gl].astype(F32))
            y = y * lax.rsqrt(jnp.mean(y * y, axis=-1, keepdims=True) + EPS)
            normed.append((y * nw_ref[:, gl]).astype(BF16))
        normed_chunks.append(jnp.concatenate(normed, axis=1))

    o_ref[...] = x_ref[...] + _dot(jnp.concatenate(normed_chunks, axis=0), wout_ref[...])

    @pl.when(pl.program_id(1) == pl.num_programs(1) - 1)
    def _():
        sout_ref[...] = st_ref[...].T


def _ssd_short_kernel(x_ref, z_ref, xbc_ref, cum_ref, csp_ref, we_ref, da_ref, s0_ref, e_ref,
                      dskip_ref, nw_ref, wout_ref, o_ref, sout_ref):
    q = x_ref.shape[0]
    n_seq = q // SHORT_LEN
    cum = cum_ref[...]
    csp_t = csp_ref[...].T
    li = lax.broadcasted_iota(jnp.int32, (q, q), 0)
    si = lax.broadcasted_iota(jnp.int32, (q, q), 1)
    causal = ((li // SHORT_LEN) == (si // SHORT_LEN)) & (si <= li)
    lane = lax.broadcasted_iota(jnp.int32, (q, LANES), 1)
    m_lo = (lane < HEAD_DIM).astype(BF16)
    m_hi = (lane >= HEAD_DIM).astype(BF16)
    row_seq = lax.broadcasted_iota(jnp.int32, (q, LANES), 0) // SHORT_LEN
    row_seq_k = lax.broadcasted_iota(jnp.int32, (LANES, D_STATE), 0) // SHORT_LEN
    w_end = we_ref[:, 0:LANES]
    e_cum = we_ref[:, LANES:2 * LANES]

    normed = []
    for g in range(N_GROUPS):
        gl = slice(g * GROUP_W, (g + 1) * GROUP_W)
        bg = xbc_ref[:, D_INNER + g * D_STATE:D_INNER + (g + 1) * D_STATE]
        cg = xbc_ref[:, D_INNER + BC_W + g * D_STATE:D_INNER + BC_W + (g + 1) * D_STATE]
        xs_b = xbc_ref[:, gl]
        xs = xs_b.astype(F32)
        cb = _dot_nt(cg, bg)
        w_exp = _dot(w_end, e_ref[:, gl])
        e_exp = _dot(e_cum, e_ref[:, gl])

        cg32 = cg.astype(F32)
        y_off = jnp.zeros((q, GROUP_W), F32)
        for s in range(n_seq):
            cmask = jnp.where(row_seq == s, cg32, 0.0).astype(BF16)
            y_off = y_off + _dot_nt(cmask, s0_ref[s, gl, :].astype(BF16))

        ys = []
        for pr in range(HEADS_PER_GROUP // 2):
            h0 = g * HEADS_PER_GROUP + 2 * pr
            ms = []
            for h in (h0, h0 + 1):
                seg = cum[:, h:h + 1] - csp_t[h:h + 1, :]
                ms.append((cb * jnp.exp(jnp.where(causal, seg, NEG_BIG))).astype(BF16))
            xp = xs_b[:, pr * LANES:(pr + 1) * LANES]
            rhs = jnp.concatenate([xp * m_lo, xp * m_hi], axis=0)
            ys.append(_dot(jnp.concatenate(ms, axis=1), rhs))
        y = jnp.concatenate(ys, axis=1) + y_off * e_exp + dskip_ref[:, gl] * xs
        y = y * _silu(z_ref[:, gl].astype(F32))
        y = y * lax.rsqrt(jnp.mean(y * y, axis=-1, keepdims=True) + EPS)
        normed.append((y * nw_ref[:, gl]).astype(BF16))

        xw = (xs_b * w_exp.astype(BF16)).astype(F32)
        xw_t = jnp.concatenate([xw, jnp.zeros((LANES - q, GROUP_W), F32)], axis=0).T.astype(BF16)
        bg_k = jnp.concatenate([bg.astype(F32), jnp.zeros((LANES - q, D_STATE), F32)], axis=0)
        for s in range(n_seq):
            bmask = jnp.where(row_seq_k == s, bg_k, 0.0).astype(BF16)
            upd = _dot(xw_t, bmask)
            for hh in range(HEADS_PER_GROUP):
                h = g * HEADS_PER_GROUP + hh
                rows = slice(h * HEAD_DIM, (h + 1) * HEAD_DIM)
                sout_ref[s, rows, :] = (s0_ref[s, rows, :] * da_ref[s:s + 1, h:h + 1]
                                        + upd[hh * HEAD_DIM:(hh + 1) * HEAD_DIM])

    o_ref[...] = x_ref[...] + _dot(jnp.concatenate(normed, axis=1), wout_ref[...])


def _const_spec(shape):
    return pl.BlockSpec(shape, lambda *_: (0,) * len(shape), pipeline_mode=pl.Buffered(1))


def _layer_spec(arr, layer):
    return pl.BlockSpec((None,) + arr.shape[1:], lambda *_: (layer, 0, 0), pipeline_mode=pl.Buffered(1))


def _token_layout(n_tok, n_seq, tm, short, shared_pre):
    if short:
        grid = (n_tok // tm,)
        tok = lambda i: (i, 0)

        def carried(c, width, layer):
            block = (tm // SHORT_LEN, width - 1, c)
            pre_spec = pl.BlockSpec((None,) + block, lambda i: (layer, i, 0, 0))
            tail_shape = jax.ShapeDtypeStruct((n_tok // SHORT_LEN, width - 1, c), F32)
            return pre_spec, pl.BlockSpec(block, lambda i: (i, 0, 0)), tail_shape

        return grid, tok, ("arbitrary",), carried
    tiles = n_tok // n_seq // tm
    grid = (n_seq, tiles)
    tok = lambda b, j: (b * tiles + j, 0)
    per_seq = lambda b, j: (b, 0)
    pre_map = (lambda b, j: (0, 0)) if shared_pre else per_seq

    def carried(c, width, layer):
        return (pl.BlockSpec((SUBLANES, c), pre_map), pl.BlockSpec((SUBLANES, c), per_seq),
                jax.ShapeDtypeStruct((n_seq * SUBLANES, c), F32))

    return grid, tok, ("arbitrary", "arbitrary"), carried


def _params(sem):
    return pltpu.CompilerParams(dimension_semantics=sem, vmem_limit_bytes=VMEM_LIMIT)


def _sc_mixer(x, pre, w, layer, *, n_seq, tm, short, shared_pre):
    n_tok, d = x.shape
    grid, tok, sem, carried = _token_layout(n_tok, n_seq, tm, short, shared_pre)
    pre_spec, tail_spec, tail_shape = carried(d, 3, layer)
    return pl.pallas_call(
        functools.partial(_sc_mixer_kernel, short=short),
        grid=grid,
        in_specs=[pl.BlockSpec((tm, d), tok), pre_spec, _layer_spec(w["norm_mix"], 2 * layer),
                  _layer_spec(w["sc_in"], layer), _layer_spec(w["sc_cw"], layer), _layer_spec(w["sc_out"], layer)],
        out_specs=[pl.BlockSpec((tm, d), tok), tail_spec],
        out_shape=[jax.ShapeDtypeStruct((n_tok, d), F32), tail_shape],
        scratch_shapes=[pltpu.VMEM((tm + SUBLANES, d), F32), pltpu.VMEM((tm, d), F32), pltpu.VMEM((tm, d), BF16)],
        compiler_params=_params(sem),
        name="sc_mixer_short" if short else "sc_mixer_long",
    )(x, pre, w["norm_mix"], w["sc_in"], w["sc_cw"], w["sc_out"])


def _conv_ffn(x, pre, w, layer, *, n_seq, tm, short, shared_pre, final_norm, prev_tails=None):
    n_tok, d = x.shape
    f = w["up"].shape[2]
    grid, tok, sem, carried = _token_layout(n_tok, n_seq, tm, short, shared_pre)
    pre_spec, tail_spec, tail_shape = carried(f, 3, layer)
    in_specs = [pl.BlockSpec((tm, d), tok), pre_spec, _layer_spec(w["norm_ffn"], layer),
                _layer_spec(w["up"], layer), _layer_spec(w["gate"], layer), _layer_spec(w["ffn_cw"], layer),
                _layer_spec(w["down"], layer), _const_spec(w["nfin"].shape)]
    operands = [x, pre, w["norm_ffn"], w["up"], w["gate"], w["ffn_cw"], w["down"], w["nfin"]]
    n_prev = 0 if prev_tails is None else prev_tails.shape[0]
    if n_prev:
        s_tile = tm // SHORT_LEN
        in_specs.append(pl.BlockSpec((n_prev, s_tile, 2, f), lambda i: (0, i, 0, 0)))
        operands.append(prev_tails)
        tail_spec = pl.BlockSpec((n_prev + 1, s_tile, 2, f), lambda i: (0, i, 0, 0))
        tail_shape = jax.ShapeDtypeStruct((n_prev + 1,) + prev_tails.shape[1:], F32)
    return pl.pallas_call(
        functools.partial(_conv_ffn_kernel, short=short, final_norm=final_norm, n_prev_tails=n_prev),
        grid=grid,
        in_specs=in_specs,
        out_specs=[pl.BlockSpec((tm, d), tok), tail_spec],
        out_shape=[jax.ShapeDtypeStruct((n_tok, d), F32), tail_shape],
        scratch_shapes=[pltpu.VMEM((tm + SUBLANES, f), F32), pltpu.VMEM((tm, f), F32), pltpu.VMEM((tm, f), BF16)],
        compiler_params=_params(sem),
        name="conv_ffn_short" if short else "conv_ffn_long",
    )(*operands)


def _conv_ffn_both(x_long, pre_long, x_short, cache_short, w, layer, *, n_seq_long, tm, tm_short, final_norm,
                   prev_tails=None):
    (n_long, d), n_short = x_long.shape, x_short.shape[0]
    f = w["up"].shape[2]
    assert tm_short <= tm
    long_tiles, short_tiles = n_long // tm, n_short // tm_short
    tiles_per_seq = long_tiles // n_seq_long
    long_tile = lambda i: jnp.minimum(i, long_tiles - 1)
    short_tile = lambda i: jnp.maximum(i - long_tiles, 0)
    s_tile = tm_short // SHORT_LEN
    n_dec = n_short // SHORT_LEN
    in_specs = [pl.BlockSpec((tm, d), lambda i: (long_tile(i), 0)), pl.BlockSpec((SUBLANES, f), lambda i: (0, 0)),
                pl.BlockSpec((tm_short, d), lambda i: (short_tile(i), 0)),
                pl.BlockSpec((None, s_tile, 2, f), lambda i: (layer, short_tile(i), 0, 0)),
                _layer_spec(w["norm_ffn"], layer), _layer_spec(w["up"], layer), _layer_spec(w["gate"], layer),
                _layer_spec(w["ffn_cw"], layer), _layer_spec(w["down"], layer), _const_spec(w["nfin"].shape)]
    operands = [x_long, pre_long, x_short, cache_short, w["norm_ffn"], w["up"], w["gate"], w["ffn_cw"], w["down"], w["nfin"]]
    n_prev = 0 if prev_tails is None else prev_tails.shape[0]
    if n_prev:
        in_specs.append(pl.BlockSpec((n_prev, s_tile, 2, f), lambda i: (0, short_tile(i), 0, 0)))
        operands.append(prev_tails)
        tail_short = (pl.BlockSpec((n_prev + 1, s_tile, 2, f), lambda i: (0, short_tile(i), 0, 0)),
                      jax.ShapeDtypeStruct((n_prev + 1, n_dec, 2, f), F32))
    else:
        tail_short = (pl.BlockSpec((s_tile, 2, f), lambda i: (short_tile(i), 0, 0)),
                      jax.ShapeDtypeStruct((n_dec, 2, f), F32))
    return pl.pallas_call(
        functools.partial(_conv_ffn_both_kernel, n_long_tiles=long_tiles, tiles_per_seq=tiles_per_seq,
                          final_norm=final_norm, n_prev_tails=n_prev),
        grid=(long_tiles + short_tiles,),
        in_specs=in_specs,
        out_specs=[pl.BlockSpec((tm, d), lambda i: (long_tile(i), 0)),
                   pl.BlockSpec((SUBLANES, f), lambda i: (long_tile(i) // tiles_per_seq, 0)),
                   pl.BlockSpec((tm_short, d), lambda i: (short_tile(i), 0)), tail_short[0]],
        out_shape=[jax.ShapeDtypeStruct((n_long, d), F32), jax.ShapeDtypeStruct((n_seq_long * SUBLANES, f), F32),
                   jax.ShapeDtypeStruct((n_short, d), F32), tail_short[1]],
        scratch_shapes=[pltpu.VMEM((tm + SUBLANES, f), F32), pltpu.VMEM((tm, f), F32), pltpu.VMEM((tm, f), BF16)],
        compiler_params=_params(("arbitrary",)),
        name="conv_ffn_both",
    )(*operands)


def _call_both(body, name, x_long, pre_long, x_short, cache_short, layer, params, param_specs, *, n_seq_long,
               tm, tm_short, width, outs_long, outs_short, scratch_shapes):
    (n_long, d), n_short = x_long.shape, x_short.shape[0]
    c = pre_long.shape[1]
    assert tm_short <= tm
    long_tiles, short_tiles = n_long // tm, n_short // tm_short
    tiles_per_seq = long_tiles // n_seq_long
    long_tile = lambda i: jnp.minimum(i, long_tiles - 1)
    long_seq = lambda i: long_tile(i) // tiles_per_seq
    short_tile = lambda i: jnp.maximum(i - long_tiles, 0)
    cache_block = (tm_short // SHORT_LEN, width - 1, c)
    outs = (outs_long(long_tile, long_seq)
            + [((SUBLANES, c), lambda i: (long_seq(i), 0), (n_seq_long * SUBLANES, c), F32)])
    outs_s = (outs_short(short_tile)
              + [(cache_block, lambda i: (short_tile(i), 0, 0), (n_short // SHORT_LEN, width - 1, c), F32)])
    return pl.pallas_call(
        functools.partial(_both_layouts_kernel, body=body, n_params=len(params), n_out_long=len(outs),
                          n_out_short=len(outs_s), n_long_tiles=long_tiles, tiles_per_seq=tiles_per_seq),
        grid=(long_tiles + short_tiles,),
        in_specs=[pl.BlockSpec((tm, d), lambda i: (long_tile(i), 0)), pl.BlockSpec((SUBLANES, c), lambda i: (0, 0)),
                  pl.BlockSpec((tm_short, d), lambda i: (short_tile(i), 0)),
                  pl.BlockSpec((None,) + cache_block, lambda i: (layer, short_tile(i), 0, 0))] + param_specs,
        out_specs=[pl.BlockSpec(block, index) for block, index, _, _ in outs + outs_s],
        out_shape=[jax.ShapeDtypeStruct(shape, dtype) for _, _, shape, dtype in outs + outs_s],
        scratch_shapes=scratch_shapes,
        compiler_params=_params(("arbitrary",)),
        name=name,
    )(x_long, pre_long, x_short, cache_short, *params)


def _sc_mixer_both(x_long, pre_long, x_short, cache_short, w, layer, *, n_seq_long, tm, tm_short):
    d = x_long.shape[1]
    n_long, n_short = x_long.shape[0], x_short.shape[0]
    params = [w["norm_mix"], w["sc_in"], w["sc_cw"], w["sc_out"]]
    param_specs = [_layer_spec(w["norm_mix"], 2 * layer), _layer_spec(w["sc_in"], layer),
                   _layer_spec(w["sc_cw"], layer), _layer_spec(w["sc_out"], layer)]
    return _call_both(
        _sc_mixer_kernel, "sc_mixer_both", x_long, pre_long, x_short, cache_short, layer, params, param_specs,
        n_seq_long=n_seq_long, tm=tm, tm_short=tm_short, width=3,
        outs_long=lambda tile, seq: [((tm, d), lambda i: (tile(i), 0), (n_long, d), F32)],
        outs_short=lambda tile: [((tm_short, d), lambda i: (tile(i), 0), (n_short, d), F32)],
        scratch_shapes=[pltpu.VMEM((tm + SUBLANES, d), F32), pltpu.VMEM((tm, d), F32), pltpu.VMEM((tm, d), BF16)])


def _ssm_in_both(x_long, pre_long, x_short, cache_short, w, layer, *, n_seq_long, tm, tm_short):
    n_long, n_short = x_long.shape[0], x_short.shape[0]
    params = [w["norm_mix"], w["ssm_in"], w["w_dt"], w["ssm_cw"], w["ssm_cb"], w["dtb"], w["alog"]]
    param_specs = [_layer_spec(w["norm_mix"], 2 * layer + 1), _layer_spec(w["ssm_in"], layer),
                   _const_spec(w["w_dt"].shape), _layer_spec(w["ssm_cw"], layer),
                   _const_spec(w["ssm_cb"].shape), _const_spec(w["dtb"].shape), _const_spec(w["alog"].shape)]
    da_rows = tm // CHUNK * SUBLANES

    def outs_long(tile, seq):
        row_block = lambda i: (tile(i), 0)
        return [((tm, D_INNER), row_block, (n_long, D_INNER), BF16), ((tm, CONV_DIM), row_block, (n_long, CONV_DIM), BF16),
                ((tm, LANES), row_block, (n_long, LANES), F32), ((LANES, tm), lambda i: (0, tile(i)), (LANES, n_long), F32),
                ((tm, 2 * LANES), row_block, (n_long, 2 * LANES), BF16),
                ((da_rows, D_INNER), row_block, (n_long // CHUNK * SUBLANES, D_INNER), F32)]

    def outs_short(tile):
        row_block = lambda i: (tile(i), 0)
        return [((tm_short, D_INNER), row_block, (n_short, D_INNER), BF16),
                ((tm_short, CONV_DIM), row_block, (n_short, CONV_DIM), BF16),
                ((tm_short, LANES), row_block, (n_short, LANES), F32), ((tm_short, LANES), row_block, (n_short, LANES), F32),
                ((tm_short, 2 * LANES), row_block, (n_short, 2 * LANES), BF16),
                ((tm_short // SHORT_LEN, LANES), row_block, (n_short // SHORT_LEN, LANES), F32)]

    return _call_both(
        _ssm_in_kernel, "ssm_in_both", x_long, pre_long, x_short, cache_short, layer, params, param_specs,
        n_seq_long=n_seq_long, tm=tm, tm_short=tm_short, width=4, outs_long=outs_long, outs_short=outs_short,
        scratch_shapes=[pltpu.VMEM((tm + SUBLANES, CONV_DIM), F32)])


def _ssm_in(x, pre, w, layer, *, n_seq, tm, short, shared_pre):
    n_tok, d = x.shape
    params = [w["norm_mix"], w["ssm_in"], w["w_dt"], w["ssm_cw"], w["ssm_cb"], w["dtb"], w["alog"]]
    param_specs = [_layer_spec(w["norm_mix"], 2 * layer + 1), _layer_spec(w["ssm_in"], layer),
                   _const_spec(w["w_dt"].shape), _layer_spec(w["ssm_cw"], layer),
                   _const_spec(w["ssm_cb"].shape), _const_spec(w["dtb"].shape), _const_spec(w["alog"].shape)]
    out_shape = [jax.ShapeDtypeStruct((n_tok, D_INNER), BF16), jax.ShapeDtypeStruct((n_tok, CONV_DIM), BF16)]
    if short:
        grid, tok, sem, carried = _token_layout(n_tok, n_seq, tm, True, shared_pre)
        pre_spec, tail_spec, tail_shape = carried(CONV_DIM, 4, layer)
        in_specs = [pl.BlockSpec((tm, d), tok), pre_spec] + param_specs
        out_specs = [pl.BlockSpec((tm, D_INNER), tok), pl.BlockSpec((tm, CONV_DIM), tok),
                     pl.BlockSpec((tm, LANES), tok), pl.BlockSpec((tm, LANES), tok),
                     pl.BlockSpec((tm, 2 * LANES), tok), pl.BlockSpec((tm // SHORT_LEN, LANES), tok), tail_spec]
        out_shape += [jax.ShapeDtypeStruct((n_tok, LANES), F32), jax.ShapeDtypeStruct((n_tok, LANES), F32),
                      jax.ShapeDtypeStruct((n_tok, 2 * LANES), BF16),
                      jax.ShapeDtypeStruct((n_tok // SHORT_LEN, LANES), F32), tail_shape]
    else:
        grid, tok, sem, carried = _token_layout(n_tok, n_seq, tm, False, shared_pre)
        pre_spec, tail_spec, tail_shape = carried(CONV_DIM, 4, layer)
        tiles = n_tok // n_seq // tm
        tok_t = lambda b, j: (0, b * tiles + j)
        ctm = max(tm, CHUNK)
        da_rows = ctm // CHUNK * SUBLANES
        n_rows = n_tok // tm * ctm
        in_specs = [pl.BlockSpec((tm, d), tok), pre_spec] + param_specs
        out_specs = [pl.BlockSpec((tm, D_INNER), tok), pl.BlockSpec((tm, CONV_DIM), tok),
                     pl.BlockSpec((ctm, LANES), tok), pl.BlockSpec((LANES, ctm), tok_t),
                     pl.BlockSpec((ctm, 2 * LANES), tok), pl.BlockSpec((da_rows, D_INNER), tok), tail_spec]
        out_shape += [jax.ShapeDtypeStruct((n_rows, LANES), F32), jax.ShapeDtypeStruct((LANES, n_rows), F32),
                      jax.ShapeDtypeStruct((n_rows, 2 * LANES), BF16),
                      jax.ShapeDtypeStruct((n_rows // CHUNK * SUBLANES, D_INNER), F32), tail_shape]
    return pl.pallas_call(
        functools.partial(_ssm_in_kernel, short=short),
        grid=grid,
        in_specs=in_specs,
        out_specs=out_specs,
        out_shape=out_shape,
        scratch_shapes=[pltpu.VMEM((tm + SUBLANES, CONV_DIM), F32)],
        compiler_params=_params(sem),
        name="ssm_in_short" if short else "ssm_in_long",
    )(x, pre, *params)


def _ssd_long(x, z, xbc, cum, cspt, we, da, s0, w, layer, *, n_seq, tm, shared_s0):
    expand, dskip, nw, w_out = w["expand"], w["dskip"], w["ssm_nw"], w["ssm_out"]
    n_tok, d = x.shape
    tiles = n_tok // n_seq // tm
    tok = lambda b, c: (b * tiles + c, 0)
    tok_t = lambda b, c: (0, b * tiles + c)
    s_spec = pl.BlockSpec((D_INNER, D_STATE), lambda b, c: (b, 0))
    s0_spec = pl.BlockSpec((D_INNER, D_STATE), lambda b, c: (0, 0)) if shared_s0 else s_spec
    return pl.pallas_call(
        _ssd_long_kernel,
        grid=(n_seq, tiles),
        in_specs=[pl.BlockSpec((tm, d), tok), pl.BlockSpec((tm, D_INNER), tok),
                  pl.BlockSpec((tm, CONV_DIM), tok), pl.BlockSpec((tm, LANES), tok),
                  pl.BlockSpec((LANES, tm), tok_t), pl.BlockSpec((tm, 2 * LANES), tok),
                  pl.BlockSpec((tm // CHUNK * SUBLANES, D_INNER), tok), s0_spec,
                  _const_spec(expand.shape), _const_spec(dskip.shape), _const_spec(nw.shape), _layer_spec(w_out, layer)],
        out_specs=[pl.BlockSpec((tm, d), tok), s_spec],
        out_shape=[jax.ShapeDtypeStruct((n_tok, d), F32), jax.ShapeDtypeStruct((n_seq * D_INNER, D_STATE), F32)],
        scratch_shapes=[pltpu.VMEM((D_STATE, D_INNER), F32)],
        compiler_params=_params(("arbitrary", "arbitrary")),
        name="ssd_long",
    )(x, z, xbc, cum, cspt, we, da, s0, expand, dskip, nw, w_out)


def _ssd_short(x, z, xbc, cum, csp, we, da, s0, w, layer, *, tm):
    expand, dskip, nw, w_out = w["expand"], w["dskip"], w["ssm_nw"], w["ssm_out"]
    n_tok, d = x.shape
    n_seq = s0.shape[0]
    tok = lambda i: (i, 0)
    s_spec = pl.BlockSpec((tm // SHORT_LEN, D_INNER, D_STATE), lambda i: (i, 0, 0))
    return pl.pallas_call(
        _ssd_short_kernel,
        grid=(n_tok // tm,),
        in_specs=[pl.BlockSpec((tm, d), tok), pl.BlockSpec((tm, D_INNER), tok), pl.BlockSpec((tm, CONV_DIM), tok),
                  pl.BlockSpec((tm, LANES), tok), pl.BlockSpec((tm, LANES), tok), pl.BlockSpec((tm, 2 * LANES), tok),
                  pl.BlockSpec((tm // SHORT_LEN, LANES), tok), s_spec,
                  _const_spec(expand.shape), _const_spec(dskip.shape), _const_spec(nw.shape), _layer_spec(w_out, layer)],
        out_specs=[pl.BlockSpec((tm, d), tok), s_spec],
        out_shape=[jax.ShapeDtypeStruct((n_tok, d), F32), jax.ShapeDtypeStruct((n_seq, D_INNER, D_STATE), F32)],
        compiler_params=_params(("arbitrary",)),
        name="ssd_short",
    )(x, z, xbc, cum, csp, we, da, s0, expand, dskip, nw, w_out)


def _trunk_single(x, pre_sc, pre_xbc, s0, pre_ffn, w, *, tm):
    kw = dict(n_seq=1, short=False, shared_pre=False, tm=tm)
    x1, t_sc = _sc_mixer(x, pre_sc, w, 0, **kw)
    x2, t_f0 = _conv_ffn(x1, pre_ffn[0], w, 0, final_norm=False, **kw)
    z, xbc, cum, cspt, we, da, t_xbc = _ssm_in(x2, pre_xbc, w, 0, **kw)
    pad_front = cum.shape[0] - x2.shape[0]
    padf = lambda t: jnp.pad(t, ((pad_front, 0), (0, 0))) if pad_front else t
    x3, s_new = _ssd_long(padf(x2), padf(z), padf(xbc), cum, cspt, we, da, s0, w, 0, n_seq=1,
                          tm=max(tm, CHUNK), shared_s0=False)
    y, t_f1 = _conv_ffn(x3[pad_front:], pre_ffn[1], w, 1, final_norm=True, **kw)
    return y, t_sc, t_xbc, s_new, (t_f0, t_f1)


def _trunk_pair(x_long, start, x_short, caches, states_short, w, *, n_seq_long, tm, tm_sc, tm_ssd_short):
    m_sc, m_xbc, m_state, m_ffn = start
    cache_sc, cache_xbc, cache_ffn = caches
    both = dict(n_seq_long=n_seq_long, tm_short=tm // 2)
    x1l, sc_l, x1s, sc_s = _sc_mixer_both(x_long, m_sc, x_short, cache_sc, w, 0, tm=tm_sc, **both)
    x2l, f0_l, x2s, f0_s = _conv_ffn_both(x1l, m_ffn[0], x1s, cache_ffn, w, 0, tm=tm, final_norm=False, **both)
    (zl, xbcl, cuml, csptl, wel, dal, xbc_l,
     zs, xbcs, cums, csps, wes, das, xbc_s) = _ssm_in_both(x2l, m_xbc, x2s, cache_xbc, w, 0, tm=tm, **both)
    x3l, state_l = _ssd_long(x2l, zl, xbcl, cuml, csptl, wel, dal, m_state, w, 0, n_seq=n_seq_long, tm=tm,
                             shared_s0=True)
    x3s, state_s = _ssd_short(x2s, zs, xbcs, cums, csps, wes, das, states_short, w, 0, tm=tm_ssd_short)
    yl, f1_l, ys, ffn_s = _conv_ffn_both(x3l, m_ffn[1], x3s, cache_ffn, w, 1, tm=tm, final_norm=True,
                                         prev_tails=f0_s[None], **both)
    return (yl, sc_l, xbc_l, state_l, (f0_l, f1_l)), (ys, sc_s, xbc_s, state_s, ffn_s)


def kernel(x_prompt, x_sample, cache_sc, cache_ssm_conv, state_ssm, cache_ffn_conv, meta_tokens, norm_mix, norm_ffn, norm_final, sc_w_in, sc_conv_w, sc_w_out, ssm_w_in, ssm_conv_w, ssm_conv_b, ssm_dt_bias, ssm_a_log, ssm_d, ssm_norm_w, ssm_w_out, ffn_w_up, ffn_w_gate, ffn_conv_w, ffn_w_down):
    b, seq, d = x_prompt.shape
    n_dec, dec_len, _ = x_sample.shape
    d_ff = ffn_w_up.shape[2]
    assert dec_len == SHORT_LEN and seq % CHUNK == 0 and N_META % SUBLANES == 0 and N_META <= CHUNK

    pad_heads = lambda v: jnp.pad(v.reshape(1, -1).astype(F32), ((0, 0), (0, LANES - N_HEADS)))
    head_of_lane = jnp.arange(D_INNER, dtype=jnp.int32)[None, :] // HEAD_DIM
    w = dict(
        norm_mix=norm_mix.reshape(-1, 1, d), norm_ffn=norm_ffn.reshape(-1, 1, d), nfin=norm_final.reshape(1, d),
        sc_in=sc_w_in.astype(BF16), sc_cw=sc_conv_w, sc_out=sc_w_out.astype(BF16),
        ssm_in=ssm_w_in.astype(BF16),
        w_dt=jnp.pad(ssm_w_in[0][:, D_INNER + CONV_DIM:], ((0, 0), (0, LANES - N_HEADS))).astype(BF16),
        ssm_cw=ssm_conv_w, ssm_cb=ssm_conv_b[0:1], dtb=pad_heads(ssm_dt_bias[0]), alog=pad_heads(ssm_a_log[0]),
        dskip=jnp.repeat(ssm_d[0], HEAD_DIM).reshape(1, -1), ssm_nw=ssm_norm_w[0:1], ssm_out=ssm_w_out.astype(BF16),
        expand=(jnp.arange(LANES, dtype=jnp.int32)[:, None] == head_of_lane).astype(BF16),
        up=ffn_w_up.astype(BF16), gate=ffn_w_gate.astype(BF16), ffn_cw=ffn_conv_w, down=ffn_w_down.astype(BF16),
    )

    zeros8 = lambda c: jnp.zeros((SUBLANES, c), F32)
    _, m_sc, m_xbc, m_state, m_ffn = _trunk_single(
        meta_tokens.astype(F32), zeros8(d), zeros8(CONV_DIM), jnp.zeros((D_INNER, D_STATE), F32),
        (zeros8(d_ff), zeros8(d_ff)), w, tm=N_META)

    (yp, p_sc, p_xbc, p_state, p_ffn), (ys, s_sc, s_xbc, s_state, s_ffn) = _trunk_pair(
        x_prompt.reshape(b * seq, d), (m_sc, m_xbc, m_state, m_ffn),
        x_sample.reshape(n_dec * dec_len, d), (cache_sc, cache_ssm_conv, cache_ffn_conv),
        state_ssm[0].reshape(n_dec, D_INNER, D_STATE), w,
        n_seq_long=b, tm=512, tm_sc=1024, tm_ssd_short=64)
    tail = lambda t, k: t.reshape(b, SUBLANES, -1)[:, SUBLANES - k:]
    out_prompt = (
        yp.reshape(b, seq, d),
        tail(p_sc, 2)[None], tail(p_xbc, 3)[None],
        p_state.reshape(1, b, N_HEADS, HEAD_DIM, D_STATE),
        jnp.stack([tail(p_ffn[0], 2), tail(p_ffn[1], 2)]),
    )

    out_sample = (
        ys.reshape(n_dec, dec_len, d),
        s_sc[None], s_xbc[None],
        s_state.reshape(1, n_dec, N_HEADS, HEAD_DIM, D_STATE),
        s_ffn,
    )
    return (out_prompt[0], out_sample[0]) + out_prompt[1:] + out_sample[1:]
```

```python
import functools

import jax
import jax.numpy as jnp
from jax import lax
from jax.experimental import pallas as pl
from jax.experimental.pallas import tpu as pltpu

F32 = jnp.float32
BF16 = jnp.bfloat16

EPS = 1e-5
N_META = 16
HEAD_DIM = 64
N_HEADS = 32
N_GROUPS = 4
HEADS_PER_GROUP = N_HEADS // N_GROUPS
D_STATE = 128
D_INNER = N_HEADS * HEAD_DIM
GROUP_W = D_INNER // N_GROUPS
BC_W = N_GROUPS * D_STATE
CONV_DIM = D_INNER + 2 * BC_W
CHUNK = 128
SHORT_LEN = 8
SUBLANES = 8
LANES = 128
STRIP = 64
COL_BLOCK = 512
FFN_COL_BLOCK = 768
NEG_BIG = -1e30
NEG_LOG2E = -1.4426950408889634
VMEM_LIMIT = 56 * 1024 * 1024


def _rmsnorm(x, w):
    r = lax.rsqrt(jnp.mean(x * x, axis=-1, keepdims=True) + EPS)
    return x * r * w


def _silu(x):
    return x / (1.0 + jnp.exp2(x * NEG_LOG2E))


def _softplus(x):
    return jnp.maximum(x, 0.0) + jnp.log1p(jnp.exp(-jnp.abs(x)))


def _dot(a, b):
    return jnp.dot(a, b, preferred_element_type=F32)


def _dot_nt(a, b):
    return lax.dot_general(a, b, (((1,), (1,)), ((), ())), preferred_element_type=F32)


def _dot_exact_lhs(m_bf16, x):
    hi = x.astype(BF16)
    r1 = x - hi.astype(F32)
    mid = r1.astype(BF16)
    lo = (r1 - mid.astype(F32)).astype(BF16)
    return _dot(m_bf16, hi) + _dot(m_bf16, mid) + _dot(m_bf16, lo)


def _conv_strips(buf_ref, pre_ref, w_ref, width, tm, col_lo, col_hi, short, emit):
    rows = min(STRIP, tm)
    row_in_seq = lax.broadcasted_iota(jnp.int32, (rows, LANES), 0) & (SHORT_LEN - 1)
    row8 = lax.broadcasted_iota(jnp.int32, (SHORT_LEN, LANES), 0)
    for c0 in range(col_lo, col_hi, LANES):
        cols = slice(c0, c0 + LANES)
        taps = [w_ref[k:k + 1, cols] for k in range(width)]
        for r0 in range(0, tm, rows):
            if short:
                xv = buf_ref[SUBLANES + r0:SUBLANES + r0 + rows, cols]
                y = xv * taps[width - 1]
                for k in range(1, width):
                    cached = []
                    for s in range(r0 // SHORT_LEN, (r0 + rows) // SHORT_LEN):
                        piece = jnp.broadcast_to(pre_ref[s, width - 2:width - 1, cols], (SHORT_LEN, LANES))
                        for j in range(k - 2, -1, -1):
                            row = jnp.broadcast_to(pre_ref[s, width - 1 - k + j:width - k + j, cols], (SHORT_LEN, LANES))
                            piece = jnp.where(row8 == j, row, piece)
                        cached.append(piece)
                    xk = jnp.where(row_in_seq < k, jnp.concatenate(cached, axis=0), pltpu.roll(xv, k, axis=0))
                    y = y + xk * taps[width - 1 - k]
            else:
                ext = buf_ref[r0:r0 + rows + SUBLANES, cols]
                y = ext[SUBLANES:] * taps[width - 1]
                for k in range(1, width):
                    y = y + pltpu.roll(ext, k, axis=0)[SUBLANES:] * taps[width - 1 - k]
            emit(r0, cols, y)


def _col_blocks(n_cols, block):
    return [(lo, min(lo + block, n_cols)) for lo in range(0, n_cols, block)]


def _pipelined_conv(buf_ref, pre_ref, tail_ref, cw_ref, width, tm, blocks, short, dots, emit, first_of_seq=None):
    if not short:
        @pl.when(pl.program_id(1) == 0 if first_of_seq is None else first_of_seq)
        def _():
            buf_ref[0:SUBLANES, :] = pre_ref[...]

    dots(*blocks[0])
    for i, (lo, hi) in enumerate(blocks):
        if i + 1 < len(blocks):
            dots(*blocks[i + 1])
        _conv_strips(buf_ref, pre_ref, cw_ref, width, tm, lo, hi, short, emit)

    if short:
        for s in range(tm // SHORT_LEN):
            last = SUBLANES + (s + 1) * SHORT_LEN
            tail_ref[s] = buf_ref[last - (width - 1):last, :]
    else:
        tail_ref[...] = buf_ref[tm:tm + SUBLANES, :]
        buf_ref[0:SUBLANES, :] = buf_ref[tm:tm + SUBLANES, :]


def _sc_mixer_kernel(x_ref, pre_ref, nw_ref, win_ref, cw_ref, wout_ref, o_ref, tail_ref,
                     buf_ref, b_ref, g_ref, *, short):
    tm, d = x_ref.shape
    h = _rmsnorm(x_ref[...], nw_ref[...]).astype(BF16)

    def dots(lo, hi):
        buf_ref[SUBLANES:, lo:hi] = _dot(h, win_ref[:, d + lo:d + hi]) * _dot(h, win_ref[:, 2 * d + lo:2 * d + hi])
        b_ref[:, lo:hi] = _dot(h, win_ref[:, lo:hi])

    def emit(r0, cols, y):
        rows = slice(r0, r0 + y.shape[0])
        g_ref[rows, cols] = (b_ref[rows, cols] * y).astype(BF16)

    _pipelined_conv(buf_ref, pre_ref, tail_ref, cw_ref, 3, tm, _col_blocks(d, COL_BLOCK), short, dots, emit)
    o_ref[...] = x_ref[...] + _dot(g_ref[...], wout_ref[...])


def _conv_ffn_both_kernel(xl_ref, prel_ref, xs_ref, pres_ref, nw_ref, wup_ref, wgate_ref, cw_ref, wdown_ref,
                          nf_ref, *rest, n_long_tiles, tiles_per_seq, final_norm, n_prev_tails):
    params = (nw_ref, wup_ref, wgate_ref, cw_ref, wdown_ref, nf_ref)
    prev_tails, rest = rest[:1 if n_prev_tails else 0], rest[1 if n_prev_tails else 0:]
    ol_ref, taill_ref, os_ref, tails_ref, buf_ref, g_ref, a_ref = rest
    step = pl.program_id(0)

    @pl.when(step < n_long_tiles)
    def _():
        _conv_ffn_kernel(xl_ref, prel_ref, *params, ol_ref, taill_ref, buf_ref, g_ref, a_ref, short=False,
                         final_norm=final_norm, n_prev_tails=0, first_of_seq=step % tiles_per_seq == 0)

    @pl.when(step >= n_long_tiles)
    def _():
        _conv_ffn_kernel(xs_ref, pres_ref, *params, *prev_tails, os_ref, tails_ref, buf_ref, g_ref, a_ref,
                         short=True, final_norm=final_norm, n_prev_tails=n_prev_tails)


def _conv_ffn_kernel(x_ref, pre_ref, nw_ref, wup_ref, wgate_ref, cw_ref, wdown_ref, nf_ref, *rest,
                     short, final_norm, n_prev_tails, first_of_seq=None):
    if n_prev_tails:
        prev_tails_ref, o_ref, tails_ref, buf_ref, g_ref, a_ref = rest
        tails_ref[0:n_prev_tails] = prev_tails_ref[...]
        tail_ref = tails_ref.at[n_prev_tails]
    else:
        o_ref, tail_ref, buf_ref, g_ref, a_ref = rest
    tm = x_ref.shape[0]
    f = wup_ref.shape[1]
    h = _rmsnorm(x_ref[...], nw_ref[...]).astype(BF16)

    def dots(lo, hi):
        buf_ref[SUBLANES:SUBLANES + tm, lo:hi] = _dot(h, wup_ref[:, lo:hi])
        g_ref[0:tm, lo:hi] = _dot(h, wgate_ref[:, lo:hi])

    def emit(r0, cols, y):
        rows = slice(r0, r0 + y.shape[0])
        a_ref[rows, cols] = (_silu(y) * g_ref[rows, cols]).astype(BF16)

    _pipelined_conv(buf_ref, pre_ref, tail_ref, cw_ref, 3, tm, _col_blocks(f, FFN_COL_BLOCK), short, dots, emit,
                    first_of_seq)
    y = x_ref[...] + _dot(a_ref[0:tm, :], wdown_ref[...])
    if final_norm:
        y = _rmsnorm(y, nf_ref[...])
    o_ref[...] = y


def _head_cols(v, h0, lo):
    return jnp.where(lo, v[:, h0:h0 + 1], v[:, h0 + 1:h0 + 2])


def _ssm_in_kernel(x_ref, pre_ref, nw_ref, w_ref, wdt_ref, cw_ref, cb_ref, dtb_ref, alog_ref,
                   z_ref, xbc_ref, *rest, short):
    tm = x_ref.shape[0]
    h = _rmsnorm(x_ref[...], nw_ref[...]).astype(BF16)
    cum_ref, csp_ref, we_ref, da_ref, tail_ref, buf_ref = rest
    if not short:
        @pl.when(pl.program_id(1) == 0)
        def _():
            buf_ref[0:SUBLANES, :] = pre_ref[...]

    def dot_xbc(i):
        cols = slice(i * COL_BLOCK, (i + 1) * COL_BLOCK)
        buf_ref[SUBLANES:, cols] = _dot(h, w_ref[:, D_INNER + i * COL_BLOCK:D_INNER + (i + 1) * COL_BLOCK])

    def dot_z(i):
        cols = slice(i * COL_BLOCK, (i + 1) * COL_BLOCK)
        z_ref[:, cols] = _dot(h, w_ref[:, cols]).astype(BF16)

    def emit(r0, cols, y):
        xbc_ref[r0:r0 + y.shape[0], cols] = _silu(y + cb_ref[:, cols]).astype(BF16)

    def conv_block(i):
        _conv_strips(buf_ref, pre_ref, cw_ref, 4, tm, i * COL_BLOCK, (i + 1) * COL_BLOCK, short, emit)

    n_x, n_z = CONV_DIM // COL_BLOCK, D_INNER // COL_BLOCK
    dot_xbc(0)
    for i in range(n_x):
        if i + 1 < n_x:
            dot_xbc(i + 1)
        if i < n_z:
            dot_z(i)
        conv_block(i)
    for i in range(n_x, n_z):
        dot_z(i)
    dt = _softplus(_dot(h, wdt_ref[...]) + dtb_ref[...])

    if short:
        for s in range(tm // SHORT_LEN):
            tail_ref[s] = buf_ref[SUBLANES + (s + 1) * SHORT_LEN - 3:SUBLANES + (s + 1) * SHORT_LEN, :]
        _decay_terms_short(dt, alog_ref, cum_ref, csp_ref, we_ref, da_ref)
    else:
        tail_ref[...] = buf_ref[tm:, :]
        buf_ref[0:SUBLANES, :] = buf_ref[tm:, :]
        _decay_terms(dt, alog_ref, cum_ref, csp_ref, we_ref, da_ref)


def _decay_terms(dt, alog_ref, cum_ref, cspt_ref, we_ref, da_ref):
    tm = dt.shape[0]
    a = -jnp.exp(alog_ref[...])
    li = lax.broadcasted_iota(jnp.int32, (CHUNK, CHUNK), 0)
    si = lax.broadcasted_iota(jnp.int32, (CHUNK, CHUNK), 1)
    tri = (si <= li).astype(BF16)
    lo8 = lax.broadcasted_iota(jnp.int32, (SUBLANES, LANES), 1) < HEAD_DIM
    pad = CHUNK - tm if tm < CHUNK else 0
    for c in range(max(tm // CHUNK, 1)):
        if pad:
            dt_c = jnp.concatenate([jnp.zeros((pad, LANES), F32), dt], axis=0)
        else:
            dt_c = dt[c * CHUNK:(c + 1) * CHUNK]
        rows = slice(c * CHUNK, (c + 1) * CHUNK)
        cum = _dot_exact_lhs(tri, dt_c * a)
        end = cum[CHUNK - 1:CHUNK, :]
        cum_ref[rows, :] = cum
        cspt_ref[:, rows] = (cum - jnp.log(dt_c)).T
        we_ref[rows, 0:LANES] = (jnp.exp(end - cum) * dt_c).astype(BF16)
        we_ref[rows, LANES:2 * LANES] = jnp.exp(cum).astype(BF16)
        e_end = jnp.broadcast_to(jnp.exp(end), (SUBLANES, LANES))
        for pr in range(N_HEADS // 2):
            da_ref[c * SUBLANES:(c + 1) * SUBLANES, pr * LANES:(pr + 1) * LANES] = _head_cols(e_end, 2 * pr, lo8)


def _decay_terms_short(dt, alog_ref, cum_ref, csp_ref, we_ref, da_ref):
    tm = dt.shape[0]
    dta = dt * -jnp.exp(alog_ref[...])
    li = lax.broadcasted_iota(jnp.int32, (tm, tm), 0)
    si = lax.broadcasted_iota(jnp.int32, (tm, tm), 1)
    same_seq = (li // SHORT_LEN) == (si // SHORT_LEN)
    cum = _dot_exact_lhs((same_seq & (si <= li)).astype(BF16), dta)
    end = _dot_exact_lhs(same_seq.astype(BF16), dta)
    cum_ref[...] = cum
    csp_ref[...] = cum - jnp.log(dt)
    we_ref[:, 0:LANES] = (jnp.exp(end - cum) * dt).astype(BF16)
    we_ref[:, LANES:2 * LANES] = jnp.exp(cum).astype(BF16)
    n_seq = tm // SHORT_LEN
    seq_of_tok = lax.broadcasted_iota(jnp.int32, (n_seq, tm), 1) // SHORT_LEN
    member = (seq_of_tok == lax.broadcasted_iota(jnp.int32, (n_seq, tm), 0)).astype(BF16)
    da_ref[...] = jnp.exp(_dot_exact_lhs(member, dta))


def _ssd_long_kernel(x_ref, z_ref, xbc_ref, cum_ref, cspt_ref, we_ref, da_ref, s0_ref, e_ref,
                     dskip_ref, nw_ref, wout_ref, o_ref, sout_ref, st_ref):
    q = CHUNK

    @pl.when(pl.program_id(1) == 0)
    def _():
        st_ref[...] = s0_ref[...].T

    li = lax.broadcasted_iota(jnp.int32, (q, q), 0)
    si = lax.broadcasted_iota(jnp.int32, (q, q), 1)
    causal = si <= li
    lane = lax.broadcasted_iota(jnp.int32, (q, LANES), 1)
    m_lo = (lane < HEAD_DIM).astype(BF16)
    m_hi = (lane >= HEAD_DIM).astype(BF16)

    normed_chunks = []
    for c in range(x_ref.shape[0] // q):
        rows = slice(c * q, (c + 1) * q)
        cum = cum_ref[rows, :]
        cspt = cspt_ref[:, rows]
        w_end = we_ref[rows, 0:LANES]
        e_cum = we_ref[rows, LANES:2 * LANES]
        normed = []
        for g in range(N_GROUPS):
            gl = slice(g * GROUP_W, (g + 1) * GROUP_W)
            bg = xbc_ref[rows, D_INNER + g * D_STATE:D_INNER + (g + 1) * D_STATE]
            cg = xbc_ref[rows, D_INNER + BC_W + g * D_STATE:D_INNER + BC_W + (g + 1) * D_STATE]
            xs_b = xbc_ref[rows, gl]
            xs = xs_b.astype(F32)
            st_g = st_ref[:, gl]
            cb = _dot_nt(cg, bg)
            y_off = _dot(cg, st_g.astype(BF16))
            w_exp = _dot(w_end, e_ref[:, gl])
            e_exp = _dot(e_cum, e_ref[:, gl])
            xw = xs_b * w_exp.astype(BF16)
            bg_t = bg.astype(F32).T.astype(BF16)
            st_ref[:, gl] = st_g * da_ref[c * SUBLANES:c * SUBLANES + 1, gl] + _dot(bg_t, xw)

            ys = []
            for pr in range(HEADS_PER_GROUP // 2):
                h0 = g * HEADS_PER_GROUP + 2 * pr
                ms = []
                for h in (h0, h0 + 1):
                    seg = cum[:, h:h + 1] - cspt[h:h + 1, :]
                    ms.append((cb * jnp.exp(jnp.where(causal, seg, NEG_BIG))).astype(BF16))
                xp = xs_b[:, pr * LANES:(pr + 1) * LANES]
                rhs = jnp.concatenate([xp * m_lo, xp * m_hi], axis=0)
                ys.append(_dot(jnp.concatenate(ms, axis=1), rhs))
            y = jnp.concatenate(ys, axis=1) + y_off * e_exp + dskip_ref[:, gl] * xs
            y = y * _silu(z_ref[rows, gl].astype(F32))
            y = y * lax.rsqrt(jnp.mean(y * y, axis=-1, keepdims=True) + EPS)
            normed.append((y * nw_ref[:, gl]).astype(BF16))
        normed_chunks.append(jnp.concatenate(normed, axis=1))

    o_ref[...] = x_ref[...] + _dot(jnp.concatenate(normed_chunks, axis=0), wout_ref[...])

    @pl.when(pl.program_id(1) == pl.num_programs(1) - 1)
    def _():
        sout_ref[...] = st_ref[...].T


def _ssd_short_kernel(x_ref, z_ref, xbc_ref, cum_ref, csp_ref, we_ref, da_ref, s0_ref, e_ref,
                      dskip_ref, nw_ref, wout_ref, o_ref, sout_ref):
    q = x_ref.shape[0]
    n_seq = q // SHORT_LEN
    cum = cum_ref[...]
    csp_t = csp_ref[...].T
    li = lax.broadcasted_iota(jnp.int32, (q, q), 0)
    si = lax.broadcasted_iota(jnp.int32, (q, q), 1)
    causal = ((li // SHORT_LEN) == (si // SHORT_LEN)) & (si <= li)
    lane = lax.broadcasted_iota(jnp.int32, (q, LANES), 1)
    m_lo = (lane < HEAD_DIM).astype(BF16)
    m_hi = (lane >= HEAD_DIM).astype(BF16)
    row_seq = lax.broadcasted_iota(jnp.int32, (q, LANES), 0) // SHORT_LEN
    row_seq_k = lax.broadcasted_iota(jnp.int32, (LANES, D_STATE), 0) // SHORT_LEN
    w_end = we_ref[:, 0:LANES]
    e_cum = we_ref[:, LANES:2 * LANES]

    normed = []
    for g in range(N_GROUPS):
        gl = slice(g * GROUP_W, (g + 1) * GROUP_W)
        bg = xbc_ref[:, D_INNER + g * D_STATE:D_INNER + (g + 1) * D_STATE]
        cg = xbc_ref[:, D_INNER + BC_W + g * D_STATE:D_INNER + BC_W + (g + 1) * D_STATE]
        xs_b = xbc_ref[:, gl]
        xs = xs_b.astype(F32)
        cb = _dot_nt(cg, bg)
        w_exp = _dot(w_end, e_ref[:, gl])
        e_exp = _dot(e_cum, e_ref[:, gl])

        cg32 = cg.astype(F32)
        y_off = jnp.zeros((q, GROUP_W), F32)
        for s in range(n_seq):
            cmask = jnp.where(row_seq == s, cg32, 0.0).astype(BF16)
            y_off = y_off + _dot_nt(cmask, s0_ref[s, gl, :].astype(BF16))

        ys = []
        for pr in range(HEADS_PER_GROUP // 2):
            h0 = g * HEADS_PER_GROUP + 2 * pr
            ms = []
            for h in (h0, h0 + 1):
                seg = cum[:, h:h + 1] - csp_t[h:h + 1, :]
                ms.append((cb * jnp.exp(jnp.where(causal, seg, NEG_BIG))).astype(BF16))
            xp = xs_b[:, pr * LANES:(pr + 1) * LANES]
            rhs = jnp.concatenate([xp * m_lo, xp * m_hi], axis=0)
            ys.append(_dot(jnp.concatenate(ms, axis=1), rhs))
        y = jnp.concatenate(ys, axis=1) + y_off * e_exp + dskip_ref[:, gl] * xs
        y = y * _silu(z_ref[:, gl].astype(F32))
        y = y * lax.rsqrt(jnp.mean(y * y, axis=-1, keepdims=True) + EPS)
        normed.append((y * nw_ref[:, gl]).astype(BF16))

        xw = (xs_b * w_exp.astype(BF16)).astype(F32)
        xw_t = jnp.concatenate([xw, jnp.zeros((LANES - q, GROUP_W), F32)], axis=0).T.astype(BF16)
        bg_k = jnp.concatenate([bg.astype(F32), jnp.zeros((LANES - q, D_STATE), F32)], axis=0)
        for s in range(n_seq):
            bmask = jnp.where(row_seq_k == s, bg_k, 0.0).astype(BF16)
            upd = _dot(xw_t, bmask)
            for hh in range(HEADS_PER_GROUP):
                h = g * HEADS_PER_GROUP + hh
                rows = slice(h * HEAD_DIM, (h + 1) * HEAD_DIM)
                sout_ref[s, rows, :] = (s0_ref[s, rows, :] * da_ref[s:s + 1, h:h + 1]
                                        + upd[hh * HEAD_DIM:(hh + 1) * HEAD_DIM])

    o_ref[...] = x_ref[...] + _dot(jnp.concatenate(normed, axis=1), wout_ref[...])


def _const_spec(shape):
    return pl.BlockSpec(shape, lambda *_: (0,) * len(shape), pipeline_mode=pl.Buffered(1))


def _layer_spec(arr, layer):
    return pl.BlockSpec((None,) + arr.shape[1:], lambda *_: (layer, 0, 0), pipeline_mode=pl.Buffered(1))


def _token_layout(n_tok, n_seq, tm, short, shared_pre):
    if short:
        grid = (n_tok // tm,)
        tok = lambda i: (i, 0)

        def carried(c, width, layer):
            block = (tm // SHORT_LEN, width - 1, c)
            pre_spec = pl.BlockSpec((None,) + block, lambda i: (layer, i, 0, 0))
            tail_shape = jax.ShapeDtypeStruct((n_tok // SHORT_LEN, width - 1, c), F32)
            return pre_spec, pl.BlockSpec(block, lambda i: (i, 0, 0)), tail_shape

        return grid, tok, ("arbitrary",), carried
    tiles = n_tok // n_seq // tm
    grid = (n_seq, tiles)
    tok = lambda b, j: (b * tiles + j, 0)
    per_seq = lambda b, j: (b, 0)
    pre_map = (lambda b, j: (0, 0)) if shared_pre else per_seq

    def carried(c, width, layer):
        return (pl.BlockSpec((SUBLANES, c), pre_map), pl.BlockSpec((SUBLANES, c), per_seq),
                jax.ShapeDtypeStruct((n_seq * SUBLANES, c), F32))

    return grid, tok, ("arbitrary", "arbitrary"), carried


def _params(sem):
    return pltpu.CompilerParams(dimension_semantics=sem, vmem_limit_bytes=VMEM_LIMIT)


def _sc_mixer(x, pre, w, layer, *, n_seq, tm, short, shared_pre):
    n_tok, d = x.shape
    grid, tok, sem, carried = _token_layout(n_tok, n_seq, tm, short, shared_pre)
    pre_spec, tail_spec, tail_shape = carried(d, 3, layer)
    return pl.pallas_call(
        functools.partial(_sc_mixer_kernel, short=short),
        grid=grid,
        in_specs=[pl.BlockSpec((tm, d), tok), pre_spec, _layer_spec(w["norm_mix"], 2 * layer),
                  _layer_spec(w["sc_in"], layer), _layer_spec(w["sc_cw"], layer), _layer_spec(w["sc_out"], layer)],
        out_specs=[pl.BlockSpec((tm, d), tok), tail_spec],
        out_shape=[jax.ShapeDtypeStruct((n_tok, d), F32), tail_shape],
        scratch_shapes=[pltpu.VMEM((tm + SUBLANES, d), F32), pltpu.VMEM((tm, d), F32), pltpu.VMEM((tm, d), BF16)],
        compiler_params=_params(sem),
        name="sc_mixer_short" if short else "sc_mixer_long",
    )(x, pre, w["norm_mix"], w["sc_in"], w["sc_cw"], w["sc_out"])


def _conv_ffn(x, pre, w, layer, *, n_seq, tm, short, shared_pre, final_norm):
    n_tok, d = x.shape
    f = w["up"].shape[2]
    grid, tok, sem, carried = _token_layout(n_tok, n_seq, tm, short, shared_pre)
    pre_spec, tail_spec, tail_shape = carried(f, 3, layer)
    return pl.pallas_call(
        functools.partial(_conv_ffn_kernel, short=short, final_norm=final_norm, n_prev_tails=0),
        grid=grid,
        in_specs=[pl.BlockSpec((tm, d), tok), pre_spec, _layer_spec(w["norm_ffn"], layer),
                  _layer_spec(w["up"], layer), _layer_spec(w["gate"], layer), _layer_spec(w["ffn_cw"], layer),
                  _layer_spec(w["down"], layer), _const_spec(w["nfin"].shape)],
        out_specs=[pl.BlockSpec((tm, d), tok), tail_spec],
        out_shape=[jax.ShapeDtypeStruct((n_tok, d), F32), tail_shape],
        scratch_shapes=[pltpu.VMEM((tm + SUBLANES, f), F32), pltpu.VMEM((tm, f), F32), pltpu.VMEM((tm, f), BF16)],
        compiler_params=_params(sem),
        name="conv_ffn_short" if short else "conv_ffn_long",
    )(x, pre, w["norm_ffn"], w["up"], w["gate"], w["ffn_cw"], w["down"], w["nfin"])


def _conv_ffn_both(x_long, pre_long, x_short, cache_short, w, layer, *, n_seq_long, tm, tm_short, final_norm,
                   prev_tails=None):
    (n_long, d), n_short = x_long.shape, x_short.shape[0]
    f = w["up"].shape[2]
    assert tm_short <= tm
    long_tiles, short_tiles = n_long // tm, n_short // tm_short
    tiles_per_seq = long_tiles // n_seq_long
    long_tile = lambda i: jnp.minimum(i, long_tiles - 1)
    short_tile = lambda i: jnp.maximum(i - long_tiles, 0)
    s_tile = tm_short // SHORT_LEN
    n_dec = n_short // SHORT_LEN
    in_specs = [pl.BlockSpec((tm, d), lambda i: (long_tile(i), 0)), pl.BlockSpec((SUBLANES, f), lambda i: (0, 0)),
                pl.BlockSpec((tm_short, d), lambda i: (short_tile(i), 0)),
                pl.BlockSpec((None, s_tile, 2, f), lambda i: (layer, short_tile(i), 0, 0)),
                _layer_spec(w["norm_ffn"], layer), _layer_spec(w["up"], layer), _layer_spec(w["gate"], layer),
                _layer_spec(w["ffn_cw"], layer), _layer_spec(w["down"], layer), _const_spec(w["nfin"].shape)]
    operands = [x_long, pre_long, x_short, cache_short, w["norm_ffn"], w["up"], w["gate"], w["ffn_cw"], w["down"], w["nfin"]]
    n_prev = 0 if prev_tails is None else prev_tails.shape[0]
    if n_prev:
        in_specs.append(pl.BlockSpec((n_prev, s_tile, 2, f), lambda i: (0, short_tile(i), 0, 0)))
        operands.append(prev_tails)
        tail_short = (pl.BlockSpec((n_prev + 1, s_tile, 2, f), lambda i: (0, short_tile(i), 0, 0)),
                      jax.ShapeDtypeStruct((n_prev + 1, n_dec, 2, f), F32))
    else:
        tail_short = (pl.BlockSpec((s_tile, 2, f), lambda i: (short_tile(i), 0, 0)),
                      jax.ShapeDtypeStruct((n_dec, 2, f), F32))
    return pl.pallas_call(
        functools.partial(_conv_ffn_both_kernel, n_long_tiles=long_tiles, tiles_per_seq=tiles_per_seq,
                          final_norm=final_norm, n_prev_tails=n_prev),
        grid=(long_tiles + short_tiles,),
        in_specs=in_specs,
        out_specs=[pl.BlockSpec((tm, d), lambda i: (long_tile(i), 0)),
                   pl.BlockSpec((SUBLANES, f), lambda i: (long_tile(i) // tiles_per_seq, 0)),
                   pl.BlockSpec((tm_short, d), lambda i: (short_tile(i), 0)), tail_short[0]],
        out_shape=[jax.ShapeDtypeStruct((n_long, d), F32), jax.ShapeDtypeStruct((n_seq_long * SUBLANES, f), F32),
                   jax.ShapeDtypeStruct((n_short, d), F32), tail_short[1]],
        scratch_shapes=[pltpu.VMEM((tm + SUBLANES, f), F32), pltpu.VMEM((tm, f), F32), pltpu.VMEM((tm, f), BF16)],
        compiler_params=_params(("arbitrary",)),
        name="conv_ffn_both",
    )(*operands)


def _ssm_in(x, pre, w, layer, *, n_seq, tm, short, shared_pre):
    n_tok, d = x.shape
    params = [w["norm_mix"], w["ssm_in"], w["w_dt"], w["ssm_cw"], w["ssm_cb"], w["dtb"], w["alog"]]
    param_specs = [_layer_spec(w["norm_mix"], 2 * layer + 1), _layer_spec(w["ssm_in"], layer),
                   _const_spec(w["w_dt"].shape), _layer_spec(w["ssm_cw"], layer),
                   _const_spec(w["ssm_cb"].shape), _const_spec(w["dtb"].shape), _const_spec(w["alog"].shape)]
    out_shape = [jax.ShapeDtypeStruct((n_tok, D_INNER), BF16), jax.ShapeDtypeStruct((n_tok, CONV_DIM), BF16)]
    if short:
        grid, tok, sem, carried = _token_layout(n_tok, n_seq, tm, True, shared_pre)
        pre_spec, tail_spec, tail_shape = carried(CONV_DIM, 4, layer)
        in_specs = [pl.BlockSpec((tm, d), tok), pre_spec] + param_specs
        out_specs = [pl.BlockSpec((tm, D_INNER), tok), pl.BlockSpec((tm, CONV_DIM), tok),
                     pl.BlockSpec((tm, LANES), tok), pl.BlockSpec((tm, LANES), tok),
                     pl.BlockSpec((tm, 2 * LANES), tok), pl.BlockSpec((tm // SHORT_LEN, LANES), tok), tail_spec]
        out_shape += [jax.ShapeDtypeStruct((n_tok, LANES), F32), jax.ShapeDtypeStruct((n_tok, LANES), F32),
                      jax.ShapeDtypeStruct((n_tok, 2 * LANES), BF16),
                      jax.ShapeDtypeStruct((n_tok // SHORT_LEN, LANES), F32), tail_shape]
    else:
        grid, tok, sem, carried = _token_layout(n_tok, n_seq, tm, False, shared_pre)
        pre_spec, tail_spec, tail_shape = carried(CONV_DIM, 4, layer)
        tiles = n_tok // n_seq // tm
        tok_t = lambda b, j: (0, b * tiles + j)
        ctm = max(tm, CHUNK)
        da_rows = ctm // CHUNK * SUBLANES
        n_rows = n_tok // tm * ctm
        in_specs = [pl.BlockSpec((tm, d), tok), pre_spec] + param_specs
        out_specs = [pl.BlockSpec((tm, D_INNER), tok), pl.BlockSpec((tm, CONV_DIM), tok),
                     pl.BlockSpec((ctm, LANES), tok), pl.BlockSpec((LANES, ctm), tok_t),
                     pl.BlockSpec((ctm, 2 * LANES), tok), pl.BlockSpec((da_rows, D_INNER), tok), tail_spec]
        out_shape += [jax.ShapeDtypeStruct((n_rows, LANES), F32), jax.ShapeDtypeStruct((LANES, n_rows), F32),
                      jax.ShapeDtypeStruct((n_rows, 2 * LANES), BF16),
                      jax.ShapeDtypeStruct((n_rows // CHUNK * SUBLANES, D_INNER), F32), tail_shape]
    return pl.pallas_call(
        functools.partial(_ssm_in_kernel, short=short),
        grid=grid,
        in_specs=in_specs,
        out_specs=out_specs,
        out_shape=out_shape,
        scratch_shapes=[pltpu.VMEM((tm + SUBLANES, CONV_DIM), F32)],
        compiler_params=_params(sem),
        name="ssm_in_short" if short else "ssm_in_long",
    )(x, pre, *params)


def _ssd_long(x, z, xbc, cum, cspt, we, da, s0, w, layer, *, n_seq, tm, shared_s0):
    expand, dskip, nw, w_out = w["expand"], w["dskip"], w["ssm_nw"], w["ssm_out"]
    n_tok, d = x.shape
    tiles = n_tok // n_seq // tm
    tok = lambda b, c: (b * tiles + c, 0)
    tok_t = lambda b, c: (0, b * tiles + c)
    s_spec = pl.BlockSpec((D_INNER, D_STATE), lambda b, c: (b, 0))
    s0_spec = pl.BlockSpec((D_INNER, D_STATE), lambda b, c: (0, 0)) if shared_s0 else s_spec
    return pl.pallas_call(
        _ssd_long_kernel,
        grid=(n_seq, tiles),
        in_specs=[pl.BlockSpec((tm, d), tok), pl.BlockSpec((tm, D_INNER), tok),
                  pl.BlockSpec((tm, CONV_DIM), tok), pl.BlockSpec((tm, LANES), tok),
                  pl.BlockSpec((LANES, tm), tok_t), pl.BlockSpec((tm, 2 * LANES), tok),
                  pl.BlockSpec((tm // CHUNK * SUBLANES, D_INNER), tok), s0_spec,
                  _const_spec(expand.shape), _const_spec(dskip.shape), _const_spec(nw.shape), _layer_spec(w_out, layer)],
        out_specs=[pl.BlockSpec((tm, d), tok), s_spec],
        out_shape=[jax.ShapeDtypeStruct((n_tok, d), F32), jax.ShapeDtypeStruct((n_seq * D_INNER, D_STATE), F32)],
        scratch_shapes=[pltpu.VMEM((D_STATE, D_INNER), F32)],
        compiler_params=_params(("arbitrary", "arbitrary")),
        name="ssd_long",
    )(x, z, xbc, cum, cspt, we, da, s0, expand, dskip, nw, w_out)


def _ssd_short(x, z, xbc, cum, csp, we, da, s0, w, layer, *, tm):
    expand, dskip, nw, w_out = w["expand"], w["dskip"], w["ssm_nw"], w["ssm_out"]
    n_tok, d = x.shape
    n_seq = s0.shape[0]
    tok = lambda i: (i, 0)
    s_spec = pl.BlockSpec((tm // SHORT_LEN, D_INNER, D_STATE), lambda i: (i, 0, 0))
    return pl.pallas_call(
        _ssd_short_kernel,
        grid=(n_tok // tm,),
        in_specs=[pl.BlockSpec((tm, d), tok), pl.BlockSpec((tm, D_INNER), tok), pl.BlockSpec((tm, CONV_DIM), tok),
                  pl.BlockSpec((tm, LANES), tok), pl.BlockSpec((tm, LANES), tok), pl.BlockSpec((tm, 2 * LANES), tok),
                  pl.BlockSpec((tm // SHORT_LEN, LANES), tok), s_spec,
                  _const_spec(expand.shape), _const_spec(dskip.shape), _const_spec(nw.shape), _layer_spec(w_out, layer)],
        out_specs=[pl.BlockSpec((tm, d), tok), s_spec],
        out_shape=[jax.ShapeDtypeStruct((n_tok, d), F32), jax.ShapeDtypeStruct((n_seq, D_INNER, D_STATE), F32)],
        compiler_params=_params(("arbitrary",)),
        name="ssd_short",
    )(x, z, xbc, cum, csp, we, da, s0, expand, dskip, nw, w_out)


def _trunk_single(x, pre_sc, pre_xbc, s0, pre_ffn, w, *, tm):
    kw = dict(n_seq=1, short=False, shared_pre=False, tm=tm)
    x1, t_sc = _sc_mixer(x, pre_sc, w, 0, **kw)
    x2, t_f0 = _conv_ffn(x1, pre_ffn[0], w, 0, final_norm=False, **kw)
    z, xbc, cum, cspt, we, da, t_xbc = _ssm_in(x2, pre_xbc, w, 0, **kw)
    pad_front = cum.shape[0] - x2.shape[0]
    padf = lambda t: jnp.pad(t, ((pad_front, 0), (0, 0))) if pad_front else t
    x3, s_new = _ssd_long(padf(x2), padf(z), padf(xbc), cum, cspt, we, da, s0, w, 0, n_seq=1,
                          tm=max(tm, CHUNK), shared_s0=False)
    y, t_f1 = _conv_ffn(x3[pad_front:], pre_ffn[1], w, 1, final_norm=True, **kw)
    return y, t_sc, t_xbc, s_new, (t_f0, t_f1)


def _trunk_pair(x_long, start, x_short, caches, states_short, w, *, n_seq_long, tm, tm_sc, tm_ssd_short):
    m_sc, m_xbc, m_state, m_ffn = start
    cache_sc, cache_xbc, cache_ffn = caches
    n_seq_short = x_short.shape[0] // SHORT_LEN
    long_kw = dict(n_seq=n_seq_long, short=False, shared_pre=True)
    short_kw = dict(n_seq=n_seq_short, short=True, shared_pre=False, tm=tm)
    x1l, sc_l = _sc_mixer(x_long, m_sc, w, 0, tm=tm_sc, **long_kw)
    x1s, sc_s = _sc_mixer(x_short, cache_sc, w, 0, **short_kw)
    x2l, f0_l, x2s, f0_s = _conv_ffn_both(x1l, m_ffn[0], x1s, cache_ffn, w, 0, n_seq_long=n_seq_long, tm=tm,
                                          tm_short=tm // 2, final_norm=False)
    zl, xbcl, cuml, csptl, wel, dal, xbc_l = _ssm_in(x2l, m_xbc, w, 0, tm=tm, **long_kw)
    zs, xbcs, cums, csps, wes, das, xbc_s = _ssm_in(x2s, cache_xbc, w, 0, **short_kw)
    x3l, state_l = _ssd_long(x2l, zl, xbcl, cuml, csptl, wel, dal, m_state, w, 0, n_seq=n_seq_long, tm=tm,
                             shared_s0=True)
    x3s, state_s = _ssd_short(x2s, zs, xbcs, cums, csps, wes, das, states_short, w, 0, tm=tm_ssd_short)
    yl, f1_l, ys, ffn_s = _conv_ffn_both(x3l, m_ffn[1], x3s, cache_ffn, w, 1, n_seq_long=n_seq_long, tm=tm,
                                         tm_short=tm // 2, final_norm=True, prev_tails=f0_s[None])
    return (yl, sc_l, xbc_l, state_l, (f0_l, f1_l)), (ys, sc_s, xbc_s, state_s, ffn_s)


def kernel(x_prompt, x_sample, cache_sc, cache_ssm_conv, state_ssm, cache_ffn_conv, meta_tokens, norm_mix, norm_ffn, norm_final, sc_w_in, sc_conv_w, sc_w_out, ssm_w_in, ssm_conv_w, ssm_conv_b, ssm_dt_bias, ssm_a_log, ssm_d, ssm_norm_w, ssm_w_out, ffn_w_up, ffn_w_gate, ffn_conv_w, ffn_w_down):
    b, seq, d = x_prompt.shape
    n_dec, dec_len, _ = x_sample.shape
    d_ff = ffn_w_up.shape[2]
    assert dec_len == SHORT_LEN and seq % CHUNK == 0 and N_META % SUBLANES == 0 and N_META <= CHUNK

    pad_heads = lambda v: jnp.pad(v.reshape(1, -1).astype(F32), ((0, 0), (0, LANES - N_HEADS)))
    head_of_lane = jnp.arange(D_INNER, dtype=jnp.int32)[None, :] // HEAD_DIM
    w = dict(
        norm_mix=norm_mix.reshape(-1, 1, d), norm_ffn=norm_ffn.reshape(-1, 1, d), nfin=norm_final.reshape(1, d),
        sc_in=sc_w_in.astype(BF16), sc_cw=sc_conv_w, sc_out=sc_w_out.astype(BF16),
        ssm_in=ssm_w_in.astype(BF16),
        w_dt=jnp.pad(ssm_w_in[0][:, D_INNER + CONV_DIM:], ((0, 0), (0, LANES - N_HEADS))).astype(BF16),
        ssm_cw=ssm_conv_w, ssm_cb=ssm_conv_b[0:1], dtb=pad_heads(ssm_dt_bias[0]), alog=pad_heads(ssm_a_log[0]),
        dskip=jnp.repeat(ssm_d[0], HEAD_DIM).reshape(1, -1), ssm_nw=ssm_norm_w[0:1], ssm_out=ssm_w_out.astype(BF16),
        expand=(jnp.arange(LANES, dtype=jnp.int32)[:, None] == head_of_lane).astype(BF16),
        up=ffn_w_up.astype(BF16), gate=ffn_w_gate.astype(BF16), ffn_cw=ffn_conv_w, down=ffn_w_down.astype(BF16),
    )

    zeros8 = lambda c: jnp.zeros((SUBLANES, c), F32)
    _, m_sc, m_xbc, m_state, m_ffn = _trunk_single(
        meta_tokens.astype(F32), zeros8(d), zeros8(CONV_DIM), jnp.zeros((D_INNER, D_STATE), F32),
        (zeros8(d_ff), zeros8(d_ff)), w, tm=N_META)

    (yp, p_sc, p_xbc, p_state, p_ffn), (ys, s_sc, s_xbc, s_state, s_ffn) = _trunk_pair(
        x_prompt.reshape(b * seq, d), (m_sc, m_xbc, m_state, m_ffn),
        x_sample.reshape(n_dec * dec_len, d), (cache_sc, cache_ssm_conv, cache_ffn_conv),
        state_ssm[0].reshape(n_dec, D_INNER, D_STATE), w,
        n_seq_long=b, tm=512, tm_sc=1024, tm_ssd_short=64)
    tail = lambda t, k: t.reshape(b, SUBLANES, -1)[:, SUBLANES - k:]
    out_prompt = (
        yp.reshape(b, seq, d),
        tail(p_sc, 2)[None], tail(p_xbc, 3)[None],
        p_state.reshape(1, b, N_HEADS, HEAD_DIM, D_STATE),
        jnp.stack([tail(p_ffn[0], 2), tail(p_ffn[1], 2)]),
    )

    out_sample = (
        ys.reshape(n_dec, dec_len, d),
        s_sc[None], s_xbc[None],
        s_state.reshape(1, n_dec, N_HEADS, HEAD_DIM, D_STATE),
        s_ffn,
    )
    return (out_prompt[0], out_sample[0]) + out_prompt[1:] + out_sample[1:]
```

```python
import functools

import jax
import jax.numpy as jnp
from jax import lax
from jax.experimental import pallas as pl
from jax.experimental.pallas import tpu as pltpu

F32 = jnp.float32
BF16 = jnp.bfloat16

EPS = 1e-5
N_META = 16
HEAD_DIM = 64
N_HEADS = 32
N_GROUPS = 4
HEADS_PER_GROUP = N_HEADS // N_GROUPS
D_STATE = 128
D_INNER = N_HEADS * HEAD_DIM
GROUP_W = D_INNER // N_GROUPS
BC_W = N_GROUPS * D_STATE
CONV_DIM = D_INNER + 2 * BC_W
CHUNK = 128
SHORT_LEN = 8
SUBLANES = 8
LANES = 128
STRIP = 64
COL_BLOCK = 512
FFN_COL_BLOCK = 768
STATE_SLOTS = 3
NEG_BIG = -1e30
NEG_LOG2E = -1.4426950408889634
VMEM_LIMIT = 56 * 1024 * 1024


def _rmsnorm(x, w):
    r = lax.rsqrt(jnp.mean(x * x, axis=-1, keepdims=True) + EPS)
    return x * r * w


def _silu(x):
    return x / (1.0 + jnp.exp2(x * NEG_LOG2E))


def _softplus(x):
    return jnp.maximum(x, 0.0) + jnp.log1p(jnp.exp(-jnp.abs(x)))


def _dot(a, b):
    return jnp.dot(a, b, preferred_element_type=F32)


def _dot_nt(a, b):
    return lax.dot_general(a, b, (((1,), (1,)), ((), ())), preferred_element_type=F32)


def _dot_exact_lhs(m_bf16, x):
    hi = x.astype(BF16)
    r1 = x - hi.astype(F32)
    mid = r1.astype(BF16)
    lo = (r1 - mid.astype(F32)).astype(BF16)
    return _dot(m_bf16, hi) + _dot(m_bf16, mid) + _dot(m_bf16, lo)


def _conv_strips(buf_ref, pre_ref, w_ref, width, tm, col_lo, col_hi, short, emit):
    rows = min(STRIP, tm)
    row_in_seq = lax.broadcasted_iota(jnp.int32, (rows, LANES), 0) & (SHORT_LEN - 1)
    row8 = lax.broadcasted_iota(jnp.int32, (SHORT_LEN, LANES), 0)
    for c0 in range(col_lo, col_hi, LANES):
        cols = slice(c0, c0 + LANES)
        taps = [w_ref[k:k + 1, cols] for k in range(width)]
        for r0 in range(0, tm, rows):
            if short:
                xv = buf_ref[SUBLANES + r0:SUBLANES + r0 + rows, cols]
                y = xv * taps[width - 1]
                for k in range(1, width):
                    cached = []
                    for s in range(r0 // SHORT_LEN, (r0 + rows) // SHORT_LEN):
                        piece = jnp.broadcast_to(pre_ref[s, width - 2:width - 1, cols], (SHORT_LEN, LANES))
                        for j in range(k - 2, -1, -1):
                            row = jnp.broadcast_to(pre_ref[s, width - 1 - k + j:width - k + j, cols], (SHORT_LEN, LANES))
                            piece = jnp.where(row8 == j, row, piece)
                        cached.append(piece)
                    xk = jnp.where(row_in_seq < k, jnp.concatenate(cached, axis=0), pltpu.roll(xv, k, axis=0))
                    y = y + xk * taps[width - 1 - k]
            else:
                ext = buf_ref[r0:r0 + rows + SUBLANES, cols]
                y = ext[SUBLANES:] * taps[width - 1]
                for k in range(1, width):
                    y = y + pltpu.roll(ext, k, axis=0)[SUBLANES:] * taps[width - 1 - k]
            emit(r0, cols, y)


def _col_blocks(n_cols, block):
    return [(lo, min(lo + block, n_cols)) for lo in range(0, n_cols, block)]


def _pipelined_conv(buf_ref, pre_ref, tail_ref, cw_ref, width, tm, blocks, short, dots, emit, first_of_seq=None):
    if not short:
        @pl.when(pl.program_id(1) == 0 if first_of_seq is None else first_of_seq)
        def _():
            buf_ref[0:SUBLANES, :] = pre_ref[...]

    dots(*blocks[0])
    for i, (lo, hi) in enumerate(blocks):
        if i + 1 < len(blocks):
            dots(*blocks[i + 1])
        _conv_strips(buf_ref, pre_ref, cw_ref, width, tm, lo, hi, short, emit)

    if short:
        for s in range(tm // SHORT_LEN):
            last = SUBLANES + (s + 1) * SHORT_LEN
            tail_ref[s] = buf_ref[last - (width - 1):last, :]
    else:
        tail_ref[...] = buf_ref[tm:tm + SUBLANES, :]
        buf_ref[0:SUBLANES, :] = buf_ref[tm:tm + SUBLANES, :]


def _sc_mixer_kernel(x_ref, pre_ref, nw_ref, win_ref, cw_ref, wout_ref, o_ref, tail_ref,
                     buf_ref, b_ref, g_ref, *, short):
    tm, d = x_ref.shape
    h = _rmsnorm(x_ref[...], nw_ref[...]).astype(BF16)

    def dots(lo, hi):
        buf_ref[SUBLANES:, lo:hi] = _dot(h, win_ref[:, d + lo:d + hi]) * _dot(h, win_ref[:, 2 * d + lo:2 * d + hi])
        b_ref[:, lo:hi] = _dot(h, win_ref[:, lo:hi])

    def emit(r0, cols, y):
        rows = slice(r0, r0 + y.shape[0])
        g_ref[rows, cols] = (b_ref[rows, cols] * y).astype(BF16)

    _pipelined_conv(buf_ref, pre_ref, tail_ref, cw_ref, 3, tm, _col_blocks(d, COL_BLOCK), short, dots, emit)
    o_ref[...] = x_ref[...] + _dot(g_ref[...], wout_ref[...])


def _conv_ffn_both_kernel(xl_ref, prel_ref, xs_ref, pres_ref, nw_ref, wup_ref, wgate_ref, cw_ref, wdown_ref,
                          nf_ref, *rest, n_long_tiles, tiles_per_seq, final_norm, n_prev_tails):
    params = (nw_ref, wup_ref, wgate_ref, cw_ref, wdown_ref, nf_ref)
    prev_tails, rest = rest[:1 if n_prev_tails else 0], rest[1 if n_prev_tails else 0:]
    ol_ref, taill_ref, os_ref, tails_ref, buf_ref, g_ref, a_ref = rest
    step = pl.program_id(0)

    @pl.when(step < n_long_tiles)
    def _():
        _conv_ffn_kernel(xl_ref, prel_ref, *params, ol_ref, taill_ref, buf_ref, g_ref, a_ref, short=False,
                         final_norm=final_norm, n_prev_tails=0, first_of_seq=step % tiles_per_seq == 0)

    @pl.when(step >= n_long_tiles)
    def _():
        _conv_ffn_kernel(xs_ref, pres_ref, *params, *prev_tails, os_ref, tails_ref, buf_ref, g_ref, a_ref,
                         short=True, final_norm=final_norm, n_prev_tails=n_prev_tails)


def _conv_ffn_kernel(x_ref, pre_ref, nw_ref, wup_ref, wgate_ref, cw_ref, wdown_ref, nf_ref, *rest,
                     short, final_norm, n_prev_tails, first_of_seq=None):
    if n_prev_tails:
        prev_tails_ref, o_ref, tails_ref, buf_ref, g_ref, a_ref = rest
        tails_ref[0:n_prev_tails] = prev_tails_ref[...]
        tail_ref = tails_ref.at[n_prev_tails]
    else:
        o_ref, tail_ref, buf_ref, g_ref, a_ref = rest
    tm = x_ref.shape[0]
    f = wup_ref.shape[1]
    h = _rmsnorm(x_ref[...], nw_ref[...]).astype(BF16)

    def dots(lo, hi):
        buf_ref[SUBLANES:SUBLANES + tm, lo:hi] = _dot(h, wup_ref[:, lo:hi])
        g_ref[0:tm, lo:hi] = _dot(h, wgate_ref[:, lo:hi])

    def emit(r0, cols, y):
        rows = slice(r0, r0 + y.shape[0])
        a_ref[rows, cols] = (_silu(y) * g_ref[rows, cols]).astype(BF16)

    _pipelined_conv(buf_ref, pre_ref, tail_ref, cw_ref, 3, tm, _col_blocks(f, FFN_COL_BLOCK), short, dots, emit,
                    first_of_seq)
    y = x_ref[...] + _dot(a_ref[0:tm, :], wdown_ref[...])
    if final_norm:
        y = _rmsnorm(y, nf_ref[...])
    o_ref[...] = y


def _head_cols(v, h0, lo):
    return jnp.where(lo, v[:, h0:h0 + 1], v[:, h0 + 1:h0 + 2])


def _ssm_in_kernel(x_ref, pre_ref, nw_ref, w_ref, wdt_ref, cw_ref, cb_ref, dtb_ref, alog_ref,
                   z_ref, xbc_ref, *rest, short):
    tm = x_ref.shape[0]
    h = _rmsnorm(x_ref[...], nw_ref[...]).astype(BF16)
    cum_ref, csp_ref, we_ref, da_ref, tail_ref, buf_ref = rest
    if not short:
        @pl.when(pl.program_id(1) == 0)
        def _():
            buf_ref[0:SUBLANES, :] = pre_ref[...]

    def dot_xbc(i):
        cols = slice(i * COL_BLOCK, (i + 1) * COL_BLOCK)
        buf_ref[SUBLANES:, cols] = _dot(h, w_ref[:, D_INNER + i * COL_BLOCK:D_INNER + (i + 1) * COL_BLOCK])

    def dot_z(i):
        cols = slice(i * COL_BLOCK, (i + 1) * COL_BLOCK)
        z_ref[:, cols] = _dot(h, w_ref[:, cols]).astype(BF16)

    def emit(r0, cols, y):
        xbc_ref[r0:r0 + y.shape[0], cols] = _silu(y + cb_ref[:, cols]).astype(BF16)

    def conv_block(i):
        _conv_strips(buf_ref, pre_ref, cw_ref, 4, tm, i * COL_BLOCK, (i + 1) * COL_BLOCK, short, emit)

    n_x, n_z = CONV_DIM // COL_BLOCK, D_INNER // COL_BLOCK
    dot_xbc(0)
    for i in range(n_x):
        if i + 1 < n_x:
            dot_xbc(i + 1)
        if i < n_z:
            dot_z(i)
        conv_block(i)
    for i in range(n_x, n_z):
        dot_z(i)
    dt = _softplus(_dot(h, wdt_ref[...]) + dtb_ref[...])

    if short:
        for s in range(tm // SHORT_LEN):
            tail_ref[s] = buf_ref[SUBLANES + (s + 1) * SHORT_LEN - 3:SUBLANES + (s + 1) * SHORT_LEN, :]
        _decay_terms_short(dt, alog_ref, cum_ref, csp_ref, we_ref, da_ref)
    else:
        tail_ref[...] = buf_ref[tm:, :]
        buf_ref[0:SUBLANES, :] = buf_ref[tm:, :]
        _decay_terms(dt, alog_ref, cum_ref, csp_ref, we_ref, da_ref)


def _decay_terms(dt, alog_ref, cum_ref, cspt_ref, we_ref, da_ref):
    tm = dt.shape[0]
    a = -jnp.exp(alog_ref[...])
    li = lax.broadcasted_iota(jnp.int32, (CHUNK, CHUNK), 0)
    si = lax.broadcasted_iota(jnp.int32, (CHUNK, CHUNK), 1)
    tri = (si <= li).astype(BF16)
    lo8 = lax.broadcasted_iota(jnp.int32, (SUBLANES, LANES), 1) < HEAD_DIM
    pad = CHUNK - tm if tm < CHUNK else 0
    for c in range(max(tm // CHUNK, 1)):
        if pad:
            dt_c = jnp.concatenate([jnp.zeros((pad, LANES), F32), dt], axis=0)
        else:
            dt_c = dt[c * CHUNK:(c + 1) * CHUNK]
        rows = slice(c * CHUNK, (c + 1) * CHUNK)
        cum = _dot_exact_lhs(tri, dt_c * a)
        end = cum[CHUNK - 1:CHUNK, :]
        cum_ref[rows, :] = cum
        cspt_ref[:, rows] = (cum - jnp.log(dt_c)).T
        we_ref[rows, 0:LANES] = (jnp.exp(end - cum) * dt_c).astype(BF16)
        we_ref[rows, LANES:2 * LANES] = jnp.exp(cum).astype(BF16)
        e_end = jnp.broadcast_to(jnp.exp(end), (SUBLANES, LANES))
        for pr in range(N_HEADS // 2):
            da_ref[c * SUBLANES:(c + 1) * SUBLANES, pr * LANES:(pr + 1) * LANES] = _head_cols(e_end, 2 * pr, lo8)


def _decay_terms_short(dt, alog_ref, cum_ref, csp_ref, we_ref, da_ref):
    tm = dt.shape[0]
    dta = dt * -jnp.exp(alog_ref[...])
    li = lax.broadcasted_iota(jnp.int32, (tm, tm), 0)
    si = lax.broadcasted_iota(jnp.int32, (tm, tm), 1)
    same_seq = (li // SHORT_LEN) == (si // SHORT_LEN)
    cum = _dot_exact_lhs((same_seq & (si <= li)).astype(BF16), dta)
    end = _dot_exact_lhs(same_seq.astype(BF16), dta)
    cum_ref[...] = cum
    csp_ref[...] = cum - jnp.log(dt)
    we_ref[:, 0:LANES] = (jnp.exp(end - cum) * dt).astype(BF16)
    we_ref[:, LANES:2 * LANES] = jnp.exp(cum).astype(BF16)
    n_seq = tm // SHORT_LEN
    seq_of_tok = lax.broadcasted_iota(jnp.int32, (n_seq, tm), 1) // SHORT_LEN
    member = (seq_of_tok == lax.broadcasted_iota(jnp.int32, (n_seq, tm), 0)).astype(BF16)
    da_ref[...] = jnp.exp(_dot_exact_lhs(member, dta))


def _ssd_long_kernel(x_ref, z_ref, xbc_ref, cum_ref, cspt_ref, we_ref, da_ref, s0_ref, e_ref,
                     dskip_ref, nw_ref, wout_ref, o_ref, sout_ref, st_ref):
    q = CHUNK

    @pl.when(pl.program_id(1) == 0)
    def _():
        st_ref[...] = s0_ref[...].T

    li = lax.broadcasted_iota(jnp.int32, (q, q), 0)
    si = lax.broadcasted_iota(jnp.int32, (q, q), 1)
    causal = si <= li
    lane = lax.broadcasted_iota(jnp.int32, (q, LANES), 1)
    m_lo = (lane < HEAD_DIM).astype(BF16)
    m_hi = (lane >= HEAD_DIM).astype(BF16)

    normed_chunks = []
    for c in range(x_ref.shape[0] // q):
        rows = slice(c * q, (c + 1) * q)
        cum = cum_ref[rows, :]
        cspt = cspt_ref[:, rows]
        w_end = we_ref[rows, 0:LANES]
        e_cum = we_ref[rows, LANES:2 * LANES]
        normed = []
        for g in range(N_GROUPS):
            gl = slice(g * GROUP_W, (g + 1) * GROUP_W)
            bg = xbc_ref[rows, D_INNER + g * D_STATE:D_INNER + (g + 1) * D_STATE]
            cg = xbc_ref[rows, D_INNER + BC_W + g * D_STATE:D_INNER + BC_W + (g + 1) * D_STATE]
            xs_b = xbc_ref[rows, gl]
            xs = xs_b.astype(F32)
            st_g = st_ref[:, gl]
            cb = _dot_nt(cg, bg)
            y_off = _dot(cg, st_g.astype(BF16))
            w_exp = _dot(w_end, e_ref[:, gl])
            e_exp = _dot(e_cum, e_ref[:, gl])
            xw = xs_b * w_exp.astype(BF16)
            bg_t = bg.astype(F32).T.astype(BF16)
            st_ref[:, gl] = st_g * da_ref[c * SUBLANES:c * SUBLANES + 1, gl] + _dot(bg_t, xw)

            ys = []
            for pr in range(HEADS_PER_GROUP // 2):
                h0 = g * HEADS_PER_GROUP + 2 * pr
                ms = []
                for h in (h0, h0 + 1):
                    seg = cum[:, h:h + 1] - cspt[h:h + 1, :]
                    ms.append((cb * jnp.exp(jnp.where(causal, seg, NEG_BIG))).astype(BF16))
                xp = xs_b[:, pr * LANES:(pr + 1) * LANES]
                rhs = jnp.concatenate([xp * m_lo, xp * m_hi], axis=0)
                ys.append(_dot(jnp.concatenate(ms, axis=1), rhs))
            y = jnp.concatenate(ys, axis=1) + y_off * e_exp + dskip_ref[:, gl] * xs
            y = y * _silu(z_ref[rows, gl].astype(F32))
            y = y * lax.rsqrt(jnp.mean(y * y, axis=-1, keepdims=True) + EPS)
            normed.append((y * nw_ref[:, gl]).astype(BF16))
        normed_chunks.append(jnp.concatenate(normed, axis=1))

    o_ref[...] = x_ref[...] + _dot(jnp.concatenate(normed_chunks, axis=0), wout_ref[...])

    @pl.when(pl.program_id(1) == pl.num_programs(1) - 1)
    def _():
        sout_ref[...] = st_ref[...].T


def _ssd_short_kernel(x_ref, z_ref, xbc_ref, cum_ref, csp_ref, we_ref, da_ref, s0_hbm, e_ref,
                      dskip_ref, nw_ref, wout_ref, o_ref, sout_ref, s0_buf, s0_sem):
    q = x_ref.shape[0]
    n_seq = q // SHORT_LEN
    step, n_steps = pl.program_id(0), pl.num_programs(0)

    def fetch(tile):
        slot = tile % STATE_SLOTS
        return pltpu.make_async_copy(s0_hbm.at[pl.ds(tile * n_seq, n_seq)], s0_buf.at[slot], s0_sem.at[slot])

    @pl.when(step == 0)
    def _():
        fetch(0).start()

    @pl.when((step == 0) & (n_steps > 1))
    def _():
        fetch(1).start()

    @pl.when(step + 2 < n_steps)
    def _():
        fetch(step + 2).start()

    fetch(step).wait()
    s0_ref = s0_buf.at[step % STATE_SLOTS]
    cum = cum_ref[...]
    csp_t = csp_ref[...].T
    li = lax.broadcasted_iota(jnp.int32, (q, q), 0)
    si = lax.broadcasted_iota(jnp.int32, (q, q), 1)
    causal = ((li // SHORT_LEN) == (si // SHORT_LEN)) & (si <= li)
    lane = lax.broadcasted_iota(jnp.int32, (q, LANES), 1)
    m_lo = (lane < HEAD_DIM).astype(BF16)
    m_hi = (lane >= HEAD_DIM).astype(BF16)
    row_seq = lax.broadcasted_iota(jnp.int32, (q, LANES), 0) // SHORT_LEN
    row_seq_k = lax.broadcasted_iota(jnp.int32, (LANES, D_STATE), 0) // SHORT_LEN
    w_end = we_ref[:, 0:LANES]
    e_cum = we_ref[:, LANES:2 * LANES]

    normed = []
    for g in range(N_GROUPS):
        gl = slice(g * GROUP_W, (g + 1) * GROUP_W)
        bg = xbc_ref[:, D_INNER + g * D_STATE:D_INNER + (g + 1) * D_STATE]
        cg = xbc_ref[:, D_INNER + BC_W + g * D_STATE:D_INNER + BC_W + (g + 1) * D_STATE]
        xs_b = xbc_ref[:, gl]
        xs = xs_b.astype(F32)
        cb = _dot_nt(cg, bg)
        w_exp = _dot(w_end, e_ref[:, gl])
        e_exp = _dot(e_cum, e_ref[:, gl])

        cg32 = cg.astype(F32)
        y_off = jnp.zeros((q, GROUP_W), F32)
        for s in range(n_seq):
            cmask = jnp.where(row_seq == s, cg32, 0.0).astype(BF16)
            y_off = y_off + _dot_nt(cmask, s0_ref[s, gl, :].astype(BF16))

        ys = []
        for pr in range(HEADS_PER_GROUP // 2):
            h0 = g * HEADS_PER_GROUP + 2 * pr
            ms = []
            for h in (h0, h0 + 1):
                seg = cum[:, h:h + 1] - csp_t[h:h + 1, :]
                ms.append((cb * jnp.exp(jnp.where(causal, seg, NEG_BIG))).astype(BF16))
            xp = xs_b[:, pr * LANES:(pr + 1) * LANES]
            rhs = jnp.concatenate([xp * m_lo, xp * m_hi], axis=0)
            ys.append(_dot(jnp.concatenate(ms, axis=1), rhs))
        y = jnp.concatenate(ys, axis=1) + y_off * e_exp + dskip_ref[:, gl] * xs
        y = y * _silu(z_ref[:, gl].astype(F32))
        y = y * lax.rsqrt(jnp.mean(y * y, axis=-1, keepdims=True) + EPS)
        normed.append((y * nw_ref[:, gl]).astype(BF16))

        xw = (xs_b * w_exp.astype(BF16)).astype(F32)
        xw_t = jnp.concatenate([xw, jnp.zeros((LANES - q, GROUP_W), F32)], axis=0).T.astype(BF16)
        bg_k = jnp.concatenate([bg.astype(F32), jnp.zeros((LANES - q, D_STATE), F32)], axis=0)
        for s in range(n_seq):
            bmask = jnp.where(row_seq_k == s, bg_k, 0.0).astype(BF16)
            upd = _dot(xw_t, bmask)
            for hh in range(HEADS_PER_GROUP):
                h = g * HEADS_PER_GROUP + hh
                rows = slice(h * HEAD_DIM, (h + 1) * HEAD_DIM)
                sout_ref[s, rows, :] = (s0_ref[s, rows, :] * da_ref[s:s + 1, h:h + 1]
                                        + upd[hh * HEAD_DIM:(hh + 1) * HEAD_DIM])

    o_ref[...] = x_ref[...] + _dot(jnp.concatenate(normed, axis=1), wout_ref[...])


def _const_spec(shape):
    return pl.BlockSpec(shape, lambda *_: (0,) * len(shape), pipeline_mode=pl.Buffered(1))


def _layer_spec(arr, layer):
    return pl.BlockSpec((None,) + arr.shape[1:], lambda *_: (layer, 0, 0), pipeline_mode=pl.Buffered(1))


def _token_layout(n_tok, n_seq, tm, short, shared_pre):
    if short:
        grid = (n_tok // tm,)
        tok = lambda i: (i, 0)

        def carried(c, width, layer):
            block = (tm // SHORT_LEN, width - 1, c)
            pre_spec = pl.BlockSpec((None,) + block, lambda i: (layer, i, 0, 0))
            tail_shape = jax.ShapeDtypeStruct((n_tok // SHORT_LEN, width - 1, c), F32)
            return pre_spec, pl.BlockSpec(block, lambda i: (i, 0, 0)), tail_shape

        return grid, tok, ("arbitrary",), carried
    tiles = n_tok // n_seq // tm
    grid = (n_seq, tiles)
    tok = lambda b, j: (b * tiles + j, 0)
    per_seq = lambda b, j: (b, 0)
    pre_map = (lambda b, j: (0, 0)) if shared_pre else per_seq

    def carried(c, width, layer):
        return (pl.BlockSpec((SUBLANES, c), pre_map), pl.BlockSpec((SUBLANES, c), per_seq),
                jax.ShapeDtypeStruct((n_seq * SUBLANES, c), F32))

    return grid, tok, ("arbitrary", "arbitrary"), carried


def _params(sem):
    return pltpu.CompilerParams(dimension_semantics=sem, vmem_limit_bytes=VMEM_LIMIT)


def _sc_mixer(x, pre, w, layer, *, n_seq, tm, short, shared_pre):
    n_tok, d = x.shape
    grid, tok, sem, carried = _token_layout(n_tok, n_seq, tm, short, shared_pre)
    pre_spec, tail_spec, tail_shape = carried(d, 3, layer)
    return pl.pallas_call(
        functools.partial(_sc_mixer_kernel, short=short),
        grid=grid,
        in_specs=[pl.BlockSpec((tm, d), tok), pre_spec, _layer_spec(w["norm_mix"], 2 * layer),
                  _layer_spec(w["sc_in"], layer), _layer_spec(w["sc_cw"], layer), _layer_spec(w["sc_out"], layer)],
        out_specs=[pl.BlockSpec((tm, d), tok), tail_spec],
        out_shape=[jax.ShapeDtypeStruct((n_tok, d), F32), tail_shape],
        scratch_shapes=[pltpu.VMEM((tm + SUBLANES, d), F32), pltpu.VMEM((tm, d), F32), pltpu.VMEM((tm, d), BF16)],
        compiler_params=_params(sem),
        name="sc_mixer_short" if short else "sc_mixer_long",
    )(x, pre, w["norm_mix"], w["sc_in"], w["sc_cw"], w["sc_out"])


def _conv_ffn(x, pre, w, layer, *, n_seq, tm, short, shared_pre, final_norm):
    n_tok, d = x.shape
    f = w["up"].shape[2]
    grid, tok, sem, carried = _token_layout(n_tok, n_seq, tm, short, shared_pre)
    pre_spec, tail_spec, tail_shape = carried(f, 3, layer)
    return pl.pallas_call(
        functools.partial(_conv_ffn_kernel, short=short, final_norm=final_norm, n_prev_tails=0),
        grid=grid,
        in_specs=[pl.BlockSpec((tm, d), tok), pre_spec, _layer_spec(w["norm_ffn"], layer),
                  _layer_spec(w["up"], layer), _layer_spec(w["gate"], layer), _layer_spec(w["ffn_cw"], layer),
                  _layer_spec(w["down"], layer), _const_spec(w["nfin"].shape)],
        out_specs=[pl.BlockSpec((tm, d), tok), tail_spec],
        out_shape=[jax.ShapeDtypeStruct((n_tok, d), F32), tail_shape],
        scratch_shapes=[pltpu.VMEM((tm + SUBLANES, f), F32), pltpu.VMEM((tm, f), F32), pltpu.VMEM((tm, f), BF16)],
        compiler_params=_params(sem),
        name="conv_ffn_short" if short else "conv_ffn_long",
    )(x, pre, w["norm_ffn"], w["up"], w["gate"], w["ffn_cw"], w["down"], w["nfin"])


def _conv_ffn_both(x_long, pre_long, x_short, cache_short, w, layer, *, n_seq_long, tm, tm_short, final_norm,
                   prev_tails=None):
    (n_long, d), n_short = x_long.shape, x_short.shape[0]
    f = w["up"].shape[2]
    assert tm_short <= tm
    long_tiles, short_tiles = n_long // tm, n_short // tm_short
    tiles_per_seq = long_tiles // n_seq_long
    long_tile = lambda i: jnp.minimum(i, long_tiles - 1)
    short_tile = lambda i: jnp.maximum(i - long_tiles, 0)
    s_tile = tm_short // SHORT_LEN
    n_dec = n_short // SHORT_LEN
    in_specs = [pl.BlockSpec((tm, d), lambda i: (long_tile(i), 0)), pl.BlockSpec((SUBLANES, f), lambda i: (0, 0)),
                pl.BlockSpec((tm_short, d), lambda i: (short_tile(i), 0)),
                pl.BlockSpec((None, s_tile, 2, f), lambda i: (layer, short_tile(i), 0, 0)),
                _layer_spec(w["norm_ffn"], layer), _layer_spec(w["up"], layer), _layer_spec(w["gate"], layer),
                _layer_spec(w["ffn_cw"], layer), _layer_spec(w["down"], layer), _const_spec(w["nfin"].shape)]
    operands = [x_long, pre_long, x_short, cache_short, w["norm_ffn"], w["up"], w["gate"], w["ffn_cw"], w["down"], w["nfin"]]
    n_prev = 0 if prev_tails is None else prev_tails.shape[0]
    if n_prev:
        in_specs.append(pl.BlockSpec((n_prev, s_tile, 2, f), lambda i: (0, short_tile(i), 0, 0)))
        operands.append(prev_tails)
        tail_short = (pl.BlockSpec((n_prev + 1, s_tile, 2, f), lambda i: (0, short_tile(i), 0, 0)),
                      jax.ShapeDtypeStruct((n_prev + 1, n_dec, 2, f), F32))
    else:
        tail_short = (pl.BlockSpec((s_tile, 2, f), lambda i: (short_tile(i), 0, 0)),
                      jax.ShapeDtypeStruct((n_dec, 2, f), F32))
    return pl.pallas_call(
        functools.partial(_conv_ffn_both_kernel, n_long_tiles=long_tiles, tiles_per_seq=tiles_per_seq,
                          final_norm=final_norm, n_prev_tails=n_prev),
        grid=(long_tiles + short_tiles,),
        in_specs=in_specs,
        out_specs=[pl.BlockSpec((tm, d), lambda i: (long_tile(i), 0)),
                   pl.BlockSpec((SUBLANES, f), lambda i: (long_tile(i) // tiles_per_seq, 0)),
                   pl.BlockSpec((tm_short, d), lambda i: (short_tile(i), 0)), tail_short[0]],
        out_shape=[jax.ShapeDtypeStruct((n_long, d), F32), jax.ShapeDtypeStruct((n_seq_long * SUBLANES, f), F32),
                   jax.ShapeDtypeStruct((n_short, d), F32), tail_short[1]],
        scratch_shapes=[pltpu.VMEM((tm + SUBLANES, f), F32), pltpu.VMEM((tm, f), F32), pltpu.VMEM((tm, f), BF16)],
        compiler_params=_params(("arbitrary",)),
        name="conv_ffn_both",
    )(*operands)


def _ssm_in(x, pre, w, layer, *, n_seq, tm, short, shared_pre):
    n_tok, d = x.shape
    params = [w["norm_mix"], w["ssm_in"], w["w_dt"], w["ssm_cw"], w["ssm_cb"], w["dtb"], w["alog"]]
    param_specs = [_layer_spec(w["norm_mix"], 2 * layer + 1), _layer_spec(w["ssm_in"], layer),
                   _const_spec(w["w_dt"].shape), _layer_spec(w["ssm_cw"], layer),
                   _const_spec(w["ssm_cb"].shape), _const_spec(w["dtb"].shape), _const_spec(w["alog"].shape)]
    out_shape = [jax.ShapeDtypeStruct((n_tok, D_INNER), BF16), jax.ShapeDtypeStruct((n_tok, CONV_DIM), BF16)]
    if short:
        grid, tok, sem, carried = _token_layout(n_tok, n_seq, tm, True, shared_pre)
        pre_spec, tail_spec, tail_shape = carried(CONV_DIM, 4, layer)
        in_specs = [pl.BlockSpec((tm, d), tok), pre_spec] + param_specs
        out_specs = [pl.BlockSpec((tm, D_INNER), tok), pl.BlockSpec((tm, CONV_DIM), tok),
                     pl.BlockSpec((tm, LANES), tok), pl.BlockSpec((tm, LANES), tok),
                     pl.BlockSpec((tm, 2 * LANES), tok), pl.BlockSpec((tm // SHORT_LEN, LANES), tok), tail_spec]
        out_shape += [jax.ShapeDtypeStruct((n_tok, LANES), F32), jax.ShapeDtypeStruct((n_tok, LANES), F32),
                      jax.ShapeDtypeStruct((n_tok, 2 * LANES), BF16),
                      jax.ShapeDtypeStruct((n_tok // SHORT_LEN, LANES), F32), tail_shape]
    else:
        grid, tok, sem, carried = _token_layout(n_tok, n_seq, tm, False, shared_pre)
        pre_spec, tail_spec, tail_shape = carried(CONV_DIM, 4, layer)
        tiles = n_tok // n_seq // tm
        tok_t = lambda b, j: (0, b * tiles + j)
        ctm = max(tm, CHUNK)
        da_rows = ctm // CHUNK * SUBLANES
        n_rows = n_tok // tm * ctm
        in_specs = [pl.BlockSpec((tm, d), tok), pre_spec] + param_specs
        out_specs = [pl.BlockSpec((tm, D_INNER), tok), pl.BlockSpec((tm, CONV_DIM), tok),
                     pl.BlockSpec((ctm, LANES), tok), pl.BlockSpec((LANES, ctm), tok_t),
                     pl.BlockSpec((ctm, 2 * LANES), tok), pl.BlockSpec((da_rows, D_INNER), tok), tail_spec]
        out_shape += [jax.ShapeDtypeStruct((n_rows, LANES), F32), jax.ShapeDtypeStruct((LANES, n_rows), F32),
                      jax.ShapeDtypeStruct((n_rows, 2 * LANES), BF16),
                      jax.ShapeDtypeStruct((n_rows // CHUNK * SUBLANES, D_INNER), F32), tail_shape]
    return pl.pallas_call(
        functools.partial(_ssm_in_kernel, short=short),
        grid=grid,
        in_specs=in_specs,
        out_specs=out_specs,
        out_shape=out_shape,
        scratch_shapes=[pltpu.VMEM((tm + SUBLANES, CONV_DIM), F32)],
        compiler_params=_params(sem),
        name="ssm_in_short" if short else "ssm_in_long",
    )(x, pre, *params)


def _ssd_long(x, z, xbc, cum, cspt, we, da, s0, w, layer, *, n_seq, tm, shared_s0):
    expand, dskip, nw, w_out = w["expand"], w["dskip"], w["ssm_nw"], w["ssm_out"]
    n_tok, d = x.shape
    tiles = n_tok // n_seq // tm
    tok = lambda b, c: (b * tiles + c, 0)
    tok_t = lambda b, c: (0, b * tiles + c)
    s_spec = pl.BlockSpec((D_INNER, D_STATE), lambda b, c: (b, 0))
    s0_spec = pl.BlockSpec((D_INNER, D_STATE), lambda b, c: (0, 0)) if shared_s0 else s_spec
    return pl.pallas_call(
        _ssd_long_kernel,
        grid=(n_seq, tiles),
        in_specs=[pl.BlockSpec((tm, d), tok), pl.BlockSpec((tm, D_INNER), tok),
                  pl.BlockSpec((tm, CONV_DIM), tok), pl.BlockSpec((tm, LANES), tok),
                  pl.BlockSpec((LANES, tm), tok_t), pl.BlockSpec((tm, 2 * LANES), tok),
                  pl.BlockSpec((tm // CHUNK * SUBLANES, D_INNER), tok), s0_spec,
                  _const_spec(expand.shape), _const_spec(dskip.shape), _const_spec(nw.shape), _layer_spec(w_out, layer)],
        out_specs=[pl.BlockSpec((tm, d), tok), s_spec],
        out_shape=[jax.ShapeDtypeStruct((n_tok, d), F32), jax.ShapeDtypeStruct((n_seq * D_INNER, D_STATE), F32)],
        scratch_shapes=[pltpu.VMEM((D_STATE, D_INNER), F32)],
        compiler_params=_params(("arbitrary", "arbitrary")),
        name="ssd_long",
    )(x, z, xbc, cum, cspt, we, da, s0, expand, dskip, nw, w_out)


def _ssd_short(x, z, xbc, cum, csp, we, da, s0, w, layer, *, tm):
    expand, dskip, nw, w_out = w["expand"], w["dskip"], w["ssm_nw"], w["ssm_out"]
    n_tok, d = x.shape
    n_seq = s0.shape[0]
    tok = lambda i: (i, 0)
    s_spec = pl.BlockSpec((tm // SHORT_LEN, D_INNER, D_STATE), lambda i: (i, 0, 0))
    return pl.pallas_call(
        _ssd_short_kernel,
        grid=(n_tok // tm,),
        in_specs=[pl.BlockSpec((tm, d), tok), pl.BlockSpec((tm, D_INNER), tok), pl.BlockSpec((tm, CONV_DIM), tok),
                  pl.BlockSpec((tm, LANES), tok), pl.BlockSpec((tm, LANES), tok), pl.BlockSpec((tm, 2 * LANES), tok),
                  pl.BlockSpec((tm // SHORT_LEN, LANES), tok), pl.BlockSpec(memory_space=pl.ANY),
                  _const_spec(expand.shape), _const_spec(dskip.shape), _const_spec(nw.shape), _layer_spec(w_out, layer)],
        out_specs=[pl.BlockSpec((tm, d), tok), s_spec],
        out_shape=[jax.ShapeDtypeStruct((n_tok, d), F32), jax.ShapeDtypeStruct((n_seq, D_INNER, D_STATE), F32)],
        scratch_shapes=[pltpu.VMEM((STATE_SLOTS, tm // SHORT_LEN, D_INNER, D_STATE), F32),
                        pltpu.SemaphoreType.DMA((STATE_SLOTS,))],
        compiler_params=_params(("arbitrary",)),
        name="ssd_short",
    )(x, z, xbc, cum, csp, we, da, s0, expand, dskip, nw, w_out)


def _trunk_single(x, pre_sc, pre_xbc, s0, pre_ffn, w, *, tm):
    kw = dict(n_seq=1, short=False, shared_pre=False, tm=tm)
    x1, t_sc = _sc_mixer(x, pre_sc, w, 0, **kw)
    x2, t_f0 = _conv_ffn(x1, pre_ffn[0], w, 0, final_norm=False, **kw)
    z, xbc, cum, cspt, we, da, t_xbc = _ssm_in(x2, pre_xbc, w, 0, **kw)
    pad_front = cum.shape[0] - x2.shape[0]
    padf = lambda t: jnp.pad(t, ((pad_front, 0), (0, 0))) if pad_front else t
    x3, s_new = _ssd_long(padf(x2), padf(z), padf(xbc), cum, cspt, we, da, s0, w, 0, n_seq=1,
                          tm=max(tm, CHUNK), shared_s0=False)
    y, t_f1 = _conv_ffn(x3[pad_front:], pre_ffn[1], w, 1, final_norm=True, **kw)
    return y, t_sc, t_xbc, s_new, (t_f0, t_f1)


def _trunk_pair(x_long, start, x_short, caches, states_short, w, *, n_seq_long, tm, tm_sc, tm_ssd_short):
    m_sc, m_xbc, m_state, m_ffn = start
    cache_sc, cache_xbc, cache_ffn = caches
    n_seq_short = x_short.shape[0] // SHORT_LEN
    long_kw = dict(n_seq=n_seq_long, short=False, shared_pre=True)
    short_kw = dict(n_seq=n_seq_short, short=True, shared_pre=False, tm=tm)
    x1l, sc_l = _sc_mixer(x_long, m_sc, w, 0, tm=tm_sc, **long_kw)
    x1s, sc_s = _sc_mixer(x_short, cache_sc, w, 0, **short_kw)
    x2l, f0_l, x2s, f0_s = _conv_ffn_both(x1l, m_ffn[0], x1s, cache_ffn, w, 0, n_seq_long=n_seq_long, tm=tm,
                                          tm_short=tm // 2, final_norm=False)
    zl, xbcl, cuml, csptl, wel, dal, xbc_l = _ssm_in(x2l, m_xbc, w, 0, tm=tm, **long_kw)
    zs, xbcs, cums, csps, wes, das, xbc_s = _ssm_in(x2s, cache_xbc, w, 0, **short_kw)
    x3l, state_l = _ssd_long(x2l, zl, xbcl, cuml, csptl, wel, dal, m_state, w, 0, n_seq=n_seq_long, tm=tm,
                             shared_s0=True)
    x3s, state_s = _ssd_short(x2s, zs, xbcs, cums, csps, wes, das, states_short, w, 0, tm=tm_ssd_short)
    yl, f1_l, ys, ffn_s = _conv_ffn_both(x3l, m_ffn[1], x3s, cache_ffn, w, 1, n_seq_long=n_seq_long, tm=tm,
                                         tm_short=tm // 2, final_norm=True, prev_tails=f0_s[None])
    return (yl, sc_l, xbc_l, state_l, (f0_l, f1_l)), (ys, sc_s, xbc_s, state_s, ffn_s)


def kernel(x_prompt, x_sample, cache_sc, cache_ssm_conv, state_ssm, cache_ffn_conv, meta_tokens, norm_mix, norm_ffn, norm_final, sc_w_in, sc_conv_w, sc_w_out, ssm_w_in, ssm_conv_w, ssm_conv_b, ssm_dt_bias, ssm_a_log, ssm_d, ssm_norm_w, ssm_w_out, ffn_w_up, ffn_w_gate, ffn_conv_w, ffn_w_down):
    b, seq, d = x_prompt.shape
    n_dec, dec_len, _ = x_sample.shape
    d_ff = ffn_w_up.shape[2]
    assert dec_len == SHORT_LEN and seq % CHUNK == 0 and N_META % SUBLANES == 0 and N_META <= CHUNK

    pad_heads = lambda v: jnp.pad(v.reshape(1, -1).astype(F32), ((0, 0), (0, LANES - N_HEADS)))
    head_of_lane = jnp.arange(D_INNER, dtype=jnp.int32)[None, :] // HEAD_DIM
    w = dict(
        norm_mix=norm_mix.reshape(-1, 1, d), norm_ffn=norm_ffn.reshape(-1, 1, d), nfin=norm_final.reshape(1, d),
        sc_in=sc_w_in.astype(BF16), sc_cw=sc_conv_w, sc_out=sc_w_out.astype(BF16),
        ssm_in=ssm_w_in.astype(BF16),
        w_dt=jnp.pad(ssm_w_in[0][:, D_INNER + CONV_DIM:], ((0, 0), (0, LANES - N_HEADS))).astype(BF16),
        ssm_cw=ssm_conv_w, ssm_cb=ssm_conv_b[0:1], dtb=pad_heads(ssm_dt_bias[0]), alog=pad_heads(ssm_a_log[0]),
        dskip=jnp.repeat(ssm_d[0], HEAD_DIM).reshape(1, -1), ssm_nw=ssm_norm_w[0:1], ssm_out=ssm_w_out.astype(BF16),
        expand=(jnp.arange(LANES, dtype=jnp.int32)[:, None] == head_of_lane).astype(BF16),
        up=ffn_w_up.astype(BF16), gate=ffn_w_gate.astype(BF16), ffn_cw=ffn_conv_w, down=ffn_w_down.astype(BF16),
    )

    zeros8 = lambda c: jnp.zeros((SUBLANES, c), F32)
    _, m_sc, m_xbc, m_state, m_ffn = _trunk_single(
        meta_tokens.astype(F32), zeros8(d), zeros8(CONV_DIM), jnp.zeros((D_INNER, D_STATE), F32),
        (zeros8(d_ff), zeros8(d_ff)), w, tm=N_META)

    (yp, p_sc, p_xbc, p_state, p_ffn), (ys, s_sc, s_xbc, s_state, s_ffn) = _trunk_pair(
        x_prompt.reshape(b * seq, d), (m_sc, m_xbc, m_state, m_ffn),
        x_sample.reshape(n_dec * dec_len, d), (cache_sc, cache_ssm_conv, cache_ffn_conv),
        state_ssm[0].reshape(n_dec, D_INNER, D_STATE), w,
        n_seq_long=b, tm=512, tm_sc=1024, tm_ssd_short=64)
    tail = lambda t, k: t.reshape(b, SUBLANES, -1)[:, SUBLANES - k:]
    out_prompt = (
        yp.reshape(b, seq, d),
        tail(p_sc, 2)[None], tail(p_xbc, 3)[None],
        p_state.reshape(1, b, N_HEADS, HEAD_DIM, D_STATE),
        jnp.stack([tail(p_ffn[0], 2), tail(p_ffn[1], 2)]),
    )

    out_sample = (
        ys.reshape(n_dec, dec_len, d),
        s_sc[None], s_xbc[None],
        s_state.reshape(1, n_dec, N_HEADS, HEAD_DIM, D_STATE),
        s_ffn,
    )
    return (out_prompt[0], out_sample[0]) + out_prompt[1:] + out_sample[1:]
```
